```python
import math
import jax, jax.numpy as jnp
from jax import lax
import numpy as np

D_MODEL = 1024
BATCH = 16
SEQ = 4096
DEPTH = 1

CHUNK = 64
Q_BLOCK = 128
HEAD_DIM = 64
N_FOX_HEADS = 8
FOX_WIDTH = N_FOX_HEADS * HEAD_DIM
N_DIFF_HEADS = 4
DIFF_V_DIM = 2 * HEAD_DIM
DIFF_QK_WIDTH = N_DIFF_HEADS * 2 * HEAD_DIM
DIFF_WIDTH = N_DIFF_HEADS * DIFF_V_DIM
N_BRANCHES = 2
ROPE_THETA = 10000.0
COL_FQ = 0
COL_FK = COL_FQ + FOX_WIDTH
COL_FV = COL_FK + FOX_WIDTH
COL_FF = COL_FV + FOX_WIDTH
COL_DQ = COL_FF + N_FOX_HEADS
COL_DK = COL_DQ + DIFF_QK_WIDTH
COL_DV = COL_DK + DIFF_QK_WIDTH
COL_GATE = COL_DV + DIFF_WIDTH
IN_COLS = COL_GATE + N_BRANCHES * D_MODEL
N_GROUPS = 4
EXPERTS_PER_GROUP = 8
N_EXPERTS = N_GROUPS * EXPERTS_PER_GROUP
TOP_K = 2
D_EXPERT = D_MODEL // 2
MOE_BLOCK = 256
EPS = 1e-6
NEG_INF = -1e30

kernel_name = 'chunk_causal_fox_diffattn_hmoe_adaln_block'


def lambda_init(layer_idx):
    return 0.8 - 0.6 * math.exp(-0.3 * layer_idx)


def rms_norm(x, g):
    xf = x.astype(jnp.float32)
    y = xf * lax.rsqrt(jnp.mean(xf * xf, axis=-1, keepdims=True) + EPS)
    return (y * g.astype(jnp.float32)).astype(x.dtype)


def rope(x, cos, sin):
    x1, x2 = jnp.split(x, 2, axis=-1)
    cos = cos.astype(x.dtype)
    sin = sin.astype(x.dtype)
    return jnp.concatenate([x1 * cos - x2 * sin, x2 * cos + x1 * sin], axis=-1)


def split_heads(t, n, d):
    b, s, _ = t.shape
    return t.reshape(b, s, n, d).transpose(0, 2, 1, 3)


def forgetting_attention(q, k, v, log_f):
    s_len = q.shape[2]
    cum = jnp.cumsum(log_f, axis=-1)
    scale = HEAD_DIM ** -0.5
    outs = []
    for i in range(s_len // Q_BLOCK):
        q0, q1 = i * Q_BLOCK, (i + 1) * Q_BLOCK
        logits = jnp.einsum('bhqd,bhkd->bhqk', q[:, :, q0:q1], k[:, :, :q1]).astype(jnp.float32) * scale
        logits = logits + cum[:, :, q0:q1, None] - cum[:, :, None, :q1]
        t_idx = jnp.arange(q0, q1)[:, None]
        s_idx = jnp.arange(q1)[None, :]
        logits = jnp.where(s_idx <= t_idx, logits, NEG_INF)
        p = jax.nn.softmax(logits, axis=-1)
        outs.append(jnp.einsum('bhqk,bhkd->bhqd', p.astype(v.dtype), v[:, :, :q1]))
    return jnp.concatenate(outs, axis=2)


def differential_attention(q, k, v, lam):
    s_len = q.shape[3]
    scale = HEAD_DIM ** -0.5
    outs = []
    for i in range(s_len // Q_BLOCK):
        q0, q1 = i * Q_BLOCK, (i + 1) * Q_BLOCK
        logits = jnp.einsum('bhmqd,bhmkd->bhmqk', q[:, :, :, q0:q1], k[:, :, :, :q1]).astype(jnp.float32) * scale
        t_chunk = (jnp.arange(q0, q1) // CHUNK)[:, None]
        s_chunk = (jnp.arange(q1) // CHUNK)[None, :]
        logits = jnp.where(s_chunk <= t_chunk, logits, NEG_INF)
        p = jax.nn.softmax(logits, axis=-1)
        attn = p[:, :, 0] - lam * p[:, :, 1]
        outs.append(jnp.einsum('bhqk,bhkd->bhqd', attn.astype(v.dtype), v[:, :, :q1]))
    return jnp.concatenate(outs, axis=2)


def hierarchical_route(h, w_rg, b_rg, w_re, b_re):
    t = h.shape[0]
    hf = h.astype(jnp.float32)
    p_group = jax.nn.softmax(hf @ w_rg.astype(jnp.float32) + b_rg.astype(jnp.float32), axis=-1)
    g_w, g_idx = lax.top_k(p_group, 1)
    e_logits = (hf @ w_re.astype(jnp.float32) + b_re.astype(jnp.float32)).reshape(t, N_GROUPS, EXPERTS_PER_GROUP)
    e_logits = jnp.take_along_axis(e_logits, g_idx[:, :, None], axis=1)[:, 0]
    p_exp = jax.nn.softmax(e_logits, axis=-1)
    e_w, e_idx = lax.top_k(p_exp, TOP_K)
    e_w = e_w / jnp.sum(e_w, axis=-1, keepdims=True)
    weights = g_w * e_w
    expert_ids = (g_idx * EXPERTS_PER_GROUP + e_idx).astype(jnp.int32)
    return expert_ids, weights


def moe_forward(h, expert_ids, weights, w1, w3, w2):
    t, d = h.shape
    a = t * TOP_K
    n_blocks = -(-a // MOE_BLOCK) + N_EXPERTS
    p_rows = n_blocks * MOE_BLOCK
    flat_e = expert_ids.reshape(a)
    flat_tok = jnp.arange(a, dtype=jnp.int32) // TOP_K
    order = jnp.argsort(flat_e)
    sorted_e = flat_e[order]
    counts = jnp.bincount(flat_e, length=N_EXPERTS).astype(jnp.int32)
    padded = ((counts + MOE_BLOCK - 1) // MOE_BLOCK) * MOE_BLOCK
    pad_end = jnp.cumsum(padded)
    pad_start = pad_end - padded
    start = jnp.cumsum(counts) - counts
    dest = pad_start[sorted_e] + (jnp.arange(a, dtype=jnp.int32) - start[sorted_e])
    buf_tok = jnp.full((p_rows,), t, dtype=jnp.int32).at[dest].set(flat_tok[order])
    h_pad = jnp.concatenate([h, jnp.zeros((1, d), h.dtype)], axis=0)
    xb = h_pad[buf_tok].reshape(n_blocks, MOE_BLOCK, d)
    block_expert = jnp.minimum(
        jnp.searchsorted(pad_end, jnp.arange(n_blocks, dtype=jnp.int32) * MOE_BLOCK, side='right'),
        N_EXPERTS - 1).astype(jnp.int32)

    def expert_block(args):
        xblk, e = args
        return (jax.nn.silu(xblk @ w1[e]) * (xblk @ w3[e])) @ w2[e]

    yb = lax.map(expert_block, (xb, block_expert)).reshape(p_rows, d)
    dest_orig = jnp.zeros((a,), jnp.int32).at[order].set(dest)
    y_assign = yb[dest_orig].reshape(t, TOP_K, d)
    return jnp.einsum('tkd,tk->td', y_assign, weights.astype(h.dtype))


def setup_inputs(seed: int = 0) -> dict:
    key = jax.random.key(seed)
    ks = jax.random.split(key, 32)
    L, D = DEPTH, D_MODEL

    def nrm(k, shape, scale):
        return jax.random.normal(k, shape, jnp.float32) * scale

    offsets = jax.random.randint(ks[2], (BATCH, 1), 0, 64, dtype=jnp.int32) * CHUNK
    positions = (offsets + jnp.arange(SEQ, dtype=jnp.int32)[None, :]).astype(jnp.int32)
    return {
        'x': nrm(ks[0], (BATCH, SEQ, D), 1.0),
        'c': nrm(ks[1], (BATCH, D), 1.0),
        'positions': positions,
        'w_ada': nrm(ks[3], (L, D, 6 * D), D ** -0.5),
        'b_ada': nrm(ks[4], (L, 6 * D), 0.02),
        'g_norm1': 1.0 + nrm(ks[5], (L, D), 0.02),
        'w_in': nrm(ks[6], (L, D, IN_COLS), D ** -0.5),
        'b_f': 3.0 + nrm(ks[7], (L, N_FOX_HEADS), 0.5),
        'g_q_fox': 1.0 + nrm(ks[8], (L, HEAD_DIM), 0.02),
        'g_k_fox': 1.0 + nrm(ks[9], (L, HEAD_DIM), 0.02),
        'g_q_diff': 1.0 + nrm(ks[10], (L, HEAD_DIM), 0.02),
        'g_k_diff': 1.0 + nrm(ks[11], (L, HEAD_DIM), 0.02),
        'lam_q1': nrm(ks[12], (L, HEAD_DIM), 0.1),
        'lam_k1': nrm(ks[13], (L, HEAD_DIM), 0.1),
        'lam_q2': nrm(ks[14], (L, HEAD_DIM), 0.1),
        'lam_k2': nrm(ks[15], (L, HEAD_DIM), 0.1),
        'g_subln': 1.0 + nrm(ks[16], (L, DIFF_V_DIM), 0.02),
        'w_proj_fox': nrm(ks[17], (L, FOX_WIDTH, D), FOX_WIDTH ** -0.5),
        'w_proj_diff': nrm(ks[18], (L, DIFF_WIDTH, D), DIFF_WIDTH ** -0.5),
        'w_out': nrm(ks[19], (L, D, D), D ** -0.5),
        'g_norm2': 1.0 + nrm(ks[20], (L, D), 0.02),
        'w_router_group': nrm(ks[21], (L, D, N_GROUPS), D ** -0.5),
        'b_router_group': nrm(ks[22], (L, N_GROUPS), 0.01),
        'w_router_expert': nrm(ks[23], (L, D, N_EXPERTS), D ** -0.5),
        'b_router_expert': nrm(ks[24], (L, N_EXPERTS), 0.01),
        'w1': nrm(ks[25], (L, N_EXPERTS, D, D_EXPERT), D ** -0.5),
        'w3': nrm(ks[26], (L, N_EXPERTS, D, D_EXPERT), D ** -0.5),
        'w2': nrm(ks[27], (L, N_EXPERTS, D_EXPERT, D), D_EXPERT ** -0.5),
    }


def reference(x, c, positions, w_ada, b_ada, g_norm1, w_in, b_f, g_q_fox, g_k_fox,
              g_q_diff, g_k_diff, lam_q1, lam_k1, lam_q2, lam_k2, g_subln,
              w_proj_fox, w_proj_diff, w_out, g_norm2, w_router_group, b_router_group,
              w_router_expert, b_router_expert, w1, w3, w2):
    b, s, d = x.shape
    inv_freq = ROPE_THETA ** (-jnp.arange(0, HEAD_DIM, 2, dtype=jnp.float32) / HEAD_DIM)
    ang = positions.astype(jnp.float32)[..., None] * inv_freq
    cos = jnp.cos(ang)[:, None, None]
    sin = jnp.sin(ang)[:, None, None]
    c_act = jax.nn.silu(c)

    for l in range(DEPTH):
        lam0 = lambda_init(l)
        mod = c_act @ w_ada[l] + b_ada[l]
        sh1, sc1, gt1, sh2, sc2, gt2 = [m[:, None, :] for m in jnp.split(mod, 6, axis=-1)]

        h = rms_norm(x, g_norm1[l]) * (1.0 + sc1) + sh1
        z = h @ w_in[l]

        q_f = rms_norm(split_heads(z[..., COL_FQ:COL_FK], N_FOX_HEADS, HEAD_DIM), g_q_fox[l])
        k_f = rms_norm(split_heads(z[..., COL_FK:COL_FV], N_FOX_HEADS, HEAD_DIM), g_k_fox[l])
        v_f = split_heads(z[..., COL_FV:COL_FF], N_FOX_HEADS, HEAD_DIM)
        log_f = jax.nn.log_sigmoid((z[..., COL_FF:COL_DQ] + b_f[l]).astype(jnp.float32)).transpose(0, 2, 1)
        o_f = forgetting_attention(q_f, k_f, v_f, log_f)
        o_f = o_f.transpose(0, 2, 1, 3).reshape(b, s, FOX_WIDTH)

        q_d = z[..., COL_DQ:COL_DK].reshape(b, s, N_DIFF_HEADS, 2, HEAD_DIM).transpose(0, 2, 3, 1, 4)
        k_d = z[..., COL_DK:COL_DV].reshape(b, s, N_DIFF_HEADS, 2, HEAD_DIM).transpose(0, 2, 3, 1, 4)
        q_d = rope(rms_norm(q_d, g_q_diff[l]), cos, sin)
        k_d = rope(rms_norm(k_d, g_k_diff[l]), cos, sin)
        v_d = split_heads(z[..., COL_DV:COL_GATE], N_DIFF_HEADS, DIFF_V_DIM)
        lam = (jnp.exp(jnp.sum(lam_q1[l].astype(jnp.float32) * lam_k1[l].astype(jnp.float32)))
               - jnp.exp(jnp.sum(lam_q2[l].astype(jnp.float32) * lam_k2[l].astype(jnp.float32)))
               + lam0)
        o_d = differential_attention(q_d, k_d, v_d, lam)
        o_d = rms_norm(o_d, g_subln[l]) * (1.0 - lam0)
        o_d = o_d.transpose(0, 2, 1, 3).reshape(b, s, DIFF_WIDTH)

        gates = jax.nn.sigmoid(z[..., COL_GATE:]).reshape(b, s, N_BRANCHES, d)
        merged = gates[:, :, 0] * (o_f @ w_proj_fox[l]) + gates[:, :, 1] * (o_d @ w_proj_diff[l])
        x = x + gt1 * (merged @ w_out[l])

        h2 = (rms_norm(x, g_norm2[l]) * (1.0 + sc2) + sh2).reshape(b * s, d)
        expert_ids, weights = hierarchical_route(h2, w_router_group[l], b_router_group[l],
                                                 w_router_expert[l], b_router_expert[l])
        y_moe = moe_forward(h2, expert_ids, weights, w1[l], w3[l], w2[l]).reshape(b, s, d)
        x = x + gt2 * y_moe
    return x
```

```python
import functools
import math

import jax
import jax.numpy as jnp
from jax import lax
from jax.experimental import pallas as pl
from jax.experimental.pallas import tpu as pltpu

F32 = jnp.float32
BF16 = jnp.bfloat16
I32 = jnp.int32
U32 = jnp.uint32

D_MODEL = 1024
HEAD_DIM = 64
N_FOX_HEADS = 8
N_DIFF_HEADS = 4
FOX_WIDTH = 512
DIFF_WIDTH = 512
CHUNK = 64
ROPE_THETA = 10000.0
N_GROUPS = 4
EXPERTS_PER_GROUP = 8
N_EXPERTS = 32
TOP_K = 2
D_EXPERT = 512
MOE_BLOCK = 256
EPS = 1e-6
NEG_INF = -1e30
LAM0 = 0.8 - 0.6 * math.exp(-0.3 * 0)

LANES = 128
VMEM_LIMIT = 56 * 1024 * 1024

_NT = (((1,), (1,)), ((), ()))


def _cparams(*sem):
    return pltpu.CompilerParams(dimension_semantics=sem, vmem_limit_bytes=VMEM_LIMIT)


def _split3(x):
    hi = x.astype(BF16)
    r1 = x - hi.astype(F32)
    mid = r1.astype(BF16)
    lo = (r1 - mid.astype(F32)).astype(BF16)
    return hi, mid, lo


def _ada_kernel(c_ref, w_ref, b_ref, o_ref):
    c = c_ref[...]
    ca = (c * jax.nn.sigmoid(c)).astype(BF16)
    o_ref[...] = jnp.dot(ca, w_ref[...].astype(BF16), preferred_element_type=F32) + b_ref[...]


def _ada(c, w_ada, b_ada):
    b, d = c.shape
    n = w_ada.shape[1]
    tn = 1024
    return pl.pallas_call(
        _ada_kernel,
        out_shape=jax.ShapeDtypeStruct((b, n), F32),
        grid=(n // tn,),
        in_specs=[pl.BlockSpec((b, d), lambda j: (0, 0)),
                  pl.BlockSpec((d, tn), lambda j: (0, j)),
                  pl.BlockSpec((1, tn), lambda j: (0, j))],
        out_specs=pl.BlockSpec((b, tn), lambda j: (0, j)),
        compiler_params=_cparams("arbitrary"),
        name="ada",
    )(c, w_ada, b_ada.reshape(1, n))


def _rope_kernel(pos_ref, invf_ref, cos_ref, sin_ref):
    ang = pos_ref[...].astype(F32) * invf_ref[...]
    lane = lax.broadcasted_iota(I32, ang.shape, 1)
    s = jnp.sin(ang)
    cos_ref[...] = jnp.cos(ang)
    sin_ref[...] = jnp.where((lane & (HEAD_DIM - 1)) < HEAD_DIM // 2, -s, s)


def _rope_tables(pos_col, invf_row, tm):
    t = pos_col.shape[0]
    return pl.pallas_call(
        _rope_kernel,
        out_shape=(jax.ShapeDtypeStruct((t, LANES), F32), jax.ShapeDtypeStruct((t, LANES), F32)),
        grid=(t // tm,),
        in_specs=[pl.BlockSpec((tm, 1), lambda i: (i, 0)),
                  pl.BlockSpec((1, LANES), lambda i: (0, 0))],
        out_specs=(pl.BlockSpec((tm, LANES), lambda i: (i, 0)),
                   pl.BlockSpec((tm, LANES), lambda i: (i, 0))),
        compiler_params=_cparams("arbitrary"),
        name="rope",
    )(pos_col, invf_row)


def _rms_rows(x, g):
    return x * lax.rsqrt(jnp.mean(x * x, axis=-1, keepdims=True) + EPS) * g


def _head_norm(z, bd, g):
    ss = jnp.dot((z * z).astype(BF16), bd, preferred_element_type=F32)
    return z * lax.rsqrt(ss * (1.0 / HEAD_DIM) + EPS) * g


def _rotate_half(z):
    n = z.shape[-1]
    lane = lax.broadcasted_iota(I32, z.shape, 1)
    fwd = pltpu.roll(z, n - HEAD_DIM // 2, 1)
    bwd = pltpu.roll(z, HEAD_DIM // 2, 1)
    return jnp.where((lane & (HEAD_DIM - 1)) < HEAD_DIM // 2, fwd, bwd)


def _inproj_kernel(x_ref, mod_ref, g1_ref, wqkv_ref, wff_ref, wg_ref, bd_ref, gv_ref, bf_ref,
                   cos_ref, sin_ref,
                   fq_ref, fk_ref, fv_ref, dq_ref, dk_ref, dv_ref, g0_ref, g1o_ref, lf_ref):
    x = x_ref[...]
    sh1 = mod_ref[0, 0:1, :]
    sc1 = mod_ref[0, 1:2, :]
    h = (_rms_rows(x, g1_ref[...]) * (1.0 + sc1) + sh1).astype(BF16)
    bd = bd_ref[...]
    w = FOX_WIDTH

    def proj(j):
        return jnp.dot(h, wqkv_ref[:, j * w:(j + 1) * w], preferred_element_type=F32)

    fq_ref[...] = (_head_norm(proj(0), bd, gv_ref[0:1, :]) * (HEAD_DIM ** -0.5)).astype(BF16)
    fk_ref[...] = _head_norm(proj(1), bd, gv_ref[1:2, :]).astype(BF16)
    fv_ref[...] = proj(2).astype(BF16)

    cos = jnp.concatenate([cos_ref[...]] * (w // LANES), axis=1)
    sin = jnp.concatenate([sin_ref[...]] * (w // LANES), axis=1)
    qn = _head_norm(proj(3), bd, gv_ref[2:3, :])
    dq_ref[...] = ((qn * cos + _rotate_half(qn) * sin) * (HEAD_DIM ** -0.5)).astype(BF16)
    kn = _head_norm(proj(4), bd, gv_ref[3:4, :])
    dk_ref[...] = (kn * cos + _rotate_half(kn) * sin).astype(BF16)
    dv_ref[...] = proj(5).astype(BF16)

    for j in range(2):
        zg = jnp.dot(h, wg_ref[:, j * D_MODEL:(j + 1) * D_MODEL], preferred_element_type=F32)
        (g0_ref, g1o_ref)[j][...] = jax.nn.sigmoid(zg).astype(BF16)

    zf = lax.dot_general(wff_ref[...], h, _NT, preferred_element_type=F32) + bf_ref[...]
    lf_ref[...] = jnp.minimum(zf, 0.0) - jnp.log1p(jnp.exp(-jnp.abs(zf)))


def _inproj(x2, mod3, g1, wqkv, wff_t, wgate, bd, gvecs, bf_col, cos_t, sin_t, tm, seq):
    t, d = x2.shape
    tps = seq // tm
    row = lambda i: (i, 0)
    full = lambda i: (0, 0)
    bsd = lambda n: jax.ShapeDtypeStruct((t, n), BF16)
    return pl.pallas_call(
        _inproj_kernel,
        out_shape=(bsd(512), bsd(512), bsd(512), bsd(512), bsd(512), bsd(512), bsd(d), bsd(d),
                   jax.ShapeDtypeStruct((16, t), F32)),
        grid=(t // tm,),
        in_specs=[pl.BlockSpec((tm, d), row),
                  pl.BlockSpec((1, 6, d), lambda i: (i // tps, 0, 0)),
                  pl.BlockSpec((1, d), full),
                  pl.BlockSpec(wqkv.shape, full),
                  pl.BlockSpec(wff_t.shape, full),
                  pl.BlockSpec(wgate.shape, full),
                  pl.BlockSpec(bd.shape, full),
                  pl.BlockSpec(gvecs.shape, full),
                  pl.BlockSpec(bf_col.shape, full),
                  pl.BlockSpec((tm, LANES), row),
                  pl.BlockSpec((tm, LANES), row)],
        out_specs=tuple([pl.BlockSpec((tm, 512), row)] * 6 + [pl.BlockSpec((tm, d), row)] * 2
                        + [pl.BlockSpec((16, tm), lambda i: (0, i))]),
        compiler_params=_cparams("arbitrary"),
        name="inproj",
    )(x2, mod3, g1, wqkv, wff_t, wgate, bd, gvecs, bf_col, cos_t, sin_t)


def _cumsum_kernel(lf_ref, tri_ref, eye_ref, row_ref, col_ref, *, seq):
    tri = tri_ref[...]
    eye = eye_ref[...]
    carry = jnp.zeros((16, 1), F32)
    for blk in range(seq // LANES):
        sl = slice(blk * LANES, (blk + 1) * LANES)
        hi, mid, lo = _split3(lf_ref[:, sl])
        c = (jnp.dot(hi, tri, preferred_element_type=F32)
             + jnp.dot(mid, tri, preferred_element_type=F32)
             + jnp.dot(lo, tri, preferred_element_type=F32)) + carry
        row_ref[0, :, sl] = c
        chi, cmid, clo = _split3(c)
        col_ref[0, sl, :] = (lax.dot_general(eye, chi, _NT, preferred_element_type=F32)
                             + lax.dot_general(eye, cmid, _NT, preferred_element_type=F32)
                             + lax.dot_general(eye, clo, _NT, preferred_element_type=F32))
        carry = c[:, LANES - 1:LANES]


def _cumsum(lf_t, batch, seq):
    tri = (jnp.arange(LANES)[:, None] <= jnp.arange(LANES)[None, :]).astype(BF16)
    eye = jnp.eye(LANES, dtype=BF16)
    return pl.pallas_call(
        functools.partial(_cumsum_kernel, seq=seq),
        out_shape=(jax.ShapeDtypeStruct((batch, 16, seq), F32),
                   jax.ShapeDtypeStruct((batch, seq, 16), F32)),
        grid=(batch,),
        in_specs=[pl.BlockSpec((16, seq), lambda b: (0, b)),
                  pl.BlockSpec((LANES, LANES), lambda b: (0, 0)),
                  pl.BlockSpec((LANES, LANES), lambda b: (0, 0))],
        out_specs=(pl.BlockSpec((1, 16, seq), lambda b: (b, 0, 0)),
                   pl.BlockSpec((1, seq, 16), lambda b: (b, 0, 0))),
        compiler_params=_cparams("arbitrary"),
        name="cumsum",
    )(lf_t, tri, eye)


def _softmax_step(q, k, v, bias_fn, mask, carry):
    m, l, acc = carry
    s = lax.dot_general(q, k, _NT, preferred_element_type=F32)
    if bias_fn is not None:
        s = bias_fn(s)
    if mask is not None:
        s = jnp.where(mask, s, NEG_INF)
    m_new = jnp.maximum(m, jnp.max(s, axis=-1, keepdims=True))
    alpha = jnp.exp(m - m_new)
    p = jnp.exp(s - m_new)
    l = alpha * l + jnp.sum(p, axis=-1, keepdims=True)
    acc = alpha * acc + jnp.dot(p.astype(BF16), v, preferred_element_type=F32)
    return m_new, l, acc


def _attend(q, load_kv, bias_of, qi, diag_mask, tq, dv):
    init = (jnp.full((tq, 1), NEG_INF, F32), jnp.zeros((tq, 1), F32), jnp.zeros((tq, dv), F32))

    def body(ki, carry):
        k, v = load_kv(ki)
        return _softmax_step(q, k, v, bias_of(ki), None, carry)

    carry = lax.fori_loop(0, qi, body, init)
    k, v = load_kv(qi)
    _, l, acc = _softmax_step(q, k, v, bias_of(qi), diag_mask, carry)
    return acc / l


def _fox_kernel(q_ref, k_ref, v_ref, crow_ref, ccol_ref, o_ref, *, tq, nk):
    hp = pl.program_id(1)
    qi = pl.program_id(2)
    row = lax.broadcasted_iota(I32, (tq, tq), 0)
    col = lax.broadcasted_iota(I32, (tq, tq), 1)
    causal = col <= row
    ccol = ccol_ref[0]
    lane16 = lax.broadcasted_iota(I32, ccol.shape, 1)
    outs = []
    for hh in range(2):
        h = 2 * hp + hh
        hs = slice(hh * HEAD_DIM, (hh + 1) * HEAD_DIM)
        q = q_ref[0, :, hs]
        cq = jnp.sum(jnp.where(lane16 == h, ccol, 0.0), axis=-1, keepdims=True)

        def load_kv(ki, hs=hs):
            k0 = pl.multiple_of(ki * tq, tq)
            return k_ref[0, pl.ds(k0, tq), hs], v_ref[0, pl.ds(k0, tq), hs]

        def bias_of(ki, h=h, cq=cq):
            ck = crow_ref[h * nk + ki]
            return lambda s: s + cq - ck

        outs.append(_attend(q, load_kv, bias_of, qi, causal, tq, HEAD_DIM))
    o_ref[0] = jnp.concatenate(outs, axis=-1).astype(BF16)


def _fox_attention(fq, fk, fv, crow, ccol, tq):
    b, s, _ = fq.shape
    nq = s // tq
    return pl.pallas_call(
        functools.partial(_fox_kernel, tq=tq, nk=nq),
        out_shape=jax.ShapeDtypeStruct((b, s, FOX_WIDTH), BF16),
        grid=(b, N_FOX_HEADS // 2, nq),
        in_specs=[pl.BlockSpec((1, tq, LANES), lambda bi, hp, qi: (bi, qi, hp)),
                  pl.BlockSpec((1, s, LANES), lambda bi, hp, qi: (bi, 0, hp)),
                  pl.BlockSpec((1, s, LANES), lambda bi, hp, qi: (bi, 0, hp)),
                  pl.BlockSpec((16 * nq, 1, tq), lambda bi, hp, qi: (bi, 0, 0)),
                  pl.BlockSpec((1, tq, 16), lambda bi, hp, qi: (bi, qi, 0))],
        out_specs=pl.BlockSpec((1, tq, LANES), lambda bi, hp, qi: (bi, qi, hp)),
        compiler_params=_cparams("arbitrary", "arbitrary", "arbitrary"),
        name="fox",
    )(fq, fk, fv, crow, ccol)


def _diff_kernel(q_ref, k_ref, v_ref, lam_ref, gs_ref, o_ref, *, tq):
    qi = pl.program_id(2)
    row = lax.broadcasted_iota(I32, (tq, tq), 0)
    col = lax.broadcasted_iota(I32, (tq, tq), 1)
    chunk_causal = (col // CHUNK) <= (row // CHUNK)
    outs = []
    for m in range(2):
        ms = slice(m * HEAD_DIM, (m + 1) * HEAD_DIM)
        q = q_ref[0, :, ms]

        def load_kv(ki, ms=ms):
            k0 = pl.multiple_of(ki * tq, tq)
            return k_ref[0, pl.ds(k0, tq), ms], v_ref[0, pl.ds(k0, tq), :]

        outs.append(_attend(q, load_kv, lambda ki: None, qi, chunk_causal, tq, 2 * HEAD_DIM))
    lv = lam_ref[...]
    lam = (jnp.exp(jnp.sum(lv[0:1] * lv[1:2], axis=-1, keepdims=True))
           - jnp.exp(jnp.sum(lv[2:3] * lv[3:4], axis=-1, keepdims=True)) + LAM0)
    o = outs[0] - lam * outs[1]
    o_ref[0] = (_rms_rows(o, gs_ref[...]) * (1.0 - LAM0)).astype(BF16)


def _diff_attention(dq, dk, dv, lam_vecs, g_subln, tq):
    b, s, _ = dq.shape
    nq = s // tq
    return pl.pallas_call(
        functools.partial(_diff_kernel, tq=tq),
        out_shape=jax.ShapeDtypeStruct((b, s, DIFF_WIDTH), BF16),
        grid=(b, N_DIFF_HEADS, nq),
        in_specs=[pl.BlockSpec((1, tq, LANES), lambda bi, h, qi: (bi, qi, h)),
                  pl.BlockSpec((1, s, LANES), lambda bi, h, qi: (bi, 0, h)),
                  pl.BlockSpec((1, s, LANES), lambda bi, h, qi: (bi, 0, h)),
                  pl.BlockSpec((4, HEAD_DIM), lambda bi, h, qi: (0, 0)),
                  pl.BlockSpec((1, LANES), lambda bi, h, qi: (0, 0))],
        out_specs=pl.BlockSpec((1, tq, LANES), lambda bi, h, qi: (bi, qi, h)),
        compiler_params=_cparams("arbitrary", "arbitrary", "arbitrary"),
        name="diff",
    )(dq, dk, dv, lam_vecs, g_subln)


def _merge_kernel(of_ref, od_ref, g0_ref, g1_ref, x_ref, mod_ref, g2_ref, wpf_ref, wpd_ref, wout_ref,
                  wrh_ref, wrl_ref, br_ref,
                  x1_ref, hp_ref, ids_ref, wts_ref, cnt_ref):
    i = pl.program_id(0)
    gt1 = mod_ref[0, 2:3, :]
    sh2 = mod_ref[0, 3:4, :]
    sc2 = mod_ref[0, 4:5, :]
    a = jnp.dot(of_ref[...], wpf_ref[...], preferred_element_type=F32)
    b = jnp.dot(od_ref[...], wpd_ref[...], preferred_element_type=F32)
    merged = g0_ref[...].astype(F32) * a + g1_ref[...].astype(F32) * b
    x1 = x_ref[...] + gt1 * jnp.dot(merged.astype(BF16), wout_ref[...], preferred_element_type=F32)
    x1_ref[...] = x1
    h2 = _rms_rows(x1, g2_ref[...]) * (1.0 + sc2) + sh2

    hi = h2.astype(BF16)
    hf = hi.astype(F32)
    half = D_MODEL // 2
    lo_bits = pltpu.bitcast(hf[:, :half], U32) >> 16
    hi_bits = pltpu.bitcast(hf[:, half:], U32) & jnp.uint32(0xFFFF0000)
    hp_ref[...] = lo_bits | hi_bits

    lo = (h2 - hf).astype(BF16)
    wrh = wrh_ref[...]
    logits = (jnp.dot(hi, wrh, preferred_element_type=F32)
              + jnp.dot(lo, wrh, preferred_element_type=F32)
              + jnp.dot(hi, wrl_ref[...], preferred_element_type=F32)) + br_ref[...]
    lg = logits[:, :LANES]
    le = logits[:, LANES:]
    lane = lax.broadcasted_iota(I32, lg.shape, 1)
    big = jnp.int32(1 << 20)

    def softmax_masked(z, mask):
        zm = jnp.where(mask, z, -jnp.inf)
        e = jnp.exp(zm - jnp.max(zm, axis=-1, keepdims=True))
        return e / jnp.sum(e, axis=-1, keepdims=True)

    def top1(p, mask):
        pm = jnp.where(mask, p, -1.0)
        best = jnp.max(pm, axis=-1, keepdims=True)
        idx = jnp.min(jnp.where(pm == best, lane, big), axis=-1, keepdims=True)
        return best, idx

    gmask = lane < N_GROUPS
    g_w, g_idx = top1(softmax_masked(lg, gmask), gmask)
    emask = (lane >> 3) == g_idx
    p_exp = softmax_masked(le, emask)
    p1, i1 = top1(p_exp, emask)
    p2, i2 = top1(p_exp, emask & (lane != i1))
    denom = p1 + p2
    w1 = g_w * (p1 / denom)
    w2 = g_w * (p2 / denom)
    ids_ref[...] = jnp.where(lane == 0, i1, i2)[:, :TOP_K]
    wts_ref[...] = jnp.where(lane == 0, w1, w2)[:, :TOP_K]

    onehot = ((lane == i1) | (lane == i2)).astype(F32)

    @pl.when(i == 0)
    def _():
        cnt_ref[...] = jnp.zeros_like(cnt_ref)

    cnt_ref[...] += jnp.sum(onehot, axis=0, keepdims=True)


def _merge(o_f, o_d, g0, g1, x2, mod3, g2, wpf, wpd, wout, wr_hi, wr_lo, br, tm, seq):
    t, d = x2.shape
    tps = seq // tm
    row = lambda i: (i, 0)
    full = lambda i: (0, 0)
    return pl.pallas_call(
        _merge_kernel,
        out_shape=(jax.ShapeDtypeStruct((t, d), F32),
                   jax.ShapeDtypeStruct((t, d // 2), U32),
                   jax.ShapeDtypeStruct((t, TOP_K), I32),
                   jax.ShapeDtypeStruct((t, TOP_K), F32),
                   jax.ShapeDtypeStruct((1, LANES), F32)),
        grid=(t // tm,),
        in_specs=[pl.BlockSpec((tm, 512), row), pl.BlockSpec((tm, 512), row),
                  pl.BlockSpec((tm, d), row), pl.BlockSpec((tm, d), row),
                  pl.BlockSpec((tm, d), row),
                  pl.BlockSpec((1, 6, d), lambda i: (i // tps, 0, 0)),
                  pl.BlockSpec((1, d), full),
                  pl.BlockSpec(wpf.shape, full), pl.BlockSpec(wpd.shape, full),
                  pl.BlockSpec(wout.shape, full),
                  pl.BlockSpec(wr_hi.shape, full), pl.BlockSpec(wr_lo.shape, full),
                  pl.BlockSpec(br.shape, full)],
        out_specs=(pl.BlockSpec((tm, d), row), pl.BlockSpec((tm, d // 2), row),
                   pl.BlockSpec((tm, TOP_K), row), pl.BlockSpec((tm, TOP_K), row),
                   pl.BlockSpec((1, LANES), full)),
        compiler_params=_cparams("arbitrary"),
        name="merge",
    )(o_f, o_d, g0, g1, x2, mod3, g2, wpf, wpd, wout, wr_hi, wr_lo, br)


def _rank_kernel(ids_ref, pstart_ref, ltri_ref, dest_ref, carry_ref):
    i = pl.program_id(0)

    @pl.when(i == 0)
    def _():
        carry_ref[...] = jnp.zeros_like(carry_ref)

    ids = ids_ref[...]
    lane = lax.broadcasted_iota(I32, (ids.shape[0], LANES), 1)
    oh0 = lane == ids[:, 0:1]
    oh1 = lane == ids[:, 1:2]
    both = (oh0 | oh1).astype(BF16)
    before = jnp.dot(ltri_ref[...], both, preferred_element_type=F32)
    base = before + carry_ref[...] + pstart_ref[...]
    d0 = jnp.sum(jnp.where(oh0, base, 0.0), axis=-1, keepdims=True)
    d1 = jnp.sum(jnp.where(oh1, base, 0.0), axis=-1, keepdims=True)
    dest_ref[...] = jnp.where(lane == 0, d0, d1)[:, :TOP_K].astype(I32)
    carry_ref[...] += jnp.sum(both.astype(F32), axis=0, keepdims=True)


def _rank(ids, pad_start_row, tt):
    t = ids.shape[0]
    ltri = (jnp.arange(tt)[:, None] > jnp.arange(tt)[None, :]).astype(BF16)
    return pl.pallas_call(
        _rank_kernel,
        out_shape=jax.ShapeDtypeStruct((t, TOP_K), I32),
        grid=(t // tt,),
        in_specs=[pl.BlockSpec((tt, TOP_K), lambda i: (i, 0)),
                  pl.BlockSpec((1, LANES), lambda i: (0, 0)),
                  pl.BlockSpec((tt, tt), lambda i: (0, 0))],
        out_specs=pl.BlockSpec((tt, TOP_K), lambda i: (i, 0)),
        scratch_shapes=[pltpu.VMEM((1, LANES), F32)],
        compiler_params=_cparams("arbitrary"),
        name="rank",
    )(ids, pad_start_row, ltri)


def _row_copy(src_ref, s, dst_ref, d, sem):
    return pltpu.make_async_copy(src_ref.at[pl.ds(s, 1)], dst_ref.at[pl.ds(d, 1)], sem)


def _dispatch_kernel(pend_ref, padded_ref, nused_ref, dest_ref, h_ref, xs_ref, zero_ref, sem, zsem,
                     *, td, n_blocks):
    i = pl.program_id(0)

    @pl.when(i == 0)
    def _():
        zero_ref[...] = jnp.zeros_like(zero_ref)

        def zero_block(start):
            cp = pltpu.make_async_copy(
                zero_ref, xs_ref.at[pl.ds(pl.multiple_of(start, MOE_BLOCK), MOE_BLOCK)], zsem)
            cp.start()
            cp.wait()

        for e in range(N_EXPERTS):
            @pl.when(padded_ref[e] > 0)
            def _(e=e):
                zero_block(pend_ref[e] - MOE_BLOCK)

            @pl.when(nused_ref[0] + e < n_blocks)
            def _(e=e):
                zero_block((nused_ref[0] + e) * MOE_BLOCK)

    def issue(t, _):
        _row_copy(h_ref, t, xs_ref, dest_ref[2 * t], sem).start()
        _row_copy(h_ref, t, xs_ref, dest_ref[2 * t + 1], sem).start()
        return 0

    lax.fori_loop(0, td, issue, 0)

    def drain(t, _):
        _row_copy(h_ref, 0, xs_ref, 0, sem).wait()
        _row_copy(h_ref, 0, xs_ref, 0, sem).wait()
        return 0

    lax.fori_loop(0, td, drain, 0)


def _dispatch(pad_end, padded, n_used, dest_flat, h_packed, p_rows, td):
    t, w = h_packed.shape
    grid_spec = pltpu.PrefetchScalarGridSpec(
        num_scalar_prefetch=3,
        grid=(t // td,),
        in_specs=[pl.BlockSpec((TOP_K * td,), lambda i, *_: (i,), memory_space=pltpu.SMEM),
                  pl.BlockSpec((td, w), lambda i, *_: (i, 0))],
        out_specs=pl.BlockSpec(memory_space=pl.ANY),
        scratch_shapes=[pltpu.VMEM((MOE_BLOCK, w), U32),
                        pltpu.SemaphoreType.DMA(()), pltpu.SemaphoreType.DMA(())],
    )
    return pl.pallas_call(
        functools.partial(_dispatch_kernel, td=td, n_blocks=p_rows // MOE_BLOCK),
        out_shape=jax.ShapeDtypeStruct((p_rows, w), U32),
        grid_spec=grid_spec,
        compiler_params=_cparams("arbitrary"),
        name="dispatch",
    )(pad_end, padded, n_used, dest_flat, h_packed)


def _expert_kernel(be_ref, nused_ref, xs_ref, w1_ref, w3_ref, w2_ref, y_ref):
    p = pl.program_id(0)

    @pl.when(p < nused_ref[0])
    def _():
        words = xs_ref[...]
        half = D_MODEL // 2
        xa = pltpu.bitcast(words << 16, F32).astype(BF16)
        xb = pltpu.bitcast(words & jnp.uint32(0xFFFF0000), F32).astype(BF16)

        def up(w_ref):
            return (jnp.dot(xa, w_ref[0, :half, :], preferred_element_type=F32)
                    + jnp.dot(xb, w_ref[0, half:, :], preferred_element_type=F32))

        g = up(w1_ref)
        u = up(w3_ref)
        act = (g * jax.nn.sigmoid(g) * u).astype(BF16)
        y_ref[...] = jnp.dot(act, w2_ref[0], preferred_element_type=F32)

    @pl.when(p >= nused_ref[0])
    def _():
        y_ref[...] = jnp.zeros_like(y_ref)


def _experts(block_expert, n_used, xs, w1, w3, w2):
    p_rows, w = xs.shape
    nb = p_rows // MOE_BLOCK
    d = w1.shape[1]
    blk = lambda p, be, nu: (p, 0)
    wsel = lambda p, be, nu: (be[jnp.minimum(p, nu[0] - 1)], 0, 0)
    grid_spec = pltpu.PrefetchScalarGridSpec(
        num_scalar_prefetch=2,
        grid=(nb,),
        in_specs=[pl.BlockSpec((MOE_BLOCK, w), blk),
                  pl.BlockSpec((1, d, D_EXPERT), wsel),
                  pl.BlockSpec((1, d, D_EXPERT), wsel),
                  pl.BlockSpec((1, D_EXPERT, d), wsel)],
        out_specs=pl.BlockSpec((MOE_BLOCK, d), blk),
    )
    return pl.pallas_call(
        _expert_kernel,
        out_shape=jax.ShapeDtypeStruct((p_rows, d), F32),
        grid_spec=grid_spec,
        compiler_params=_cparams("arbitrary"),
        name="experts",
    )(block_expert, n_used, xs, w1, w3, w2)


def _combine_kernel(dest_ref, wts_ref, x1_ref, mod_ref, yb_ref, o_ref, ybuf_ref, sem, *, tc):
    def issue(t, _):
        _row_copy(yb_ref, dest_ref[2 * t], ybuf_ref.at[0], t, sem).start()
        _row_copy(yb_ref, dest_ref[2 * t + 1], ybuf_ref.at[1], t, sem).start()
        return 0

    lax.fori_loop(0, tc, issue, 0)

    def drain(t, _):
        _row_copy(yb_ref, 0, ybuf_ref.at[0], 0, sem).wait()
        _row_copy(yb_ref, 0, ybuf_ref.at[1], 0, sem).wait()
        return 0

    lax.fori_loop(0, tc, drain, 0)
    gt2 = mod_ref[0, 5:6, :]
    wts = wts_ref[...]
    y = ybuf_ref[0] * wts[:, 0:1] + ybuf_ref[1] * wts[:, 1:2]
    o_ref[...] = x1_ref[...] + gt2 * y


def _combine(dest_flat, wts, x1, mod3, yb, tc, seq):
    t, d = x1.shape
    tps = seq // tc
    return pl.pallas_call(
        functools.partial(_combine_kernel, tc=tc),
        out_shape=jax.ShapeDtypeStruct((t, d), F32),
        grid=(t // tc,),
        in_specs=[pl.BlockSpec((TOP_K * tc,), lambda i: (i,), memory_space=pltpu.SMEM),
                  pl.BlockSpec((tc, TOP_K), lambda i: (i, 0)),
                  pl.BlockSpec((tc, d), lambda i: (i, 0)),
                  pl.BlockSpec((1, 6, d), lambda i: (i // tps, 0, 0)),
                  pl.BlockSpec(memory_space=pl.ANY)],
        out_specs=pl.BlockSpec((tc, d), lambda i: (i, 0)),
        scratch_shapes=[pltpu.VMEM((2, tc, d), F32), pltpu.SemaphoreType.DMA(())],
        compiler_params=_cparams("arbitrary"),
        name="combine",
    )(dest_flat, wts, x1, mod3, yb)


def _tile(n, pref):
    t = pref
    while n % t:
        t //= 2
    return t


def kernel(x, c, positions, w_ada, b_ada, g_norm1, w_in, b_f, g_q_fox, g_k_fox, g_q_diff, g_k_diff,
           lam_q1, lam_k1, lam_q2, lam_k2, g_subln, w_proj_fox, w_proj_diff, w_out, g_norm2,
           w_router_group, b_router_group, w_router_expert, b_router_expert, w1, w3, w2):
    b, s, d = x.shape
    t = b * s
    assert d == D_MODEL and w_ada.shape[0] == 1 and s % LANES == 0
    tm = _tile(s, 512)
    tq = _tile(s, 512)

    wi = w_in[0]
    c_fk, c_fv, c_ff = FOX_WIDTH, 2 * FOX_WIDTH, 3 * FOX_WIDTH
    c_dq = c_ff + N_FOX_HEADS
    c_gate = c_dq + 3 * 512
    wqkv = jnp.concatenate([wi[:, :c_ff], wi[:, c_dq:c_gate]], axis=1).astype(BF16)
    wff_t = jnp.zeros((16, d), BF16).at[:N_FOX_HEADS].set(wi[:, c_ff:c_dq].T.astype(BF16))
    wgate = wi[:, c_gate:].astype(BF16)
    bf_col = jnp.zeros((16, 1), F32).at[:N_FOX_HEADS, 0].set(b_f[0])
    head_id = jnp.arange(FOX_WIDTH) // HEAD_DIM
    bd = (head_id[:, None] == head_id[None, :]).astype(BF16)
    gvecs = jnp.stack([jnp.tile(g[0], FOX_WIDTH // HEAD_DIM)
                       for g in (g_q_fox, g_k_fox, g_q_diff, g_k_diff)])
    lam_vecs = jnp.stack([lam_q1[0], lam_k1[0], lam_q2[0], lam_k2[0]])
    inv_freq = ROPE_THETA ** (-jnp.arange(0, HEAD_DIM, 2, dtype=F32) / HEAD_DIM)
    invf_row = jnp.tile(inv_freq, LANES // (HEAD_DIM // 2)).reshape(1, LANES)
    wr = jnp.zeros((d, 2 * LANES), F32)
    wr = wr.at[:, :N_GROUPS].set(w_router_group[0]).at[:, LANES:LANES + N_EXPERTS].set(w_router_expert[0])
    wr_hi = wr.astype(BF16)
    wr_lo = (wr - wr_hi.astype(F32)).astype(BF16)
    br = jnp.zeros((1, 2 * LANES), F32)
    br = br.at[0, :N_GROUPS].set(b_router_group[0]).at[0, LANES:LANES + N_EXPERTS].set(b_router_expert[0])

    x2 = x.reshape(t, d)
    mod3 = _ada(c, w_ada[0], b_ada[0]).reshape(b, 6, d)
    cos_t, sin_t = _rope_tables(positions.reshape(t, 1), invf_row, tm)

    fq, fk, fv, dq, dk, dv, g0, g1, lf_t = _inproj(
        x2, mod3, g_norm1, wqkv, wff_t, wgate, bd, gvecs, bf_col, cos_t, sin_t, tm, s)
    crow, ccol = _cumsum(lf_t, b, s)
    nq = s // tq
    crow = crow.reshape(b * 16 * nq, 1, tq)
    r3 = lambda a: a.reshape(b, s, a.shape[-1])
    o_f = _fox_attention(r3(fq), r3(fk), r3(fv), crow, ccol, tq)
    o_d = _diff_attention(r3(dq), r3(dk), r3(dv), lam_vecs, g_subln, tq)

    x1, h_packed, ids, wts, counts = _merge(
        o_f.reshape(t, FOX_WIDTH), o_d.reshape(t, DIFF_WIDTH), g0, g1, x2, mod3, g_norm2,
        w_proj_fox[0].astype(BF16), w_proj_diff[0].astype(BF16), w_out[0].astype(BF16),
        wr_hi, wr_lo, br, tm, s)

    a = t * TOP_K
    n_blocks = -(-a // MOE_BLOCK) + N_EXPERTS
    p_rows = n_blocks * MOE_BLOCK
    cnt = counts[0, :N_EXPERTS].astype(I32)
    padded = ((cnt + MOE_BLOCK - 1) // MOE_BLOCK) * MOE_BLOCK
    pad_end = jnp.cumsum(padded).astype(I32)
    pad_start = pad_end - padded
    block_expert = jnp.minimum(
        jnp.searchsorted(pad_end, jnp.arange(n_blocks, dtype=I32) * MOE_BLOCK, side='right'),
        N_EXPERTS - 1).astype(I32)
    n_used = (pad_end[-1:] // MOE_BLOCK).astype(I32)
    pstart_row = jnp.zeros((1, LANES), F32).at[0, :N_EXPERTS].set(pad_start.astype(F32))

    dest = _rank(ids, pstart_row, _tile(t, 256)).reshape(a)
    xs = _dispatch(pad_end, padded, n_used, dest, h_packed, p_rows, _tile(t, 512))
    yb = _experts(block_expert, n_used, xs, w1[0].astype(BF16), w3[0].astype(BF16), w2[0].astype(BF16))
    out = _combine(dest, wts, x1, mod3, yb, _tile(s, 256), s)
    return out.reshape(b, s, d)
```

```python
import functools
import math

import jax
import jax.numpy as jnp
from jax import lax
from jax.experimental import pallas as pl
from jax.experimental.pallas import tpu as pltpu

F32 = jnp.float32
BF16 = jnp.bfloat16
I32 = jnp.int32
U32 = jnp.uint32

D_MODEL = 1024
HEAD_DIM = 64
N_FOX_HEADS = 8
N_DIFF_HEADS = 4
FOX_WIDTH = 512
DIFF_WIDTH = 512
CHUNK = 64
CHUNK_SHIFT = CHUNK.bit_length() - 1
ROPE_THETA = 10000.0
N_GROUPS = 4
EXPERTS_PER_GROUP = 8
N_EXPERTS = 32
TOP_K = 2
D_EXPERT = 512
MOE_BLOCK = 256
EPS = 1e-6
NEG_INF = -1e30
LAM0 = 0.8 - 0.6 * math.exp(-0.3 * 0)
LOG2E = math.log2(math.e)
Q_SCALE = HEAD_DIM ** -0.5 * LOG2E

LANES = 128
VMEM_LIMIT = 56 * 1024 * 1024

DMA_UNROLL = 8

_NT = (((1,), (1,)), ((), ()))


def _cparams(*sem):
    return pltpu.CompilerParams(dimension_semantics=sem, vmem_limit_bytes=VMEM_LIMIT)


def _split3(x):
    hi = x.astype(BF16)
    r1 = x - hi.astype(F32)
    mid = r1.astype(BF16)
    lo = (r1 - mid.astype(F32)).astype(BF16)
    return hi, mid, lo


def _ada_kernel(c_ref, w_ref, b_ref, o_ref):
    c = c_ref[...]
    ca = (c * jax.nn.sigmoid(c)).astype(BF16)
    o_ref[...] = jnp.dot(ca, w_ref[...].astype(BF16), preferred_element_type=F32) + b_ref[...]


def _ada(c, w_ada, b_ada):
    b, d = c.shape
    n = w_ada.shape[1]
    tn = 1024
    return pl.pallas_call(
        _ada_kernel,
        out_shape=jax.ShapeDtypeStruct((b, n), F32),
        grid=(n // tn,),
        in_specs=[pl.BlockSpec((b, d), lambda j: (0, 0)),
                  pl.BlockSpec((d, tn), lambda j: (0, j)),
                  pl.BlockSpec((1, tn), lambda j: (0, j))],
        out_specs=pl.BlockSpec((b, tn), lambda j: (0, j)),
        compiler_params=_cparams("arbitrary"),
        name="ada",
    )(c, w_ada, b_ada.reshape(1, n))


def _rope_kernel(pos_ref, invf_ref, cos_ref, sin_ref):
    ang = pos_ref[...].astype(F32) * invf_ref[...]
    lane = lax.broadcasted_iota(I32, ang.shape, 1)
    s = jnp.sin(ang)
    cos_ref[...] = jnp.cos(ang)
    sin_ref[...] = jnp.where((lane & (HEAD_DIM - 1)) < HEAD_DIM // 2, -s, s)


def _rope_tables(pos_col, invf_row, tm):
    t = pos_col.shape[0]
    return pl.pallas_call(
        _rope_kernel,
        out_shape=(jax.ShapeDtypeStruct((t, LANES), F32), jax.ShapeDtypeStruct((t, LANES), F32)),
        grid=(t // tm,),
        in_specs=[pl.BlockSpec((tm, 1), lambda i: (i, 0)),
                  pl.BlockSpec((1, LANES), lambda i: (0, 0))],
        out_specs=(pl.BlockSpec((tm, LANES), lambda i: (i, 0)),
                   pl.BlockSpec((tm, LANES), lambda i: (i, 0))),
        compiler_params=_cparams("arbitrary"),
        name="rope",
    )(pos_col, invf_row)


def _rms_rows(x, g):
    return x * lax.rsqrt(jnp.mean(x * x, axis=-1, keepdims=True) + EPS) * g


def _head_norm(z, bd, g):
    ss = jnp.dot((z * z).astype(BF16), bd, preferred_element_type=F32)
    return z * lax.rsqrt(ss * (1.0 / HEAD_DIM) + EPS) * g


def _rotate_half(z):
    n = z.shape[-1]
    lane = lax.broadcasted_iota(I32, z.shape, 1)
    fwd = pltpu.roll(z, n - HEAD_DIM // 2, 1)
    bwd = pltpu.roll(z, HEAD_DIM // 2, 1)
    return jnp.where((lane & (HEAD_DIM - 1)) < HEAD_DIM // 2, fwd, bwd)


def _inproj_kernel(x_ref, mod_ref, g1_ref, wqkv_ref, wff_ref, wg_ref, bd_ref, gv_ref, bf_ref,
                   cos_ref, sin_ref,
                   fq_ref, fk_ref, fv_ref, dq_ref, dk_ref, dv_ref, g0_ref, g1o_ref, lf_ref):
    x = x_ref[...]
    sh1 = mod_ref[0, 0:1, :]
    sc1 = mod_ref[0, 1:2, :]
    h = (_rms_rows(x, g1_ref[...]) * (1.0 + sc1) + sh1).astype(BF16)
    bd = bd_ref[...]
    w = FOX_WIDTH

    def proj(j):
        return jnp.dot(h, wqkv_ref[:, j * w:(j + 1) * w], preferred_element_type=F32)

    fq_ref[...] = (_head_norm(proj(0), bd, gv_ref[0:1, :]) * Q_SCALE).astype(BF16)
    fk_ref[...] = _head_norm(proj(1), bd, gv_ref[1:2, :]).astype(BF16)
    fv_ref[...] = proj(2).astype(BF16)

    cos = jnp.concatenate([cos_ref[...]] * (w // LANES), axis=1)
    sin = jnp.concatenate([sin_ref[...]] * (w // LANES), axis=1)
    qn = _head_norm(proj(3), bd, gv_ref[2:3, :])
    dq_ref[...] = ((qn * cos + _rotate_half(qn) * sin) * Q_SCALE).astype(BF16)
    kn = _head_norm(proj(4), bd, gv_ref[3:4, :])
    dk_ref[...] = (kn * cos + _rotate_half(kn) * sin).astype(BF16)
    dv_ref[...] = proj(5).astype(BF16)

    for j in range(2):
        zg = jnp.dot(h, wg_ref[:, j * D_MODEL:(j + 1) * D_MODEL], preferred_element_type=F32)
        (g0_ref, g1o_ref)[j][...] = jax.nn.sigmoid(zg).astype(BF16)

    zf = lax.dot_general(wff_ref[...], h, _NT, preferred_element_type=F32) + bf_ref[...]
    lf_ref[...] = jnp.minimum(zf, 0.0) - jnp.log1p(jnp.exp(-jnp.abs(zf)))


def _inproj(x2, mod3, g1, wqkv, wff_t, wgate, bd, gvecs, bf_col, cos_t, sin_t, tm, seq):
    t, d = x2.shape
    tps = seq // tm
    row = lambda i: (i, 0)
    full = lambda i: (0, 0)
    bsd = lambda n: jax.ShapeDtypeStruct((t, n), BF16)
    return pl.pallas_call(
        _inproj_kernel,
        out_shape=(bsd(512), bsd(512), bsd(512), bsd(512), bsd(512), bsd(512), bsd(d), bsd(d),
                   jax.ShapeDtypeStruct((16, t), F32)),
        grid=(t // tm,),
        in_specs=[pl.BlockSpec((tm, d), row),
                  pl.BlockSpec((1, 6, d), lambda i: (i // tps, 0, 0)),
                  pl.BlockSpec((1, d), full),
                  pl.BlockSpec(wqkv.shape, full),
                  pl.BlockSpec(wff_t.shape, full),
                  pl.BlockSpec(wgate.shape, full),
                  pl.BlockSpec(bd.shape, full),
                  pl.BlockSpec(gvecs.shape, full),
                  pl.BlockSpec(bf_col.shape, full),
                  pl.BlockSpec((tm, LANES), row),
                  pl.BlockSpec((tm, LANES), row)],
        out_specs=tuple([pl.BlockSpec((tm, 512), row)] * 6 + [pl.BlockSpec((tm, d), row)] * 2
                        + [pl.BlockSpec((16, tm), lambda i: (0, i))]),
        compiler_params=_cparams("arbitrary"),
        name="inproj",
    )(x2, mod3, g1, wqkv, wff_t, wgate, bd, gvecs, bf_col, cos_t, sin_t)


def _cumsum_kernel(lf_ref, tri_ref, eye_ref, sel_ref, ones_ref, dec_ref, *, seq):
    tri = tri_ref[...]
    eye = eye_ref[...]
    sel = sel_ref[...]
    carry = jnp.zeros((16, 1), F32)
    for blk in range(seq // LANES):
        sl = slice(blk * LANES, (blk + 1) * LANES)
        hi, mid, lo = _split3(lf_ref[:, sl])
        c = (jnp.dot(hi, tri, preferred_element_type=F32)
             + jnp.dot(mid, tri, preferred_element_type=F32)
             + jnp.dot(lo, tri, preferred_element_type=F32)) + carry
        pieces = jnp.concatenate(list(_split3(c * LOG2E)) + [jnp.zeros((16, LANES), BF16)], axis=0)
        cols = lax.dot_general(eye, pieces, _NT, preferred_element_type=F32).astype(BF16)
        dec_ref[0, sl, :] = (jnp.dot(cols, sel, preferred_element_type=F32) + ones_ref[...]).astype(BF16)
        carry = c[:, LANES - 1:LANES]


def _cumsum(lf_t, batch, seq):
    tri = (jnp.arange(LANES)[:, None] <= jnp.arange(LANES)[None, :]).astype(BF16)
    eye = jnp.eye(LANES, dtype=BF16)
    j, h = jnp.meshgrid(jnp.arange(3), jnp.arange(N_FOX_HEADS), indexing="ij")
    rows = (16 * j + h).reshape(-1)
    sel = jnp.zeros((4 * 16, LANES), F32)
    sel = sel.at[rows, (8 * h + j).reshape(-1)].set(1.0)
    sel = sel.at[rows, (HEAD_DIM + 8 * h + 3 + j).reshape(-1)].set(-1.0).astype(BF16)
    ones = jnp.zeros((1, LANES), F32)
    ones = ones.at[0, (8 * h + 3 + j).reshape(-1)].set(1.0).at[0, (HEAD_DIM + 8 * h + j).reshape(-1)].set(1.0)
    return pl.pallas_call(
        functools.partial(_cumsum_kernel, seq=seq),
        out_shape=jax.ShapeDtypeStruct((batch, seq, LANES), BF16),
        grid=(batch,),
        in_specs=[pl.BlockSpec((16, seq), lambda b: (0, b)),
                  pl.BlockSpec((LANES, LANES), lambda b: (0, 0)),
                  pl.BlockSpec((LANES, LANES), lambda b: (0, 0)),
                  pl.BlockSpec(sel.shape, lambda b: (0, 0)),
                  pl.BlockSpec((1, LANES), lambda b: (0, 0))],
        out_specs=pl.BlockSpec((1, seq, LANES), lambda b: (b, 0, 0)),
        compiler_params=_cparams("arbitrary"),
        name="cumsum",
    )(lf_t, tri, eye, sel, ones)


def _attend(n_chains, q_of, k_of, v_of, qi, diag_mask, s_a, s_b, m_ref, acc_ref):
    m_ref[...] = jnp.full(m_ref.shape, NEG_INF, F32)
    acc_ref[...] = jnp.zeros(acc_ref.shape, F32)

    def produce(ki, s_ref):
        for c in range(n_chains):
            s_ref[c] = lax.dot_general(q_of(c), k_of(c, ki), _NT, preferred_element_type=F32)

    def consume(ki, s_ref, mask):
        for c in range(n_chains):
            s = s_ref[c]
            if mask is not None:
                s = jnp.where(mask, s, NEG_INF)
            m = m_ref[c]
            m_new = jnp.maximum(m, jnp.max(s, axis=-1, keepdims=True))
            p = jnp.exp2(s - m_new).astype(BF16)
            acc_ref[c] = (jnp.exp2(m - m_new) * acc_ref[c]
                          + jnp.dot(p, v_of(c, ki), preferred_element_type=F32))
            m_ref[c] = m_new

    produce(0, s_a)

    def pair(j, _):
        produce(2 * j + 1, s_b)
        consume(2 * j, s_a, None)
        produce(2 * j + 2, s_a)
        consume(2 * j + 1, s_b, None)
        return 0

    lax.fori_loop(0, lax.shift_right_logical(qi, 1), pair, 0)
    odd = lax.bitwise_and(qi, 1)

    @pl.when(odd == 1)
    def _():
        produce(qi, s_b)
        consume(qi - 1, s_a, None)
        consume(qi, s_b, diag_mask)

    @pl.when(odd == 0)
    def _():
        consume(qi, s_a, diag_mask)


def _fox_kernel(q_ref, k_ref, v_ref, dec_ref, o_ref, qaug_ref, kaug_ref, vaug_ref, s_a, s_b,
                m_ref, acc_ref, *, tq):
    hp = pl.program_id(1)
    qi = pl.program_id(2)

    @pl.when(qi == 0)
    def _():
        seq = v_ref.shape[1]
        ones = jnp.ones((seq, HEAD_DIM), BF16)
        kdec = dec_ref[0, :, HEAD_DIM:].astype(F32)
        lane_head = lax.shift_right_logical(lax.broadcasted_iota(I32, kdec.shape, 1), 3)
        for hh in range(2):
            hs = slice(hh * HEAD_DIM, (hh + 1) * HEAD_DIM)
            vaug_ref[hh] = jnp.concatenate([v_ref[0, :, hs], ones], axis=-1)
            kd = jnp.where(lane_head == 2 * hp + hh, kdec, 0.0).astype(BF16)
            kaug_ref[hh] = jnp.concatenate([k_ref[0, :, hs], kd], axis=-1)

    qdec = dec_ref[0, pl.ds(pl.multiple_of(qi * tq, tq), tq), :HEAD_DIM]
    for hh in range(2):
        qaug_ref[hh] = jnp.concatenate([q_ref[0, :, hh * HEAD_DIM:(hh + 1) * HEAD_DIM], qdec], axis=-1)

    row = lax.broadcasted_iota(I32, (tq, tq), 0)
    col = lax.broadcasted_iota(I32, (tq, tq), 1)

    def tile(ref, c, ki):
        return ref[c, pl.ds(pl.multiple_of(ki * tq, tq), tq), :]

    _attend(2, lambda c: qaug_ref[c], functools.partial(tile, kaug_ref), functools.partial(tile, vaug_ref),
            qi, col <= row, s_a, s_b, m_ref, acc_ref)
    outs = []
    for c in range(2):
        acc = acc_ref[c]
        outs.append((acc / pltpu.roll(acc, HEAD_DIM, 1))[:, :HEAD_DIM])
    o_ref[0] = jnp.concatenate(outs, axis=-1).astype(BF16)


def _attn_scratch(tq, width):
    return [pltpu.VMEM((2, tq, tq), F32), pltpu.VMEM((2, tq, tq), F32),
            pltpu.VMEM((2, tq, 1), F32), pltpu.VMEM((2, tq, width), F32)]


def _fox_attention(fq, fk, fv, dec, tq):
    b, s, _ = fq.shape
    nq = s // tq
    return pl.pallas_call(
        functools.partial(_fox_kernel, tq=tq),
        out_shape=jax.ShapeDtypeStruct((b, s, FOX_WIDTH), BF16),
        grid=(b, N_FOX_HEADS // 2, nq),
        in_specs=[pl.BlockSpec((1, tq, LANES), lambda bi, hp, qi: (bi, qi, hp)),
                  pl.BlockSpec((1, s, LANES), lambda bi, hp, qi: (bi, 0, hp)),
                  pl.BlockSpec((1, s, LANES), lambda bi, hp, qi: (bi, 0, hp)),
                  pl.BlockSpec((1, s, LANES), lambda bi, hp, qi: (bi, 0, 0))],
        out_specs=pl.BlockSpec((1, tq, LANES), lambda bi, hp, qi: (bi, qi, hp)),
        scratch_shapes=[pltpu.VMEM((2, tq, 2 * HEAD_DIM), BF16),
                        pltpu.VMEM((2, s, 2 * HEAD_DIM), BF16),
                        pltpu.VMEM((2, s, 2 * HEAD_DIM), BF16)] + _attn_scratch(tq, 2 * HEAD_DIM),
        compiler_params=_cparams("arbitrary", "arbitrary", "arbitrary"),
        name="fox",
    )(fq, fk, fv, dec)


def _diff_kernel(q_ref, k_ref, v_ref, lam_ref, gs_ref, o_ref, vaug_ref, s_a, s_b, m_ref, acc_ref, *, tq):
    qi = pl.program_id(2)
    dv = 2 * HEAD_DIM

    @pl.when(qi == 0)
    def _():
        vaug_ref[...] = jnp.concatenate([v_ref[0], jnp.ones((v_ref.shape[1], dv), BF16)], axis=-1)

    row = lax.broadcasted_iota(I32, (tq, tq), 0)
    col = lax.broadcasted_iota(I32, (tq, tq), 1)
    chunk_causal = lax.shift_right_logical(col, CHUNK_SHIFT) <= lax.shift_right_logical(row, CHUNK_SHIFT)

    def q_of(c):
        return q_ref[0, :, c * HEAD_DIM:(c + 1) * HEAD_DIM]

    def k_of(c, ki):
        return k_ref[0, pl.ds(pl.multiple_of(ki * tq, tq), tq), c * HEAD_DIM:(c + 1) * HEAD_DIM]

    def v_of(c, ki):
        return vaug_ref[pl.ds(pl.multiple_of(ki * tq, tq), tq), :]

    _attend(2, q_of, k_of, v_of, qi, chunk_causal, s_a, s_b, m_ref, acc_ref)
    outs = [acc_ref[c][:, :dv] / acc_ref[c][:, dv:] for c in range(2)]
    lv = lam_ref[...]
    lam = (jnp.exp(jnp.sum(lv[0:1] * lv[1:2], axis=-1, keepdims=True))
           - jnp.exp(jnp.sum(lv[2:3] * lv[3:4], axis=-1, keepdims=True)) + LAM0)
    o = outs[0] - lam * outs[1]
    o_ref[0] = (_rms_rows(o, gs_ref[...]) * (1.0 - LAM0)).astype(BF16)


def _diff_attention(dq, dk, dv, lam_vecs, g_subln, tq):
    b, s, _ = dq.shape
    nq = s // tq
    return pl.pallas_call(
        functools.partial(_diff_kernel, tq=tq),
        out_shape=jax.ShapeDtypeStruct((b, s, DIFF_WIDTH), BF16),
        grid=(b, N_DIFF_HEADS, nq),
        in_specs=[pl.BlockSpec((1, tq, LANES), lambda bi, h, qi: (bi, qi, h)),
                  pl.BlockSpec((1, s, LANES), lambda bi, h, qi: (bi, 0, h)),
                  pl.BlockSpec((1, s, LANES), lambda bi, h, qi: (bi, 0, h)),
                  pl.BlockSpec((4, HEAD_DIM), lambda bi, h, qi: (0, 0)),
                  pl.BlockSpec((1, LANES), lambda bi, h, qi: (0, 0))],
        out_specs=pl.BlockSpec((1, tq, LANES), lambda bi, h, qi: (bi, qi, h)),
        scratch_shapes=[pltpu.VMEM((s, 4 * HEAD_DIM), BF16)] + _attn_scratch(tq, 4 * HEAD_DIM),
        compiler_params=_cparams("arbitrary", "arbitrary", "arbitrary"),
        name="diff",
    )(dq, dk, dv, lam_vecs, g_subln)


def _merge_kernel(of_ref, od_ref, g0_ref, g1_ref, x_ref, mod_ref, g2_ref, wpf_ref, wpd_ref, wout_ref,
                  wrh_ref, wrl_ref, br_ref,
                  x1_ref, hp_ref, ids_ref, wts_ref, cnt_ref):
    i = pl.program_id(0)
    gt1 = mod_ref[0, 2:3, :]
    sh2 = mod_ref[0, 3:4, :]
    sc2 = mod_ref[0, 4:5, :]
    a = jnp.dot(of_ref[...], wpf_ref[...], preferred_element_type=F32)
    b = jnp.dot(od_ref[...], wpd_ref[...], preferred_element_type=F32)
    merged = g0_ref[...].astype(F32) * a + g1_ref[...].astype(F32) * b
    x1 = x_ref[...] + gt1 * jnp.dot(merged.astype(BF16), wout_ref[...], preferred_element_type=F32)
    x1_ref[...] = x1
    h2 = _rms_rows(x1, g2_ref[...]) * (1.0 + sc2) + sh2

    hi = h2.astype(BF16)
    hf = hi.astype(F32)
    half = D_MODEL // 2
    lo_bits = pltpu.bitcast(hf[:, :half], U32) >> 16
    hi_bits = pltpu.bitcast(hf[:, half:], U32) & jnp.uint32(0xFFFF0000)
    hp_ref[...] = lo_bits | hi_bits

    lo = (h2 - hf).astype(BF16)
    wrh = wrh_ref[...]
    logits = (jnp.dot(hi, wrh, preferred_element_type=F32)
              + jnp.dot(lo, wrh, preferred_element_type=F32)
              + jnp.dot(hi, wrl_ref[...], preferred_element_type=F32)) + br_ref[...]
    lg = logits[:, :LANES]
    le = logits[:, LANES:]
    lane = lax.broadcasted_iota(I32, lg.shape, 1)
    big = jnp.int32(1 << 20)

    def softmax_masked(z, mask):
        zm = jnp.where(mask, z, -jnp.inf)
        e = jnp.exp(zm - jnp.max(zm, axis=-1, keepdims=True))
        return e / jnp.sum(e, axis=-1, keepdims=True)

    def top1(p, mask):
        pm = jnp.where(mask, p, -1.0)
        best = jnp.max(pm, axis=-1, keepdims=True)
        idx = jnp.min(jnp.where(pm == best, lane, big), axis=-1, keepdims=True)
        return best, idx

    gmask = lane < N_GROUPS
    g_w, g_idx = top1(softmax_masked(lg, gmask), gmask)
    emask = (lane >> 3) == g_idx
    p_exp = softmax_masked(le, emask)
    p1, i1 = top1(p_exp, emask)
    p2, i2 = top1(p_exp, emask & (lane != i1))
    denom = p1 + p2
    w1 = g_w * (p1 / denom)
    w2 = g_w * (p2 / denom)
    ids_ref[...] = jnp.where(lane == 0, i1, i2)[:, :TOP_K]
    wts_ref[...] = jnp.where(lane == 0, w1, w2)[:, :TOP_K]

    onehot = ((lane == i1) | (lane == i2)).astype(F32)

    @pl.when(i == 0)
    def _():
        cnt_ref[...] = jnp.zeros_like(cnt_ref)

    cnt_ref[...] += jnp.sum(onehot, axis=0, keepdims=True)


def _merge(o_f, o_d, g0, g1, x2, mod3, g2, wpf, wpd, wout, wr_hi, wr_lo, br, tm, seq):
    t, d = x2.shape
    tps = seq // tm
    row = lambda i: (i, 0)
    full = lambda i: (0, 0)
    return pl.pallas_call(
        _merge_kernel,
        out_shape=(jax.ShapeDtypeStruct((t, d), F32),
                   jax.ShapeDtypeStruct((t, d // 2), U32),
                   jax.ShapeDtypeStruct((t, TOP_K), I32),
                   jax.ShapeDtypeStruct((t, TOP_K), F32),
                   jax.ShapeDtypeStruct((1, LANES), F32)),
        grid=(t // tm,),
        in_specs=[pl.BlockSpec((tm, 512), row), pl.BlockSpec((tm, 512), row),
                  pl.BlockSpec((tm, d), row), pl.BlockSpec((tm, d), row),
                  pl.BlockSpec((tm, d), row),
                  pl.BlockSpec((1, 6, d), lambda i: (i // tps, 0, 0)),
                  pl.BlockSpec((1, d), full),
                  pl.BlockSpec(wpf.shape, full), pl.BlockSpec(wpd.shape, full),
                  pl.BlockSpec(wout.shape, full),
                  pl.BlockSpec(wr_hi.shape, full), pl.BlockSpec(wr_lo.shape, full),
                  pl.BlockSpec(br.shape, full)],
        out_specs=(pl.BlockSpec((tm, d), row), pl.BlockSpec((tm, d // 2), row),
                   pl.BlockSpec((tm, TOP_K), row), pl.BlockSpec((tm, TOP_K), row),
                   pl.BlockSpec((1, LANES), full)),
        compiler_params=_cparams("arbitrary"),
        name="merge",
    )(o_f, o_d, g0, g1, x2, mod3, g2, wpf, wpd, wout, wr_hi, wr_lo, br)


def _rank_kernel(ids_ref, pstart_ref, ltri_ref, dest_ref, carry_ref):
    i = pl.program_id(0)

    @pl.when(i == 0)
    def _():
        carry_ref[...] = jnp.zeros_like(carry_ref)

    ids = ids_ref[...]
    lane = lax.broadcasted_iota(I32, (ids.shape[0], LANES), 1)
    oh0 = lane == ids[:, 0:1]
    oh1 = lane == ids[:, 1:2]
    both = (oh0 | oh1).astype(BF16)
    before = jnp.dot(ltri_ref[...], both, preferred_element_type=F32)
    base = before + carry_ref[...] + pstart_ref[...]
    d0 = jnp.sum(jnp.where(oh0, base, 0.0), axis=-1, keepdims=True)
    d1 = jnp.sum(jnp.where(oh1, base, 0.0), axis=-1, keepdims=True)
    dest_ref[...] = jnp.where(lane == 0, d0, d1)[:, :TOP_K].astype(I32)
    carry_ref[...] += jnp.sum(both.astype(F32), axis=0, keepdims=True)


def _rank(ids, pad_start_row, tt):
    t = ids.shape[0]
    ltri = (jnp.arange(tt)[:, None] > jnp.arange(tt)[None, :]).astype(BF16)
    return pl.pallas_call(
        _rank_kernel,
        out_shape=jax.ShapeDtypeStruct((t, TOP_K), I32),
        grid=(t // tt,),
        in_specs=[pl.BlockSpec((tt, TOP_K), lambda i: (i, 0)),
                  pl.BlockSpec((1, LANES), lambda i: (0, 0)),
                  pl.BlockSpec((tt, tt), lambda i: (0, 0))],
        out_specs=pl.BlockSpec((tt, TOP_K), lambda i: (i, 0)),
        scratch_shapes=[pltpu.VMEM((1, LANES), F32)],
        compiler_params=_cparams("arbitrary"),
        name="rank",
    )(ids, pad_start_row, ltri)


def _row_copy(src_ref, s, dst_ref, d, sem):
    return pltpu.make_async_copy(src_ref.at[pl.ds(s, 1)], dst_ref.at[pl.ds(d, 1)], sem)


def _dispatch_kernel(pend_ref, padded_ref, nused_ref, dest_ref, h_ref, xs_ref, zero_ref, sem, zsem,
                     *, td, n_blocks):
    i = pl.program_id(0)

    @pl.when(i == 0)
    def _():
        zero_ref[...] = jnp.zeros_like(zero_ref)

        def zero_block(start):
            cp = pltpu.make_async_copy(
                zero_ref, xs_ref.at[pl.ds(pl.multiple_of(start, MOE_BLOCK), MOE_BLOCK)], zsem)
            cp.start()
            cp.wait()

        for e in range(N_EXPERTS):
            @pl.when(padded_ref[e] > 0)
            def _(e=e):
                zero_block(pend_ref[e] - MOE_BLOCK)

            @pl.when(nused_ref[0] + e < n_blocks)
            def _(e=e):
                zero_block((nused_ref[0] + e) * MOE_BLOCK)

    def issue(t, _):
        _row_copy(h_ref, t, xs_ref, dest_ref[2 * t], sem).start()
        _row_copy(h_ref, t, xs_ref, dest_ref[2 * t + 1], sem).start()
        return 0

    lax.fori_loop(0, td, issue, 0, unroll=DMA_UNROLL)

    def drain(t, _):
        _row_copy(h_ref, 0, xs_ref, 0, sem).wait()
        _row_copy(h_ref, 0, xs_ref, 0, sem).wait()
        return 0

    lax.fori_loop(0, td, drain, 0, unroll=DMA_UNROLL)


def _dispatch(pad_end, padded, n_used, dest_flat, h_packed, p_rows, td):
    t, w = h_packed.shape
    grid_spec = pltpu.PrefetchScalarGridSpec(
        num_scalar_prefetch=3,
        grid=(t // td,),
        in_specs=[pl.BlockSpec((TOP_K * td,), lambda i, *_: (i,), memory_space=pltpu.SMEM),
                  pl.BlockSpec((td, w), lambda i, *_: (i, 0))],
        out_specs=pl.BlockSpec(memory_space=pl.ANY),
        scratch_shapes=[pltpu.VMEM((MOE_BLOCK, w), U32),
                        pltpu.SemaphoreType.DMA(()), pltpu.SemaphoreType.DMA(())],
    )
    return pl.pallas_call(
        functools.partial(_dispatch_kernel, td=td, n_blocks=p_rows // MOE_BLOCK),
        out_shape=jax.ShapeDtypeStruct((p_rows, w), U32),
        grid_spec=grid_spec,
        compiler_params=_cparams("arbitrary"),
        name="dispatch",
    )(pad_end, padded, n_used, dest_flat, h_packed)


def _expert_kernel(be_ref, nused_ref, xs_ref, w1_ref, w3_ref, w2_ref, y_ref):
    p = pl.program_id(0)

    @pl.when(p < nused_ref[0])
    def _():
        words = xs_ref[...]
        half = D_MODEL // 2
        xa = pltpu.bitcast(words << 16, F32).astype(BF16)
        xb = pltpu.bitcast(words & jnp.uint32(0xFFFF0000), F32).astype(BF16)

        def up(w_ref):
            return (jnp.dot(xa, w_ref[0, :half, :], preferred_element_type=F32)
                    + jnp.dot(xb, w_ref[0, half:, :], preferred_element_type=F32))

        g = up(w1_ref)
        u = up(w3_ref)
        act = (g * jax.nn.sigmoid(g) * u).astype(BF16)
        y_ref[...] = jnp.dot(act, w2_ref[0], preferred_element_type=F32)

    @pl.when(p >= nused_ref[0])
    def _():
        y_ref[...] = jnp.zeros_like(y_ref)


def _experts(block_expert, n_used, xs, w1, w3, w2):
    p_rows, w = xs.shape
    nb = p_rows // MOE_BLOCK
    d = w1.shape[1]
    blk = lambda p, be, nu: (p, 0)
    wsel = lambda p, be, nu: (be[jnp.minimum(p, nu[0] - 1)], 0, 0)
    grid_spec = pltpu.PrefetchScalarGridSpec(
        num_scalar_prefetch=2,
        grid=(nb,),
        in_specs=[pl.BlockSpec((MOE_BLOCK, w), blk),
                  pl.BlockSpec((1, d, D_EXPERT), wsel),
                  pl.BlockSpec((1, d, D_EXPERT), wsel),
                  pl.BlockSpec((1, D_EXPERT, d), wsel)],
        out_specs=pl.BlockSpec((MOE_BLOCK, d), blk),
    )
    return pl.pallas_call(
        _expert_kernel,
        out_shape=jax.ShapeDtypeStruct((p_rows, d), F32),
        grid_spec=grid_spec,
        compiler_params=_cparams("arbitrary"),
        name="experts",
    )(block_expert, n_used, xs, w1, w3, w2)


def _combine_kernel(dest_ref, wts_ref, x1_ref, mod_ref, yb_ref, o_ref, ybuf_ref, sem, *, tc):
    def issue(t, _):
        _row_copy(yb_ref, dest_ref[2 * t], ybuf_ref.at[0], t, sem).start()
        _row_copy(yb_ref, dest_ref[2 * t + 1], ybuf_ref.at[1], t, sem).start()
        return 0

    lax.fori_loop(0, tc, issue, 0, unroll=DMA_UNROLL)

    def drain(t, _):
        _row_copy(yb_ref, 0, ybuf_ref.at[0], 0, sem).wait()
        _row_copy(yb_ref, 0, ybuf_ref.at[1], 0, sem).wait()
        return 0

    lax.fori_loop(0, tc, drain, 0, unroll=DMA_UNROLL)
    gt2 = mod_ref[0, 5:6, :]
    wts = wts_ref[...]
    y = ybuf_ref[0] * wts[:, 0:1] + ybuf_ref[1] * wts[:, 1:2]
    o_ref[...] = x1_ref[...] + gt2 * y


def _combine(dest_flat, wts, x1, mod3, yb, tc, seq):
    t, d = x1.shape
    tps = seq // tc
    return pl.pallas_call(
        functools.partial(_combine_kernel, tc=tc),
        out_shape=jax.ShapeDtypeStruct((t, d), F32),
        grid=(t // tc,),
        in_specs=[pl.BlockSpec((TOP_K * tc,), lambda i: (i,), memory_space=pltpu.SMEM),
                  pl.BlockSpec((tc, TOP_K), lambda i: (i, 0)),
                  pl.BlockSpec((tc, d), lambda i: (i, 0)),
                  pl.BlockSpec((1, 6, d), lambda i: (i // tps, 0, 0)),
                  pl.BlockSpec(memory_space=pl.ANY)],
        out_specs=pl.BlockSpec((tc, d), lambda i: (i, 0)),
        scratch_shapes=[pltpu.VMEM((2, tc, d), F32), pltpu.SemaphoreType.DMA(())],
        compiler_params=_cparams("arbitrary"),
        name="combine",
    )(dest_flat, wts, x1, mod3, yb)


def _tile(n, pref):
    t = pref
    while n % t:
        t //= 2
    return t


def kernel(x, c, positions, w_ada, b_ada, g_norm1, w_in, b_f, g_q_fox, g_k_fox, g_q_diff, g_k_diff,
           lam_q1, lam_k1, lam_q2, lam_k2, g_subln, w_proj_fox, w_proj_diff, w_out, g_norm2,
           w_router_group, b_router_group, w_router_expert, b_router_expert, w1, w3, w2):
    b, s, d = x.shape
    t = b * s
    assert d == D_MODEL and w_ada.shape[0] == 1 and s % LANES == 0
    tm = _tile(s, 512)
    tq = _tile(s, 512)

    wi = w_in[0]
    c_fk, c_fv, c_ff = FOX_WIDTH, 2 * FOX_WIDTH, 3 * FOX_WIDTH
    c_dq = c_ff + N_FOX_HEADS
    c_gate = c_dq + 3 * 512
    wqkv = jnp.concatenate([wi[:, :c_ff], wi[:, c_dq:c_gate]], axis=1).astype(BF16)
    wff_t = jnp.zeros((16, d), BF16).at[:N_FOX_HEADS].set(wi[:, c_ff:c_dq].T.astype(BF16))
    wgate = wi[:, c_gate:].astype(BF16)
    bf_col = jnp.zeros((16, 1), F32).at[:N_FOX_HEADS, 0].set(b_f[0])
    head_id = jnp.arange(FOX_WIDTH) // HEAD_DIM
    bd = (head_id[:, None] == head_id[None, :]).astype(BF16)
    gvecs = jnp.stack([jnp.tile(g[0], FOX_WIDTH // HEAD_DIM)
                       for g in (g_q_fox, g_k_fox, g_q_diff, g_k_diff)])
    lam_vecs = jnp.stack([lam_q1[0], lam_k1[0], lam_q2[0], lam_k2[0]])
    inv_freq = ROPE_THETA ** (-jnp.arange(0, HEAD_DIM, 2, dtype=F32) / HEAD_DIM)
    invf_row = jnp.tile(inv_freq, LANES // (HEAD_DIM // 2)).reshape(1, LANES)
    wr = jnp.zeros((d, 2 * LANES), F32)
    wr = wr.at[:, :N_GROUPS].set(w_router_group[0]).at[:, LANES:LANES + N_EXPERTS].set(w_router_expert[0])
    wr_hi = wr.astype(BF16)
    wr_lo = (wr - wr_hi.astype(F32)).astype(BF16)
    br = jnp.zeros((1, 2 * LANES), F32)
    br = br.at[0, :N_GROUPS].set(b_router_group[0]).at[0, LANES:LANES + N_EXPERTS].set(b_router_expert[0])

    x2 = x.reshape(t, d)
    mod3 = _ada(c, w_ada[0], b_ada[0]).reshape(b, 6, d)
    cos_t, sin_t = _rope_tables(positions.reshape(t, 1), invf_row, tm)

    fq, fk, fv, dq, dk, dv, g0, g1, lf_t = _inproj(
        x2, mod3, g_norm1, wqkv, wff_t, wgate, bd, gvecs, bf_col, cos_t, sin_t, tm, s)
    dec = _cumsum(lf_t, b, s)
    r3 = lambda a: a.reshape(b, s, a.shape[-1])
    o_f = _fox_attention(r3(fq), r3(fk), r3(fv), dec, tq)
    o_d = _diff_attention(r3(dq), r3(dk), r3(dv), lam_vecs, g_subln, tq)

    x1, h_packed, ids, wts, counts = _merge(
        o_f.reshape(t, FOX_WIDTH), o_d.reshape(t, DIFF_WIDTH), g0, g1, x2, mod3, g_norm2,
        w_proj_fox[0].astype(BF16), w_proj_diff[0].astype(BF16), w_out[0].astype(BF16),
        wr_hi, wr_lo, br, tm, s)

    a = t * TOP_K
    n_blocks = -(-a // MOE_BLOCK) + N_EXPERTS
    p_rows = n_blocks * MOE_BLOCK
    cnt = counts[0, :N_EXPERTS].astype(I32)
    padded = ((cnt + MOE_BLOCK - 1) // MOE_BLOCK) * MOE_BLOCK
    pad_end = jnp.cumsum(padded).astype(I32)
    pad_start = pad_end - padded
    block_start = jnp.arange(n_blocks, dtype=I32) * MOE_BLOCK
    block_expert = jnp.minimum(
        jnp.sum((pad_end[None, :] <= block_start[:, None]).astype(I32), axis=1), N_EXPERTS - 1)
    n_used = (pad_end[-1:] // MOE_BLOCK).astype(I32)
    pstart_row = jnp.zeros((1, LANES), F32).at[0, :N_EXPERTS].set(pad_start.astype(F32))

    dest = _rank(ids, pstart_row, _tile(t, 256)).reshape(a)
    xs = _dispatch(pad_end, padded, n_used, dest, h_packed, p_rows, _tile(t, 512))
    yb = _experts(block_expert, n_used, xs, w1[0].astype(BF16), w3[0].astype(BF16), w2[0].astype(BF16))
    out = _combine(dest, wts, x1, mod3, yb, _tile(s, 256), s)
    return out.reshape(b, s, d)
```

```python
import functools
import math

import jax
import jax.numpy as jnp
from jax import lax
from jax.experimental import pallas as pl
from jax.experimental.pallas import tpu as pltpu

F32 = jnp.float32
BF16 = jnp.bfloat16
I32 = jnp.int32
U32 = jnp.uint32

D_MODEL = 1024
HEAD_DIM = 64
N_FOX_HEADS = 8
N_DIFF_HEADS = 4
FOX_WIDTH = 512
DIFF_WIDTH = 512
CHUNK = 64
CHUNK_SHIFT = CHUNK.bit_length() - 1
ROPE_THETA = 10000.0
N_GROUPS = 4
EXPERTS_PER_GROUP = 8
N_EXPERTS = 32
TOP_K = 2
D_EXPERT = 512
MOE_BLOCK = 256
EPS = 1e-6
NEG_INF = -1e30
LAM0 = 0.8 - 0.6 * math.exp(-0.3 * 0)
LOG2E = math.log2(math.e)
Q_SCALE = HEAD_DIM ** -0.5 * LOG2E

LANES = 128
VMEM_LIMIT = 56 * 1024 * 1024

DMA_UNROLL = 8
ROW_CHAINS = 2

_NT = (((1,), (1,)), ((), ()))


def _cparams(*sem):
    return pltpu.CompilerParams(dimension_semantics=sem, vmem_limit_bytes=VMEM_LIMIT)


def _split3(x):
    hi = x.astype(BF16)
    r1 = x - hi.astype(F32)
    mid = r1.astype(BF16)
    lo = (r1 - mid.astype(F32)).astype(BF16)
    return hi, mid, lo


def _ada_kernel(c_ref, w_ref, b_ref, o_ref):
    c = c_ref[...]
    ca = (c * jax.nn.sigmoid(c)).astype(BF16)
    o_ref[...] = jnp.dot(ca, w_ref[...].astype(BF16), preferred_element_type=F32) + b_ref[...]


def _ada(c, w_ada, b_ada):
    b, d = c.shape
    n = w_ada.shape[1]
    tn = 1024
    return pl.pallas_call(
        _ada_kernel,
        out_shape=jax.ShapeDtypeStruct((b, n), F32),
        grid=(n // tn,),
        in_specs=[pl.BlockSpec((b, d), lambda j: (0, 0)),
                  pl.BlockSpec((d, tn), lambda j: (0, j)),
                  pl.BlockSpec((1, tn), lambda j: (0, j))],
        out_specs=pl.BlockSpec((b, tn), lambda j: (0, j)),
        compiler_params=_cparams("arbitrary"),
        name="ada",
    )(c, w_ada, b_ada.reshape(1, n))


def _rope_kernel(pos_ref, invf_ref, cos_ref, sin_ref):
    ang = pos_ref[...].astype(F32) * invf_ref[...]
    lane = lax.broadcasted_iota(I32, ang.shape, 1)
    s = jnp.sin(ang)
    cos_ref[...] = jnp.cos(ang)
    sin_ref[...] = jnp.where((lane & (HEAD_DIM - 1)) < HEAD_DIM // 2, -s, s)


def _rope_tables(pos_col, invf_row, tm):
    t = pos_col.shape[0]
    return pl.pallas_call(
        _rope_kernel,
        out_shape=(jax.ShapeDtypeStruct((t, LANES), F32), jax.ShapeDtypeStruct((t, LANES), F32)),
        grid=(t // tm,),
        in_specs=[pl.BlockSpec((tm, 1), lambda i: (i, 0)),
                  pl.BlockSpec((1, LANES), lambda i: (0, 0))],
        out_specs=(pl.BlockSpec((tm, LANES), lambda i: (i, 0)),
                   pl.BlockSpec((tm, LANES), lambda i: (i, 0))),
        compiler_params=_cparams("arbitrary"),
        name="rope",
    )(pos_col, invf_row)


def _rms_rows(x, g):
    return x * lax.rsqrt(jnp.mean(x * x, axis=-1, keepdims=True) + EPS) * g


def _head_norm(z, bd, g):
    ss = jnp.dot((z * z).astype(BF16), bd, preferred_element_type=F32)
    return z * lax.rsqrt(ss * (1.0 / HEAD_DIM) + EPS) * g


def _rotate_half(z):
    n = z.shape[-1]
    lane = lax.broadcasted_iota(I32, z.shape, 1)
    fwd = pltpu.roll(z, n - HEAD_DIM // 2, 1)
    bwd = pltpu.roll(z, HEAD_DIM // 2, 1)
    return jnp.where((lane & (HEAD_DIM - 1)) < HEAD_DIM // 2, fwd, bwd)


def _inproj_kernel(x_ref, mod_ref, g1_ref, wqkv_ref, wff_ref, wg_ref, bd_ref, gv_ref, bf_ref,
                   cos_ref, sin_ref,
                   fq_ref, fk_ref, fv_ref, dq_ref, dk_ref, dv_ref, g0_ref, g1o_ref, lf_ref):
    x = x_ref[...]
    sh1 = mod_ref[0, 0:1, :]
    sc1 = mod_ref[0, 1:2, :]
    h = (_rms_rows(x, g1_ref[...]) * (1.0 + sc1) + sh1).astype(BF16)
    bd = bd_ref[...]
    w = FOX_WIDTH

    def proj(j):
        return jnp.dot(h, wqkv_ref[:, j * w:(j + 1) * w], preferred_element_type=F32)

    fq_ref[...] = (_head_norm(proj(0), bd, gv_ref[0:1, :]) * Q_SCALE).astype(BF16)
    fk_ref[...] = _head_norm(proj(1), bd, gv_ref[1:2, :]).astype(BF16)
    fv_ref[...] = proj(2).astype(BF16)

    cos = jnp.concatenate([cos_ref[...]] * (w // LANES), axis=1)
    sin = jnp.concatenate([sin_ref[...]] * (w // LANES), axis=1)
    qn = _head_norm(proj(3), bd, gv_ref[2:3, :])
    dq_ref[...] = ((qn * cos + _rotate_half(qn) * sin) * Q_SCALE).astype(BF16)
    kn = _head_norm(proj(4), bd, gv_ref[3:4, :])
    dk_ref[...] = (kn * cos + _rotate_half(kn) * sin).astype(BF16)
    dv_ref[...] = proj(5).astype(BF16)

    for j in range(2):
        zg = jnp.dot(h, wg_ref[:, j * D_MODEL:(j + 1) * D_MODEL], preferred_element_type=F32)
        (g0_ref, g1o_ref)[j][...] = jax.nn.sigmoid(zg).astype(BF16)

    zf = lax.dot_general(wff_ref[...], h, _NT, preferred_element_type=F32) + bf_ref[...]
    lf_ref[...] = jnp.minimum(zf, 0.0) - jnp.log1p(jnp.exp(-jnp.abs(zf)))


def _inproj(x2, mod3, g1, wqkv, wff_t, wgate, bd, gvecs, bf_col, cos_t, sin_t, tm, seq):
    t, d = x2.shape
    tps = seq // tm
    row = lambda i: (i, 0)
    full = lambda i: (0, 0)
    bsd = lambda n: jax.ShapeDtypeStruct((t, n), BF16)
    return pl.pallas_call(
        _inproj_kernel,
        out_shape=(bsd(512), bsd(512), bsd(512), bsd(512), bsd(512), bsd(512), bsd(d), bsd(d),
                   jax.ShapeDtypeStruct((16, t), F32)),
        grid=(t // tm,),
        in_specs=[pl.BlockSpec((tm, d), row),
                  pl.BlockSpec((1, 6, d), lambda i: (i // tps, 0, 0)),
                  pl.BlockSpec((1, d), full),
                  pl.BlockSpec(wqkv.shape, full),
                  pl.BlockSpec(wff_t.shape, full),
                  pl.BlockSpec(wgate.shape, full),
                  pl.BlockSpec(bd.shape, full),
                  pl.BlockSpec(gvecs.shape, full),
                  pl.BlockSpec(bf_col.shape, full),
                  pl.BlockSpec((tm, LANES), row),
                  pl.BlockSpec((tm, LANES), row)],
        out_specs=tuple([pl.BlockSpec((tm, 512), row)] * 6 + [pl.BlockSpec((tm, d), row)] * 2
                        + [pl.BlockSpec((16, tm), lambda i: (0, i))]),
        compiler_params=_cparams("arbitrary"),
        name="inproj",
    )(x2, mod3, g1, wqkv, wff_t, wgate, bd, gvecs, bf_col, cos_t, sin_t)


def _cumsum_kernel(lf_ref, tri_ref, eye_ref, sel_ref, ones_ref, dec_ref, *, seq):
    tri = tri_ref[...]
    eye = eye_ref[...]
    sel = sel_ref[...]
    carry = jnp.zeros((16, 1), F32)
    for blk in range(seq // LANES):
        sl = slice(blk * LANES, (blk + 1) * LANES)
        hi, mid, lo = _split3(lf_ref[:, sl])
        c = (jnp.dot(hi, tri, preferred_element_type=F32)
             + jnp.dot(mid, tri, preferred_element_type=F32)
             + jnp.dot(lo, tri, preferred_element_type=F32)) + carry
        pieces = jnp.concatenate(list(_split3(c * LOG2E)) + [jnp.zeros((16, LANES), BF16)], axis=0)
        cols = lax.dot_general(eye, pieces, _NT, preferred_element_type=F32).astype(BF16)
        dec_ref[0, sl, :] = (jnp.dot(cols, sel, preferred_element_type=F32) + ones_ref[...]).astype(BF16)
        carry = c[:, LANES - 1:LANES]


def _cumsum(lf_t, batch, seq):
    tri = (jnp.arange(LANES)[:, None] <= jnp.arange(LANES)[None, :]).astype(BF16)
    eye = jnp.eye(LANES, dtype=BF16)
    j, h = jnp.meshgrid(jnp.arange(3), jnp.arange(N_FOX_HEADS), indexing="ij")
    rows = (16 * j + h).reshape(-1)
    sel = jnp.zeros((4 * 16, LANES), F32)
    sel = sel.at[rows, (8 * h + j).reshape(-1)].set(1.0)
    sel = sel.at[rows, (HEAD_DIM + 8 * h + 3 + j).reshape(-1)].set(-1.0).astype(BF16)
    ones = jnp.zeros((1, LANES), F32)
    ones = ones.at[0, (8 * h + 3 + j).reshape(-1)].set(1.0).at[0, (HEAD_DIM + 8 * h + j).reshape(-1)].set(1.0)
    return pl.pallas_call(
        functools.partial(_cumsum_kernel, seq=seq),
        out_shape=jax.ShapeDtypeStruct((batch, seq, LANES), BF16),
        grid=(batch,),
        in_specs=[pl.BlockSpec((16, seq), lambda b: (0, b)),
                  pl.BlockSpec((LANES, LANES), lambda b: (0, 0)),
                  pl.BlockSpec((LANES, LANES), lambda b: (0, 0)),
                  pl.BlockSpec(sel.shape, lambda b: (0, 0)),
                  pl.BlockSpec((1, LANES), lambda b: (0, 0))],
        out_specs=pl.BlockSpec((1, seq, LANES), lambda b: (b, 0, 0)),
        compiler_params=_cparams("arbitrary"),
        name="cumsum",
    )(lf_t, tri, eye, sel, ones)


def _attend(n_chains, nq, q_of, k_of, v_of, diag_mask, finalize, s_a, s_b, m_ref, acc_ref):
    n_steps = nq * (nq + 1) // 2

    def produce(qi, ki, s_ref):
        for c in range(n_chains):
            s_ref[c] = lax.dot_general(q_of(c, qi), k_of(c, ki), _NT, preferred_element_type=F32)

    def consume(ki, s_ref, mask):
        for c in range(n_chains):
            s = s_ref[c]
            if mask is not None:
                s = jnp.where(mask, s, NEG_INF)
            m = m_ref[c]
            m_new = jnp.maximum(m, jnp.max(s, axis=-1, keepdims=True))
            p = jnp.exp2(s - m_new).astype(BF16)
            acc_ref[c] = (jnp.exp2(m - m_new) * acc_ref[c]
                          + jnp.dot(p, v_of(c, ki), preferred_element_type=F32))
            m_ref[c] = m_new

    def step(qi, ki, s_cur, s_nxt, has_next):
        last = (ki == qi).astype(I32)
        nqi = qi + last
        nki = (ki + 1) * (1 - last)

        @pl.when(ki == 0)
        def _():
            m_ref[...] = jnp.full(m_ref.shape, NEG_INF, F32)
            acc_ref[...] = jnp.zeros(acc_ref.shape, F32)

        def block(mask):
            if has_next:
                produce(nqi, nki, s_nxt)
            consume(ki, s_cur, mask)

        @pl.when(ki == qi)
        def _():
            block(diag_mask)
            finalize(qi)

        @pl.when(ki != qi)
        def _():
            block(None)

        return nqi, nki

    produce(0, 0, s_a)
    n_pairs = (n_steps - 1) // 2

    def pair(_, carry):
        qi, ki = step(*carry, s_a, s_b, True)
        return step(qi, ki, s_b, s_a, True)

    qi, ki = lax.fori_loop(0, n_pairs, pair, (jnp.int32(0), jnp.int32(0)))
    if n_steps - 2 * n_pairs == 2:
        qi, ki = step(qi, ki, s_a, s_b, True)
        step(qi, ki, s_b, s_a, False)
    else:
        step(qi, ki, s_a, s_b, False)


def _fox_kernel(q_ref, k_ref, v_ref, dec_ref, o_ref, qaug_ref, kaug_ref, vaug_ref, s_a, s_b,
                m_ref, acc_ref, *, tq):
    hp = pl.program_id(1)
    seq = v_ref.shape[1]
    ones = jnp.ones((seq, HEAD_DIM), BF16)
    qdec = dec_ref[0, :, :HEAD_DIM]
    kdec = dec_ref[0, :, HEAD_DIM:].astype(F32)
    lane_head = lax.shift_right_logical(lax.broadcasted_iota(I32, kdec.shape, 1), 3)
    for hh in range(2):
        hs = slice(hh * HEAD_DIM, (hh + 1) * HEAD_DIM)
        vaug_ref[hh] = jnp.concatenate([v_ref[0, :, hs], ones], axis=-1)
        kd = jnp.where(lane_head == 2 * hp + hh, kdec, 0.0).astype(BF16)
        kaug_ref[hh] = jnp.concatenate([k_ref[0, :, hs], kd], axis=-1)
        qaug_ref[hh] = jnp.concatenate([q_ref[0, :, hs], qdec], axis=-1)

    row = lax.broadcasted_iota(I32, (tq, tq), 0)
    col = lax.broadcasted_iota(I32, (tq, tq), 1)

    def tile(ref, c, i):
        return ref[c, pl.ds(pl.multiple_of(i * tq, tq), tq), :]

    def finalize(qi):
        outs = []
        for c in range(2):
            acc = acc_ref[c]
            outs.append((acc / pltpu.roll(acc, HEAD_DIM, 1))[:, :HEAD_DIM])
        o_ref[0, pl.ds(pl.multiple_of(qi * tq, tq), tq), :] = jnp.concatenate(outs, axis=-1).astype(BF16)

    _attend(2, seq // tq, functools.partial(tile, qaug_ref), functools.partial(tile, kaug_ref),
            functools.partial(tile, vaug_ref), col <= row, finalize, s_a, s_b, m_ref, acc_ref)


def _attn_scratch(tq, width):
    return [pltpu.VMEM((2, tq, tq), F32), pltpu.VMEM((2, tq, tq), F32),
            pltpu.VMEM((2, tq, 1), F32), pltpu.VMEM((2, tq, width), F32)]


def _fox_attention(fq, fk, fv, dec, tq):
    b, s, _ = fq.shape
    head_pair = pl.BlockSpec((1, s, LANES), lambda bi, hp: (bi, 0, hp))
    return pl.pallas_call(
        functools.partial(_fox_kernel, tq=tq),
        out_shape=jax.ShapeDtypeStruct((b, s, FOX_WIDTH), BF16),
        grid=(b, N_FOX_HEADS // 2),
        in_specs=[head_pair, head_pair, head_pair,
                  pl.BlockSpec((1, s, LANES), lambda bi, hp: (bi, 0, 0))],
        out_specs=head_pair,
        scratch_shapes=[pltpu.VMEM((2, s, 2 * HEAD_DIM), BF16)] * 3 + _attn_scratch(tq, 2 * HEAD_DIM),
        compiler_params=_cparams("arbitrary", "arbitrary"),
        name="fox",
    )(fq, fk, fv, dec)


def _diff_kernel(q_ref, k_ref, v_ref, lam_ref, gs_ref, o_ref, vaug_ref, s_a, s_b, m_ref, acc_ref, *, tq):
    dv = 2 * HEAD_DIM
    seq = v_ref.shape[1]
    vaug_ref[...] = jnp.concatenate([v_ref[0], jnp.ones((seq, dv), BF16)], axis=-1)

    row = lax.broadcasted_iota(I32, (tq, tq), 0)
    col = lax.broadcasted_iota(I32, (tq, tq), 1)
    chunk_causal = lax.shift_right_logical(col, CHUNK_SHIFT) <= lax.shift_right_logical(row, CHUNK_SHIFT)
    lv = lam_ref[...]
    lam = (jnp.exp(jnp.sum(lv[0:1] * lv[1:2], axis=-1, keepdims=True))
           - jnp.exp(jnp.sum(lv[2:3] * lv[3:4], axis=-1, keepdims=True)) + LAM0)

    def rows(i):
        return pl.ds(pl.multiple_of(i * tq, tq), tq)

    def q_of(c, qi):
        return q_ref[0, rows(qi), c * HEAD_DIM:(c + 1) * HEAD_DIM]

    def k_of(c, ki):
        return k_ref[0, rows(ki), c * HEAD_DIM:(c + 1) * HEAD_DIM]

    def v_of(c, ki):
        return vaug_ref[rows(ki), :]

    def finalize(qi):
        outs = [acc_ref[c][:, :dv] / acc_ref[c][:, dv:] for c in range(2)]
        o = outs[0] - lam * outs[1]
        o_ref[0, rows(qi), :] = (_rms_rows(o, gs_ref[...]) * (1.0 - LAM0)).astype(BF16)

    _attend(2, seq // tq, q_of, k_of, v_of, chunk_causal, finalize, s_a, s_b, m_ref, acc_ref)


def _diff_attention(dq, dk, dv, lam_vecs, g_subln, tq):
    b, s, _ = dq.shape
    head = pl.BlockSpec((1, s, LANES), lambda bi, h: (bi, 0, h))
    return pl.pallas_call(
        functools.partial(_diff_kernel, tq=tq),
        out_shape=jax.ShapeDtypeStruct((b, s, DIFF_WIDTH), BF16),
        grid=(b, N_DIFF_HEADS),
        in_specs=[head, head, head,
                  pl.BlockSpec((4, HEAD_DIM), lambda bi, h: (0, 0)),
                  pl.BlockSpec((1, LANES), lambda bi, h: (0, 0))],
        out_specs=head,
        scratch_shapes=[pltpu.VMEM((s, 4 * HEAD_DIM), BF16)] + _attn_scratch(tq, 4 * HEAD_DIM),
        compiler_params=_cparams("arbitrary", "arbitrary"),
        name="diff",
    )(dq, dk, dv, lam_vecs, g_subln)


def _merge_kernel(of_ref, od_ref, g0_ref, g1_ref, x_ref, mod_ref, g2_ref, wpf_ref, wpd_ref, wout_ref,
                  wrh_ref, wrl_ref, br_ref,
                  x1_ref, hp_ref, lg_ref):
    gt1 = mod_ref[0, 2:3, :]
    sh2 = mod_ref[0, 3:4, :]
    sc2 = mod_ref[0, 4:5, :]
    sub = x_ref.shape[0] // ROW_CHAINS
    for r in range(ROW_CHAINS):
        rows = pl.ds(r * sub, sub)
        a = jnp.dot(of_ref[rows, :], wpf_ref[...], preferred_element_type=F32)
        b = jnp.dot(od_ref[rows, :], wpd_ref[...], preferred_element_type=F32)
        merged = g0_ref[rows, :].astype(F32) * a + g1_ref[rows, :].astype(F32) * b
        x1 = x_ref[rows, :] + gt1 * jnp.dot(merged.astype(BF16), wout_ref[...], preferred_element_type=F32)
        x1_ref[rows, :] = x1
        h2 = _rms_rows(x1, g2_ref[...]) * (1.0 + sc2) + sh2

        hi = h2.astype(BF16)
        hf = hi.astype(F32)
        half = D_MODEL // 2
        lo_bits = pltpu.bitcast(hf[:, :half], U32) >> 16
        hi_bits = pltpu.bitcast(hf[:, half:], U32) & jnp.uint32(0xFFFF0000)
        hp_ref[rows, :] = lo_bits | hi_bits

        lo = (h2 - hf).astype(BF16)
        wrh = wrh_ref[...]
        lg_ref[rows, :] = (jnp.dot(hi, wrh, preferred_element_type=F32)
                           + jnp.dot(lo, wrh, preferred_element_type=F32)
                           + jnp.dot(hi, wrl_ref[...], preferred_element_type=F32)) + br_ref[...]


def _route_kernel(lg_ref, ids_ref, wts_ref, cnt_ref):
    i = pl.program_id(0)
    lg = lg_ref[:, :LANES]
    le = lg_ref[:, LANES:]
    lane = lax.broadcasted_iota(I32, lg.shape, 1)
    big = jnp.int32(1 << 20)

    def softmax_masked(z, mask):
        zm = jnp.where(mask, z, -jnp.inf)
        e = jnp.exp(zm - jnp.max(zm, axis=-1, keepdims=True))
        return e / jnp.sum(e, axis=-1, keepdims=True)

    def top1(p, mask):
        pm = jnp.where(mask, p, -1.0)
        best = jnp.max(pm, axis=-1, keepdims=True)
        idx = jnp.min(jnp.where(pm == best, lane, big), axis=-1, keepdims=True)
        return best, idx

    gmask = lane < N_GROUPS
    g_w, g_idx = top1(softmax_masked(lg, gmask), gmask)
    emask = (lane >> 3) == g_idx
    p_exp = softmax_masked(le, emask)
    p1, i1 = top1(p_exp, emask)
    p2, i2 = top1(p_exp, emask & (lane != i1))
    denom = p1 + p2
    w1 = g_w * (p1 / denom)
    w2 = g_w * (p2 / denom)
    ids_ref[...] = jnp.where(lane == 0, i1, i2)[:, :TOP_K]
    wts_ref[...] = jnp.where(lane == 0, w1, w2)[:, :TOP_K]

    @pl.when(i == 0)
    def _():
        cnt_ref[...] = jnp.zeros_like(cnt_ref)

    cnt_ref[...] += jnp.sum(((lane == i1) | (lane == i2)).astype(F32), axis=0, keepdims=True)


def _route(logits, tr):
    t = logits.shape[0]
    return pl.pallas_call(
        _route_kernel,
        out_shape=(jax.ShapeDtypeStruct((t, TOP_K), I32),
                   jax.ShapeDtypeStruct((t, TOP_K), F32),
                   jax.ShapeDtypeStruct((1, LANES), F32)),
        grid=(t // tr,),
        in_specs=[pl.BlockSpec((tr, 2 * LANES), lambda i: (i, 0))],
        out_specs=(pl.BlockSpec((tr, TOP_K), lambda i: (i, 0)),
                   pl.BlockSpec((tr, TOP_K), lambda i: (i, 0)),
                   pl.BlockSpec((1, LANES), lambda i: (0, 0))),
        compiler_params=_cparams("arbitrary"),
        name="route",
    )(logits)


def _merge(o_f, o_d, g0, g1, x2, mod3, g2, wpf, wpd, wout, wr_hi, wr_lo, br, tm, seq):
    t, d = x2.shape
    tps = seq // tm
    row = lambda i: (i, 0)
    full = lambda i: (0, 0)
    return pl.pallas_call(
        _merge_kernel,
        out_shape=(jax.ShapeDtypeStruct((t, d), F32),
                   jax.ShapeDtypeStruct((t, d // 2), U32),
                   jax.ShapeDtypeStruct((t, 2 * LANES), F32)),
        grid=(t // tm,),
        in_specs=[pl.BlockSpec((tm, 512), row), pl.BlockSpec((tm, 512), row),
                  pl.BlockSpec((tm, d), row), pl.BlockSpec((tm, d), row),
                  pl.BlockSpec((tm, d), row),
                  pl.BlockSpec((1, 6, d), lambda i: (i // tps, 0, 0)),
                  pl.BlockSpec((1, d), full),
                  pl.BlockSpec(wpf.shape, full), pl.BlockSpec(wpd.shape, full),
                  pl.BlockSpec(wout.shape, full),
                  pl.BlockSpec(wr_hi.shape, full), pl.BlockSpec(wr_lo.shape, full),
                  pl.BlockSpec(br.shape, full)],
        out_specs=(pl.BlockSpec((tm, d), row), pl.BlockSpec((tm, d // 2), row),
                   pl.BlockSpec((tm, 2 * LANES), row)),
        compiler_params=_cparams("arbitrary"),
        name="merge",
    )(o_f, o_d, g0, g1, x2, mod3, g2, wpf, wpd, wout, wr_hi, wr_lo, br)


def _rank_kernel(ids_ref, pstart_ref, ltri_ref, dest_ref, carry_ref):
    i = pl.program_id(0)

    @pl.when(i == 0)
    def _():
        carry_ref[...] = jnp.zeros_like(carry_ref)

    ids = ids_ref[...]
    lane = lax.broadcasted_iota(I32, (ids.shape[0], LANES), 1)
    oh0 = lane == ids[:, 0:1]
    oh1 = lane == ids[:, 1:2]
    both = (oh0 | oh1).astype(BF16)
    before = jnp.dot(ltri_ref[...], both, preferred_element_type=F32)
    base = before + carry_ref[...] + pstart_ref[...]
    d0 = jnp.sum(jnp.where(oh0, base, 0.0), axis=-1, keepdims=True)
    d1 = jnp.sum(jnp.where(oh1, base, 0.0), axis=-1, keepdims=True)
    dest_ref[...] = jnp.where(lane == 0, d0, d1)[:, :TOP_K].astype(I32)
    carry_ref[...] += jnp.sum(both.astype(F32), axis=0, keepdims=True)


def _rank(ids, pad_start_row, tt):
    t = ids.shape[0]
    ltri = (jnp.arange(tt)[:, None] > jnp.arange(tt)[None, :]).astype(BF16)
    return pl.pallas_call(
        _rank_kernel,
        out_shape=jax.ShapeDtypeStruct((t, TOP_K), I32),
        grid=(t // tt,),
        in_specs=[pl.BlockSpec((tt, TOP_K), lambda i: (i, 0)),
                  pl.BlockSpec((1, LANES), lambda i: (0, 0)),
                  pl.BlockSpec((tt, tt), lambda i: (0, 0))],
        out_specs=pl.BlockSpec((tt, TOP_K), lambda i: (i, 0)),
        scratch_shapes=[pltpu.VMEM((1, LANES), F32)],
        compiler_params=_cparams("arbitrary"),
        name="rank",
    )(ids, pad_start_row, ltri)


def _row_copy(src_ref, s, dst_ref, d, sem):
    return pltpu.make_async_copy(src_ref.at[pl.ds(s, 1)], dst_ref.at[pl.ds(d, 1)], sem)


def _dispatch_kernel(pend_ref, padded_ref, nused_ref, dest_ref, h_ref, xs_ref, zero_ref, sem, zsem,
                     *, td, n_blocks):
    i = pl.program_id(0)

    @pl.when(i == 0)
    def _():
        zero_ref[...] = jnp.zeros_like(zero_ref)

        def zero_block(start):
            cp = pltpu.make_async_copy(
                zero_ref, xs_ref.at[pl.ds(pl.multiple_of(start, MOE_BLOCK), MOE_BLOCK)], zsem)
            cp.start()
            cp.wait()

        for e in range(N_EXPERTS):
            @pl.when(padded_ref[e] > 0)
            def _(e=e):
                zero_block(pend_ref[e] - MOE_BLOCK)

            @pl.when(nused_ref[0] + e < n_blocks)
            def _(e=e):
                zero_block((nused_ref[0] + e) * MOE_BLOCK)

    def issue(t, _):
        _row_copy(h_ref, t, xs_ref, dest_ref[2 * t], sem).start(priority=0)
        _row_copy(h_ref, t, xs_ref, dest_ref[2 * t + 1], sem).start(priority=1)
        return 0

    lax.fori_loop(0, td, issue, 0, unroll=DMA_UNROLL)

    def drain(t, _):
        _row_copy(h_ref, 0, xs_ref, 0, sem).wait()
        _row_copy(h_ref, 0, xs_ref, 0, sem).wait()
        return 0

    lax.fori_loop(0, td, drain, 0, unroll=DMA_UNROLL)


def _dispatch(pad_end, padded, n_used, dest_flat, h_packed, p_rows, td):
    t, w = h_packed.shape
    grid_spec = pltpu.PrefetchScalarGridSpec(
        num_scalar_prefetch=3,
        grid=(t // td,),
        in_specs=[pl.BlockSpec((TOP_K * td,), lambda i, *_: (i,), memory_space=pltpu.SMEM),
                  pl.BlockSpec((td, w), lambda i, *_: (i, 0))],
        out_specs=pl.BlockSpec(memory_space=pl.ANY),
        scratch_shapes=[pltpu.VMEM((MOE_BLOCK, w), U32),
                        pltpu.SemaphoreType.DMA(()), pltpu.SemaphoreType.DMA(())],
    )
    return pl.pallas_call(
        functools.partial(_dispatch_kernel, td=td, n_blocks=p_rows // MOE_BLOCK),
        out_shape=jax.ShapeDtypeStruct((p_rows, w), U32),
        grid_spec=grid_spec,
        compiler_params=_cparams("arbitrary"),
        name="dispatch",
    )(pad_end, padded, n_used, dest_flat, h_packed)


def _expert_kernel(be_ref, nused_ref, xs_ref, w1_ref, w3_ref, w2_ref, y_ref):
    p = pl.program_id(0)

    @pl.when(p < nused_ref[0])
    def _():
        words = xs_ref[...]
        half = D_MODEL // 2
        xa = pltpu.bitcast(words << 16, F32).astype(BF16)
        xb = pltpu.bitcast(words & jnp.uint32(0xFFFF0000), F32).astype(BF16)

        def up(w_ref):
            return (jnp.dot(xa, w_ref[0, :half, :], preferred_element_type=F32)
                    + jnp.dot(xb, w_ref[0, half:, :], preferred_element_type=F32))

        g = up(w1_ref)
        u = up(w3_ref)
        act = (g * jax.nn.sigmoid(g) * u).astype(BF16)
        y_ref[...] = jnp.dot(act, w2_ref[0], preferred_element_type=F32)

    @pl.when(p >= nused_ref[0])
    def _():
        y_ref[...] = jnp.zeros_like(y_ref)


def _experts(block_expert, n_used, xs, w1, w3, w2):
    p_rows, w = xs.shape
    nb = p_rows // MOE_BLOCK
    d = w1.shape[1]
    blk = lambda p, be, nu: (p, 0)
    wsel = lambda p, be, nu: (be[jnp.minimum(p, nu[0] - 1)], 0, 0)
    grid_spec = pltpu.PrefetchScalarGridSpec(
        num_scalar_prefetch=2,
        grid=(nb,),
        in_specs=[pl.BlockSpec((MOE_BLOCK, w), blk),
                  pl.BlockSpec((1, d, D_EXPERT), wsel),
                  pl.BlockSpec((1, d, D_EXPERT), wsel),
                  pl.BlockSpec((1, D_EXPERT, d), wsel)],
        out_specs=pl.BlockSpec((MOE_BLOCK, d), blk),
    )
    return pl.pallas_call(
        _expert_kernel,
        out_shape=jax.ShapeDtypeStruct((p_rows, d), F32),
        grid_spec=grid_spec,
        compiler_params=_cparams("arbitrary"),
        name="experts",
    )(block_expert, n_used, xs, w1, w3, w2)


def _combine_kernel(dest_ref, wts_ref, x1_ref, mod_ref, yb_ref, o_ref, ybuf_ref, sem, *, tc):
    def issue(t, _):
        _row_copy(yb_ref, dest_ref[2 * t], ybuf_ref.at[0], t, sem).start(priority=0)
        _row_copy(yb_ref, dest_ref[2 * t + 1], ybuf_ref.at[1], t, sem).start(priority=1)
        return 0

    lax.fori_loop(0, tc, issue, 0, unroll=DMA_UNROLL)

    def drain(t, _):
        _row_copy(yb_ref, 0, ybuf_ref.at[0], 0, sem).wait()
        _row_copy(yb_ref, 0, ybuf_ref.at[1], 0, sem).wait()
        return 0

    lax.fori_loop(0, tc, drain, 0, unroll=DMA_UNROLL)
    gt2 = mod_ref[0, 5:6, :]
    wts = wts_ref[...]
    y = ybuf_ref[0] * wts[:, 0:1] + ybuf_ref[1] * wts[:, 1:2]
    o_ref[...] = x1_ref[...] + gt2 * y


def _combine(dest_flat, wts, x1, mod3, yb, tc, seq):
    t, d = x1.shape
    tps = seq // tc
    return pl.pallas_call(
        functools.partial(_combine_kernel, tc=tc),
        out_shape=jax.ShapeDtypeStruct((t, d), F32),
        grid=(t // tc,),
        in_specs=[pl.BlockSpec((TOP_K * tc,), lambda i: (i,), memory_space=pltpu.SMEM),
                  pl.BlockSpec((tc, TOP_K), lambda i: (i, 0)),
                  pl.BlockSpec((tc, d), lambda i: (i, 0)),
                  pl.BlockSpec((1, 6, d), lambda i: (i // tps, 0, 0)),
                  pl.BlockSpec(memory_space=pl.ANY)],
        out_specs=pl.BlockSpec((tc, d), lambda i: (i, 0)),
        scratch_shapes=[pltpu.VMEM((2, tc, d), F32), pltpu.SemaphoreType.DMA(())],
        compiler_params=_cparams("arbitrary"),
        name="combine",
    )(dest_flat, wts, x1, mod3, yb)


def _tile(n, pref):
    t = pref
    while n % t:
        t //= 2
    return t


def kernel(x, c, positions, w_ada, b_ada, g_norm1, w_in, b_f, g_q_fox, g_k_fox, g_q_diff, g_k_diff,
           lam_q1, lam_k1, lam_q2, lam_k2, g_subln, w_proj_fox, w_proj_diff, w_out, g_norm2,
           w_router_group, b_router_group, w_router_expert, b_router_expert, w1, w3, w2):
    b, s, d = x.shape
    t = b * s
    assert d == D_MODEL and w_ada.shape[0] == 1 and s % LANES == 0
    tm = _tile(s, 512)
    tq = _tile(s, 512)

    wi = w_in[0]
    c_fk, c_fv, c_ff = FOX_WIDTH, 2 * FOX_WIDTH, 3 * FOX_WIDTH
    c_dq = c_ff + N_FOX_HEADS
    c_gate = c_dq + 3 * 512
    wqkv = jnp.concatenate([wi[:, :c_ff], wi[:, c_dq:c_gate]], axis=1).astype(BF16)
    wff_t = jnp.zeros((16, d), BF16).at[:N_FOX_HEADS].set(wi[:, c_ff:c_dq].T.astype(BF16))
    wgate = wi[:, c_gate:].astype(BF16)
    bf_col = jnp.zeros((16, 1), F32).at[:N_FOX_HEADS, 0].set(b_f[0])
    head_id = jnp.arange(FOX_WIDTH) // HEAD_DIM
    bd = (head_id[:, None] == head_id[None, :]).astype(BF16)
    gvecs = jnp.stack([jnp.tile(g[0], FOX_WIDTH // HEAD_DIM)
                       for g in (g_q_fox, g_k_fox, g_q_diff, g_k_diff)])
    lam_vecs = jnp.stack([lam_q1[0], lam_k1[0], lam_q2[0], lam_k2[0]])
    inv_freq = ROPE_THETA ** (-jnp.arange(0, HEAD_DIM, 2, dtype=F32) / HEAD_DIM)
    invf_row = jnp.tile(inv_freq, LANES // (HEAD_DIM // 2)).reshape(1, LANES)
    wr = jnp.zeros((d, 2 * LANES), F32)
    wr = wr.at[:, :N_GROUPS].set(w_router_group[0]).at[:, LANES:LANES + N_EXPERTS].set(w_router_expert[0])
    wr_hi = wr.astype(BF16)
    wr_lo = (wr - wr_hi.astype(F32)).astype(BF16)
    br = jnp.zeros((1, 2 * LANES), F32)
    br = br.at[0, :N_GROUPS].set(b_router_group[0]).at[0, LANES:LANES + N_EXPERTS].set(b_router_expert[0])

    x2 = x.reshape(t, d)
    mod3 = _ada(c, w_ada[0], b_ada[0]).reshape(b, 6, d)
    cos_t, sin_t = _rope_tables(positions.reshape(t, 1), invf_row, tm)

    fq, fk, fv, dq, dk, dv, g0, g1, lf_t = _inproj(
        x2, mod3, g_norm1, wqkv, wff_t, wgate, bd, gvecs, bf_col, cos_t, sin_t, tm, s)
    dec = _cumsum(lf_t, b, s)
    r3 = lambda a: a.reshape(b, s, a.shape[-1])
    o_f = _fox_attention(r3(fq), r3(fk), r3(fv), dec, tq)
    o_d = _diff_attention(r3(dq), r3(dk), r3(dv), lam_vecs, g_subln, tq)

    x1, h_packed, logits = _merge(
        o_f.reshape(t, FOX_WIDTH), o_d.reshape(t, DIFF_WIDTH), g0, g1, x2, mod3, g_norm2,
        w_proj_fox[0].astype(BF16), w_proj_diff[0].astype(BF16), w_out[0].astype(BF16),
        wr_hi, wr_lo, br, tm, s)
    ids, wts, counts = _route(logits, _tile(t, 2048))

    a = t * TOP_K
    n_blocks = -(-a // MOE_BLOCK) + N_EXPERTS
    p_rows = n_blocks * MOE_BLOCK
    cnt = counts[0, :N_EXPERTS].astype(I32)
    padded = ((cnt + MOE_BLOCK - 1) // MOE_BLOCK) * MOE_BLOCK
    pad_end = jnp.cumsum(padded).astype(I32)
    pad_start = pad_end - padded
    block_start = jnp.arange(n_blocks, dtype=I32) * MOE_BLOCK
    block_expert = jnp.minimum(
        jnp.sum((pad_end[None, :] <= block_start[:, None]).astype(I32), axis=1), N_EXPERTS - 1)
    n_used = (pad_end[-1:] // MOE_BLOCK).astype(I32)
    pstart_row = jnp.zeros((1, LANES), F32).at[0, :N_EXPERTS].set(pad_start.astype(F32))

    dest = _rank(ids, pstart_row, _tile(t, 256)).reshape(a)
    xs = _dispatch(pad_end, padded, n_used, dest, h_packed, p_rows, _tile(t, 512))
    yb = _experts(block_expert, n_used, xs, w1[0].astype(BF16), w3[0].astype(BF16), w2[0].astype(BF16))
    out = _combine(dest, wts, x1, mod3, yb, _tile(s, 256), s)
    return out.reshape(b, s, d)
```

```python
import functools
import math

import jax
import jax.numpy as jnp
from jax import lax
from jax.experimental import pallas as pl
from jax.experimental.pallas import tpu as pltpu

F32 = jnp.float32
BF16 = jnp.bfloat16
I32 = jnp.int32
U32 = jnp.uint32

D_MODEL = 1024
HEAD_DIM = 64
N_FOX_HEADS = 8
N_DIFF_HEADS = 4
FOX_WIDTH = 512
DIFF_WIDTH = 512
CHUNK = 64
CHUNK_SHIFT = CHUNK.bit_length() - 1
ROPE_THETA = 10000.0
N_GROUPS = 4
EXPERTS_PER_GROUP = 8
N_EXPERTS = 32
TOP_K = 2
D_EXPERT = 512
MOE_BLOCK = 256
EPS = 1e-6
NEG_INF = -1e30
LAM0 = 0.8 - 0.6 * math.exp(-0.3 * 0)
LOG2E = math.log2(math.e)
Q_SCALE = HEAD_DIM ** -0.5 * LOG2E

LANES = 128
VMEM_LIMIT = 56 * 1024 * 1024

DMA_UNROLL = 8
ONES_ROWS = 16
ROW_CHAINS = 2

_NT = (((1,), (1,)), ((), ()))


def _cparams(*sem):
    return pltpu.CompilerParams(dimension_semantics=sem, vmem_limit_bytes=VMEM_LIMIT)


def _split3(x):
    hi = x.astype(BF16)
    r1 = x - hi.astype(F32)
    mid = r1.astype(BF16)
    lo = (r1 - mid.astype(F32)).astype(BF16)
    return hi, mid, lo


def _ada_kernel(c_ref, w_ref, b_ref, o_ref):
    c = c_ref[...]
    ca = (c * jax.nn.sigmoid(c)).astype(BF16)
    o_ref[...] = jnp.dot(ca, w_ref[...].astype(BF16), preferred_element_type=F32) + b_ref[...]


def _ada(c, w_ada, b_ada):
    b, d = c.shape
    n = w_ada.shape[1]
    tn = 1024
    return pl.pallas_call(
        _ada_kernel,
        out_shape=jax.ShapeDtypeStruct((b, n), F32),
        grid=(n // tn,),
        in_specs=[pl.BlockSpec((b, d), lambda j: (0, 0)),
                  pl.BlockSpec((d, tn), lambda j: (0, j)),
                  pl.BlockSpec((1, tn), lambda j: (0, j))],
        out_specs=pl.BlockSpec((b, tn), lambda j: (0, j)),
        compiler_params=_cparams("arbitrary"),
        name="ada",
    )(c, w_ada, b_ada.reshape(1, n))


def _rope_kernel(pos_ref, invf_ref, cos_ref, sin_ref):
    ang = pos_ref[...].astype(F32) * invf_ref[...]
    lane = lax.broadcasted_iota(I32, ang.shape, 1)
    s = jnp.sin(ang)
    cos_ref[...] = jnp.cos(ang)
    sin_ref[...] = jnp.where((lane & (HEAD_DIM - 1)) < HEAD_DIM // 2, -s, s)


def _rope_tables(pos_col, invf_row, tm):
    t = pos_col.shape[0]
    return pl.pallas_call(
        _rope_kernel,
        out_shape=(jax.ShapeDtypeStruct((t, LANES), F32), jax.ShapeDtypeStruct((t, LANES), F32)),
        grid=(t // tm,),
        in_specs=[pl.BlockSpec((tm, 1), lambda i: (i, 0)),
                  pl.BlockSpec((1, LANES), lambda i: (0, 0))],
        out_specs=(pl.BlockSpec((tm, LANES), lambda i: (i, 0)),
                   pl.BlockSpec((tm, LANES), lambda i: (i, 0))),
        compiler_params=_cparams("arbitrary"),
        name="rope",
    )(pos_col, invf_row)


def _rms_rows(x, g):
    return x * lax.rsqrt(jnp.mean(x * x, axis=-1, keepdims=True) + EPS) * g


def _head_norm(z, bd, g):
    ss = jnp.dot((z * z).astype(BF16), bd, preferred_element_type=F32)
    return z * lax.rsqrt(ss * (1.0 / HEAD_DIM) + EPS) * g


def _rotate_half(z):
    n = z.shape[-1]
    lane = lax.broadcasted_iota(I32, z.shape, 1)
    fwd = pltpu.roll(z, n - HEAD_DIM // 2, 1)
    bwd = pltpu.roll(z, HEAD_DIM // 2, 1)
    return jnp.where((lane & (HEAD_DIM - 1)) < HEAD_DIM // 2, fwd, bwd)


def _inproj_kernel(x_ref, mod_ref, g1_ref, wqk_ref, wvt_ref, wff_ref, wg_ref, bd_ref, gv_ref, bf_ref,
                   cos_ref, sin_ref,
                   fq_ref, fk_ref, fv_ref, dq_ref, dk_ref, dv_ref, g0_ref, g1o_ref, lf_ref):
    x = x_ref[...]
    sh1 = mod_ref[0, 0:1, :]
    sc1 = mod_ref[0, 1:2, :]
    h = (_rms_rows(x, g1_ref[...]) * (1.0 + sc1) + sh1).astype(BF16)
    bd = bd_ref[...]
    w = FOX_WIDTH

    def proj(j):
        return jnp.dot(h, wqk_ref[:, j * w:(j + 1) * w], preferred_element_type=F32)

    def proj_t(j):
        return lax.dot_general(wvt_ref[j * w:(j + 1) * w, :], h, _NT, preferred_element_type=F32)

    fq_ref[...] = (_head_norm(proj(0), bd, gv_ref[0:1, :]) * Q_SCALE).astype(BF16)
    fk_ref[...] = _head_norm(proj(1), bd, gv_ref[1:2, :]).astype(BF16)
    fv_ref[...] = proj_t(0).astype(BF16)

    cos = jnp.concatenate([cos_ref[...]] * (w // LANES), axis=1)
    sin = jnp.concatenate([sin_ref[...]] * (w // LANES), axis=1)
    qn = _head_norm(proj(2), bd, gv_ref[2:3, :])
    dq_ref[...] = ((qn * cos + _rotate_half(qn) * sin) * Q_SCALE).astype(BF16)
    kn = _head_norm(proj(3), bd, gv_ref[3:4, :])
    dk_ref[...] = (kn * cos + _rotate_half(kn) * sin).astype(BF16)
    dv_ref[...] = proj_t(1).astype(BF16)

    for j in range(2):
        zg = jnp.dot(h, wg_ref[:, j * D_MODEL:(j + 1) * D_MODEL], preferred_element_type=F32)
        (g0_ref, g1o_ref)[j][...] = jax.nn.sigmoid(zg).astype(BF16)

    zf = lax.dot_general(wff_ref[...], h, _NT, preferred_element_type=F32) + bf_ref[...]
    lf_ref[...] = jnp.minimum(zf, 0.0) - jnp.log1p(jnp.exp(-jnp.abs(zf)))


def _inproj(x2, mod3, g1, wqk, wvt, wff_t, wgate, bd, gvecs, bf_col, cos_t, sin_t, tm, seq):
    t, d = x2.shape
    tps = seq // tm
    row = lambda i: (i, 0)
    full = lambda i: (0, 0)
    bsd = lambda n: jax.ShapeDtypeStruct((t, n), BF16)
    bsd_t = jax.ShapeDtypeStruct((512, t), BF16)
    tok = pl.BlockSpec((tm, 512), row)
    tok_t = pl.BlockSpec((512, tm), lambda i: (0, i))
    return pl.pallas_call(
        _inproj_kernel,
        out_shape=(bsd(512), bsd(512), bsd_t, bsd(512), bsd(512), bsd_t, bsd(d), bsd(d),
                   jax.ShapeDtypeStruct((16, t), F32)),
        grid=(t // tm,),
        in_specs=[pl.BlockSpec((tm, d), row),
                  pl.BlockSpec((1, 6, d), lambda i: (i // tps, 0, 0)),
                  pl.BlockSpec((1, d), full),
                  pl.BlockSpec(wqk.shape, full),
                  pl.BlockSpec(wvt.shape, full),
                  pl.BlockSpec(wff_t.shape, full),
                  pl.BlockSpec(wgate.shape, full),
                  pl.BlockSpec(bd.shape, full),
                  pl.BlockSpec(gvecs.shape, full),
                  pl.BlockSpec(bf_col.shape, full),
                  pl.BlockSpec((tm, LANES), row),
                  pl.BlockSpec((tm, LANES), row)],
        out_specs=(tok, tok, tok_t, tok, tok, tok_t, pl.BlockSpec((tm, d), row), pl.BlockSpec((tm, d), row),
                   pl.BlockSpec((16, tm), lambda i: (0, i))),
        compiler_params=_cparams("arbitrary"),
        name="inproj",
    )(x2, mod3, g1, wqk, wvt, wff_t, wgate, bd, gvecs, bf_col, cos_t, sin_t)


def _cumsum_kernel(lf_ref, tri_ref, eye_ref, sel_ref, ones_ref, dec_ref, *, seq):
    tri = tri_ref[...]
    eye = eye_ref[...]
    sel = sel_ref[...]
    carry = jnp.zeros((16, 1), F32)
    for blk in range(seq // LANES):
        sl = slice(blk * LANES, (blk + 1) * LANES)
        hi, mid, lo = _split3(lf_ref[:, sl])
        c = (jnp.dot(hi, tri, preferred_element_type=F32)
             + jnp.dot(mid, tri, preferred_element_type=F32)
             + jnp.dot(lo, tri, preferred_element_type=F32)) + carry
        pieces = jnp.concatenate(list(_split3(c * LOG2E)) + [jnp.zeros((16, LANES), BF16)], axis=0)
        cols = lax.dot_general(eye, pieces, _NT, preferred_element_type=F32).astype(BF16)
        dec_ref[0, sl, :] = (jnp.dot(cols, sel, preferred_element_type=F32) + ones_ref[...]).astype(BF16)
        carry = c[:, LANES - 1:LANES]


def _cumsum(lf_t, batch, seq):
    tri = (jnp.arange(LANES)[:, None] <= jnp.arange(LANES)[None, :]).astype(BF16)
    eye = jnp.eye(LANES, dtype=BF16)
    j, h = jnp.meshgrid(jnp.arange(3), jnp.arange(N_FOX_HEADS), indexing="ij")
    rows = (16 * j + h).reshape(-1)
    sel = jnp.zeros((4 * 16, LANES), F32)
    sel = sel.at[rows, (8 * h + j).reshape(-1)].set(1.0)
    sel = sel.at[rows, (HEAD_DIM + 8 * h + 3 + j).reshape(-1)].set(-1.0).astype(BF16)
    ones = jnp.zeros((1, LANES), F32)
    ones = ones.at[0, (8 * h + 3 + j).reshape(-1)].set(1.0).at[0, (HEAD_DIM + 8 * h + j).reshape(-1)].set(1.0)
    return pl.pallas_call(
        functools.partial(_cumsum_kernel, seq=seq),
        out_shape=jax.ShapeDtypeStruct((batch, seq, LANES), BF16),
        grid=(batch,),
        in_specs=[pl.BlockSpec((16, seq), lambda b: (0, b)),
                  pl.BlockSpec((LANES, LANES), lambda b: (0, 0)),
                  pl.BlockSpec((LANES, LANES), lambda b: (0, 0)),
                  pl.BlockSpec(sel.shape, lambda b: (0, 0)),
                  pl.BlockSpec((1, LANES), lambda b: (0, 0))],
        out_specs=pl.BlockSpec((1, seq, LANES), lambda b: (b, 0, 0)),
        compiler_params=_cparams("arbitrary"),
        name="cumsum",
    )(lf_t, tri, eye, sel, ones)


def _attend(n_chains, nq, q_of, k_of, vt_of, diag_mask, finalize, s_a, s_b, m_ref, acc_ref):
    n_steps = nq * (nq + 1) // 2

    def produce(qi, ki, s_ref):
        for c in range(n_chains):
            s_ref[c] = lax.dot_general(k_of(c, ki), q_of(c, qi), _NT, preferred_element_type=F32)

    def consume(ki, s_ref, mask):
        for c in range(n_chains):
            s = s_ref[c]
            if mask is not None:
                s = jnp.where(mask, s, NEG_INF)
            m = m_ref[c]
            m_new = jnp.maximum(m, jnp.max(s, axis=0, keepdims=True))
            p = jnp.exp2(s - m_new).astype(BF16)
            acc_ref[c] = (jnp.exp2(m - m_new) * acc_ref[c]
                          + jnp.dot(vt_of(c, ki), p, preferred_element_type=F32))
            m_ref[c] = m_new

    def step(qi, ki, s_cur, s_nxt, has_next):
        last = (ki == qi).astype(I32)
        nqi = qi + last
        nki = (ki + 1) * (1 - last)

        @pl.when(ki == 0)
        def _():
            m_ref[...] = jnp.full(m_ref.shape, NEG_INF, F32)
            acc_ref[...] = jnp.zeros(acc_ref.shape, F32)

        def block(mask):
            if has_next:
                produce(nqi, nki, s_nxt)
            consume(ki, s_cur, mask)

        @pl.when(ki == qi)
        def _():
            block(diag_mask)
            finalize(qi)

        @pl.when(ki != qi)
        def _():
            block(None)

        return nqi, nki

    produce(0, 0, s_a)
    n_pairs = (n_steps - 1) // 2

    def pair(_, carry):
        qi, ki = step(*carry, s_a, s_b, True)
        return step(qi, ki, s_b, s_a, True)

    qi, ki = lax.fori_loop(0, n_pairs, pair, (jnp.int32(0), jnp.int32(0)))
    if n_steps - 2 * n_pairs == 2:
        qi, ki = step(qi, ki, s_a, s_b, True)
        step(qi, ki, s_b, s_a, False)
    else:
        step(qi, ki, s_a, s_b, False)


def _attn_scratch(tq, rows):
    return [pltpu.VMEM((2, tq, tq), F32), pltpu.VMEM((2, tq, tq), F32),
            pltpu.VMEM((2, 1, tq), F32), pltpu.VMEM((2, rows, tq), F32)]


def _fox_kernel(q_ref, k_ref, vt_ref, dec_ref, o_ref, qaug_ref, kaug_ref, vtaug_ref, s_a, s_b,
                m_ref, acc_ref, *, tq):
    hp = pl.program_id(1)
    seq = q_ref.shape[1]
    nq = seq // tq
    qdec = dec_ref[0, :, :HEAD_DIM]
    kdec = dec_ref[0, :, HEAD_DIM:].astype(F32)
    lane_head = lax.shift_right_logical(lax.broadcasted_iota(I32, kdec.shape, 1), 3)
    ones = jnp.ones((HEAD_DIM, tq), BF16)
    for hh in range(2):
        hs = slice(hh * HEAD_DIM, (hh + 1) * HEAD_DIM)
        kd = jnp.where(lane_head == 2 * hp + hh, kdec, 0.0).astype(BF16)
        kaug_ref[hh] = jnp.concatenate([k_ref[0, :, hs], kd], axis=-1)
        qaug_ref[hh] = jnp.concatenate([q_ref[0, :, hs], qdec], axis=-1)
        for j in range(nq):
            vtaug_ref[hh, j] = jnp.concatenate([vt_ref[hs, j * tq:(j + 1) * tq], ones], axis=0)

    key = lax.broadcasted_iota(I32, (tq, tq), 0)
    qry = lax.broadcasted_iota(I32, (tq, tq), 1)

    def tile(ref, c, i):
        return ref[c, pl.ds(pl.multiple_of(i * tq, tq), tq), :]

    def finalize(qi):
        outs = []
        for c in range(2):
            acc = acc_ref[c]
            outs.append(acc[:HEAD_DIM] / acc[HEAD_DIM:HEAD_DIM + 1])
        o_t = jnp.concatenate(outs, axis=0)
        o_ref[0, pl.ds(pl.multiple_of(qi * tq, tq), tq), :] = o_t.T.astype(BF16)

    _attend(2, nq, functools.partial(tile, qaug_ref), functools.partial(tile, kaug_ref),
            lambda c, ki: vtaug_ref[c, ki], key <= qry, finalize, s_a, s_b, m_ref, acc_ref)


def _fox_attention(fq, fk, fv_t, dec, tq):
    b, s, _ = fq.shape
    head_pair = pl.BlockSpec((1, s, LANES), lambda bi, hp: (bi, 0, hp))
    return pl.pallas_call(
        functools.partial(_fox_kernel, tq=tq),
        out_shape=jax.ShapeDtypeStruct((b, s, FOX_WIDTH), BF16),
        grid=(b, N_FOX_HEADS // 2),
        in_specs=[head_pair, head_pair,
                  pl.BlockSpec((LANES, s), lambda bi, hp: (hp, bi)),
                  pl.BlockSpec((1, s, LANES), lambda bi, hp: (bi, 0, 0))],
        out_specs=head_pair,
        scratch_shapes=[pltpu.VMEM((2, s, 2 * HEAD_DIM), BF16), pltpu.VMEM((2, s, 2 * HEAD_DIM), BF16),
                        pltpu.VMEM((2, s // tq, 2 * HEAD_DIM, tq), BF16)] + _attn_scratch(tq, 2 * HEAD_DIM),
        compiler_params=_cparams("arbitrary", "arbitrary"),
        name="fox",
    )(fq, fk, fv_t, dec)


def _diff_kernel(q_ref, k_ref, vt_ref, lam_ref, gs_ref, o_ref, vtaug_ref, s_a, s_b, m_ref, acc_ref, *, tq):
    dv = 2 * HEAD_DIM
    seq = q_ref.shape[1]
    nq = seq // tq
    ones = jnp.ones((ONES_ROWS, tq), BF16)
    for j in range(nq):
        vtaug_ref[j] = jnp.concatenate([vt_ref[:, j * tq:(j + 1) * tq], ones], axis=0)

    key = lax.broadcasted_iota(I32, (tq, tq), 0)
    qry = lax.broadcasted_iota(I32, (tq, tq), 1)
    chunk_causal = lax.shift_right_logical(key, CHUNK_SHIFT) <= lax.shift_right_logical(qry, CHUNK_SHIFT)
    lv = lam_ref[...]
    lam = (jnp.exp(jnp.sum(lv[0:1] * lv[1:2], axis=-1, keepdims=True))
           - jnp.exp(jnp.sum(lv[2:3] * lv[3:4], axis=-1, keepdims=True)) + LAM0)

    def rows(i):
        return pl.ds(pl.multiple_of(i * tq, tq), tq)

    def q_of(c, qi):
        return q_ref[0, rows(qi), c * HEAD_DIM:(c + 1) * HEAD_DIM]

    def k_of(c, ki):
        return k_ref[0, rows(ki), c * HEAD_DIM:(c + 1) * HEAD_DIM]

    def finalize(qi):
        outs = [acc_ref[c][:dv] / acc_ref[c][dv:dv + 1] for c in range(2)]
        o_t = outs[0] - lam * outs[1]
        inv = lax.rsqrt(jnp.mean(o_t * o_t, axis=0, keepdims=True) + EPS)
        o_t = o_t * inv * gs_ref[...] * (1.0 - LAM0)
        o_ref[0, rows(qi), :] = o_t.T.astype(BF16)

    _attend(2, nq, q_of, k_of, lambda c, ki: vtaug_ref[ki], chunk_causal, finalize,
            s_a, s_b, m_ref, acc_ref)


def _diff_attention(dq, dk, dv_t, lam_vecs, g_subln_col, tq):
    b, s, _ = dq.shape
    head = pl.BlockSpec((1, s, LANES), lambda bi, h: (bi, 0, h))
    rows = 2 * HEAD_DIM + ONES_ROWS
    return pl.pallas_call(
        functools.partial(_diff_kernel, tq=tq),
        out_shape=jax.ShapeDtypeStruct((b, s, DIFF_WIDTH), BF16),
        grid=(b, N_DIFF_HEADS),
        in_specs=[head, head,
                  pl.BlockSpec((LANES, s), lambda bi, h: (h, bi)),
                  pl.BlockSpec((4, HEAD_DIM), lambda bi, h: (0, 0)),
                  pl.BlockSpec((2 * HEAD_DIM, 1), lambda bi, h: (0, 0))],
        out_specs=head,
        scratch_shapes=[pltpu.VMEM((s // tq, rows, tq), BF16)] + _attn_scratch(tq, rows),
        compiler_params=_cparams("arbitrary", "arbitrary"),
        name="diff",
    )(dq, dk, dv_t, lam_vecs, g_subln_col)


def _merge_kernel(of_ref, od_ref, g0_ref, g1_ref, x_ref, mod_ref, g2_ref, wpf_ref, wpd_ref, wout_ref,
                  wrh_ref, wrl_ref, br_ref,
                  x1_ref, hp_ref, lg_ref):
    gt1 = mod_ref[0, 2:3, :]
    sh2 = mod_ref[0, 3:4, :]
    sc2 = mod_ref[0, 4:5, :]
    sub = x_ref.shape[0] // ROW_CHAINS
    for r in range(ROW_CHAINS):
        rows = pl.ds(r * sub, sub)
        a = jnp.dot(of_ref[rows, :], wpf_ref[...], preferred_element_type=F32)
        b = jnp.dot(od_ref[rows, :], wpd_ref[...], preferred_element_type=F32)
        merged = g0_ref[rows, :].astype(F32) * a + g1_ref[rows, :].astype(F32) * b
        x1 = x_ref[rows, :] + gt1 * jnp.dot(merged.astype(BF16), wout_ref[...], preferred_element_type=F32)
        x1_ref[rows, :] = x1
        h2 = _rms_rows(x1, g2_ref[...]) * (1.0 + sc2) + sh2

        hi = h2.astype(BF16)
        hf = hi.astype(F32)
        half = D_MODEL // 2
        lo_bits = pltpu.bitcast(hf[:, :half], U32) >> 16
        hi_bits = pltpu.bitcast(hf[:, half:], U32) & jnp.uint32(0xFFFF0000)
        hp_ref[rows, :] = lo_bits | hi_bits

        lo = (h2 - hf).astype(BF16)
        wrh = wrh_ref[...]
        lg_ref[rows, :] = (jnp.dot(hi, wrh, preferred_element_type=F32)
                           + jnp.dot(lo, wrh, preferred_element_type=F32)
                           + jnp.dot(hi, wrl_ref[...], preferred_element_type=F32)) + br_ref[...]


def _route_kernel(lg_ref, ids_ref, wts_ref, cnt_ref):
    i = pl.program_id(0)
    lg = lg_ref[:, :LANES]
    le = lg_ref[:, LANES:]
    lane = lax.broadcasted_iota(I32, lg.shape, 1)
    big = jnp.int32(1 << 20)

    def softmax_masked(z, mask):
        zm = jnp.where(mask, z, -jnp.inf)
        e = jnp.exp(zm - jnp.max(zm, axis=-1, keepdims=True))
        return e / jnp.sum(e, axis=-1, keepdims=True)

    def top1(p, mask):
        pm = jnp.where(mask, p, -1.0)
        best = jnp.max(pm, axis=-1, keepdims=True)
        idx = jnp.min(jnp.where(pm == best, lane, big), axis=-1, keepdims=True)
        return best, idx

    gmask = lane < N_GROUPS
    g_w, g_idx = top1(softmax_masked(lg, gmask), gmask)
    emask = (lane >> 3) == g_idx
    p_exp = softmax_masked(le, emask)
    p1, i1 = top1(p_exp, emask)
    p2, i2 = top1(p_exp, emask & (lane != i1))
    denom = p1 + p2
    w1 = g_w * (p1 / denom)
    w2 = g_w * (p2 / denom)
    ids_ref[...] = jnp.where(lane == 0, i1, i2)[:, :TOP_K]
    wts_ref[...] = jnp.where(lane == 0, w1, w2)[:, :TOP_K]

    @pl.when(i == 0)
    def _():
        cnt_ref[...] = jnp.zeros_like(cnt_ref)

    cnt_ref[...] += jnp.sum(((lane == i1) | (lane == i2)).astype(F32), axis=0, keepdims=True)


def _route(logits, tr):
    t = logits.shape[0]
    return pl.pallas_call(
        _route_kernel,
        out_shape=(jax.ShapeDtypeStruct((t, TOP_K), I32),
                   jax.ShapeDtypeStruct((t, TOP_K), F32),
                   jax.ShapeDtypeStruct((1, LANES), F32)),
        grid=(t // tr,),
        in_specs=[pl.BlockSpec((tr, 2 * LANES), lambda i: (i, 0))],
        out_specs=(pl.BlockSpec((tr, TOP_K), lambda i: (i, 0)),
                   pl.BlockSpec((tr, TOP_K), lambda i: (i, 0)),
                   pl.BlockSpec((1, LANES), lambda i: (0, 0))),
        compiler_params=_cparams("arbitrary"),
        name="route",
    )(logits)


def _merge(o_f, o_d, g0, g1, x2, mod3, g2, wpf, wpd, wout, wr_hi, wr_lo, br, tm, seq):
    t, d = x2.shape
    tps = seq // tm
    row = lambda i: (i, 0)
    full = lambda i: (0, 0)
    return pl.pallas_call(
        _merge_kernel,
        out_shape=(jax.ShapeDtypeStruct((t, d), F32),
                   jax.ShapeDtypeStruct((t, d // 2), U32),
                   jax.ShapeDtypeStruct((t, 2 * LANES), F32)),
        grid=(t // tm,),
        in_specs=[pl.BlockSpec((tm, 512), row), pl.BlockSpec((tm, 512), row),
                  pl.BlockSpec((tm, d), row), pl.BlockSpec((tm, d), row),
                  pl.BlockSpec((tm, d), row),
                  pl.BlockSpec((1, 6, d), lambda i: (i // tps, 0, 0)),
                  pl.BlockSpec((1, d), full),
                  pl.BlockSpec(wpf.shape, full), pl.BlockSpec(wpd.shape, full),
                  pl.BlockSpec(wout.shape, full),
                  pl.BlockSpec(wr_hi.shape, full), pl.BlockSpec(wr_lo.shape, full),
                  pl.BlockSpec(br.shape, full)],
        out_specs=(pl.BlockSpec((tm, d), row), pl.BlockSpec((tm, d // 2), row),
                   pl.BlockSpec((tm, 2 * LANES), row)),
        compiler_params=_cparams("arbitrary"),
        name="merge",
    )(o_f, o_d, g0, g1, x2, mod3, g2, wpf, wpd, wout, wr_hi, wr_lo, br)


def _rank_kernel(ids_ref, pstart_ref, ltri_ref, dest_ref, carry_ref):
    i = pl.program_id(0)

    @pl.when(i == 0)
    def _():
        carry_ref[...] = jnp.zeros_like(carry_ref)

    ids = ids_ref[...]
    lane = lax.broadcasted_iota(I32, (ids.shape[0], LANES), 1)
    oh0 = lane == ids[:, 0:1]
    oh1 = lane == ids[:, 1:2]
    both = (oh0 | oh1).astype(BF16)
    before = jnp.dot(ltri_ref[...], both, preferred_element_type=F32)
    base = before + carry_ref[...] + pstart_ref[...]
    d0 = jnp.sum(jnp.where(oh0, base, 0.0), axis=-1, keepdims=True)
    d1 = jnp.sum(jnp.where(oh1, base, 0.0), axis=-1, keepdims=True)
    dest_ref[...] = jnp.where(lane == 0, d0, d1)[:, :TOP_K].astype(I32)
    carry_ref[...] += jnp.sum(both.astype(F32), axis=0, keepdims=True)


def _rank(ids, pad_start_row, tt):
    t = ids.shape[0]
    ltri = (jnp.arange(tt)[:, None] > jnp.arange(tt)[None, :]).astype(BF16)
    return pl.pallas_call(
        _rank_kernel,
        out_shape=jax.ShapeDtypeStruct((t, TOP_K), I32),
        grid=(t // tt,),
        in_specs=[pl.BlockSpec((tt, TOP_K), lambda i: (i, 0)),
                  pl.BlockSpec((1, LANES), lambda i: (0, 0)),
                  pl.BlockSpec((tt, tt), lambda i: (0, 0))],
        out_specs=pl.BlockSpec((tt, TOP_K), lambda i: (i, 0)),
        scratch_shapes=[pltpu.VMEM((1, LANES), F32)],
        compiler_params=_cparams("arbitrary"),
        name="rank",
    )(ids, pad_start_row, ltri)


def _row_copy(src_ref, s, dst_ref, d, sem):
    return pltpu.make_async_copy(src_ref.at[pl.ds(s, 1)], dst_ref.at[pl.ds(d, 1)], sem)


def _dispatch_kernel(pend_ref, padded_ref, nused_ref, dest_ref, h_ref, xs_ref, zero_ref, sem, zsem,
                     *, td, n_blocks):
    i = pl.program_id(0)

    @pl.when(i == 0)
    def _():
        zero_ref[...] = jnp.zeros_like(zero_ref)

        def zero_block(start):
            cp = pltpu.make_async_copy(
                zero_ref, xs_ref.at[pl.ds(pl.multiple_of(start, MOE_BLOCK), MOE_BLOCK)], zsem)
            cp.start()
            cp.wait()

        for e in range(N_EXPERTS):
            @pl.when(padded_ref[e] > 0)
            def _(e=e):
                zero_block(pend_ref[e] - MOE_BLOCK)

            @pl.when(nused_ref[0] + e < n_blocks)
            def _(e=e):
                zero_block((nused_ref[0] + e) * MOE_BLOCK)

    def issue(t, _):
        _row_copy(h_ref, t, xs_ref, dest_ref[2 * t], sem).start(priority=0)
        _row_copy(h_ref, t, xs_ref, dest_ref[2 * t + 1], sem).start(priority=1)
        return 0

    lax.fori_loop(0, td, issue, 0, unroll=DMA_UNROLL)

    def drain(t, _):
        _row_copy(h_ref, 0, xs_ref, 0, sem).wait()
        _row_copy(h_ref, 0, xs_ref, 0, sem).wait()
        return 0

    lax.fori_loop(0, td, drain, 0, unroll=DMA_UNROLL)


def _dispatch(pad_end, padded, n_used, dest_flat, h_packed, p_rows, td):
    t, w = h_packed.shape
    grid_spec = pltpu.PrefetchScalarGridSpec(
        num_scalar_prefetch=3,
        grid=(t // td,),
        in_specs=[pl.BlockSpec((TOP_K * td,), lambda i, *_: (i,), memory_space=pltpu.SMEM),
                  pl.BlockSpec((td, w), lambda i, *_: (i, 0))],
        out_specs=pl.BlockSpec(memory_space=pl.ANY),
        scratch_shapes=[pltpu.VMEM((MOE_BLOCK, w), U32),
                        pltpu.SemaphoreType.DMA(()), pltpu.SemaphoreType.DMA(())],
    )
    return pl.pallas_call(
        functools.partial(_dispatch_kernel, td=td, n_blocks=p_rows // MOE_BLOCK),
        out_shape=jax.ShapeDtypeStruct((p_rows, w), U32),
        grid_spec=grid_spec,
        compiler_params=_cparams("arbitrary"),
        name="dispatch",
    )(pad_end, padded, n_used, dest_flat, h_packed)


def _expert_kernel(be_ref, nused_ref, xs_ref, w1_ref, w3_ref, w2_ref, y_ref):
    p = pl.program_id(0)

    @pl.when(p < nused_ref[0])
    def _():
        words = xs_ref[...]
        half = D_MODEL // 2
        xa = pltpu.bitcast(words << 16, F32).astype(BF16)
        xb = pltpu.bitcast(words & jnp.uint32(0xFFFF0000), F32).astype(BF16)

        def up(w_ref):
            return (jnp.dot(xa, w_ref[0, :half, :], preferred_element_type=F32)
                    + jnp.dot(xb, w_ref[0, half:, :], preferred_element_type=F32))

        g = up(w1_ref)
        u = up(w3_ref)
        act = (g * jax.nn.sigmoid(g) * u).astype(BF16)
        y_ref[...] = jnp.dot(act, w2_ref[0], preferred_element_type=F32)

    @pl.when(p >= nused_ref[0])
    def _():
        y_ref[...] = jnp.zeros_like(y_ref)


def _experts(block_expert, n_used, xs, w1, w3, w2):
    p_rows, w = xs.shape
    nb = p_rows // MOE_BLOCK
    d = w1.shape[1]
    blk = lambda p, be, nu: (p, 0)
    wsel = lambda p, be, nu: (be[jnp.minimum(p, nu[0] - 1)], 0, 0)
    grid_spec = pltpu.PrefetchScalarGridSpec(
        num_scalar_prefetch=2,
        grid=(nb,),
        in_specs=[pl.BlockSpec((MOE_BLOCK, w), blk),
                  pl.BlockSpec((1, d, D_EXPERT), wsel),
                  pl.BlockSpec((1, d, D_EXPERT), wsel),
                  pl.BlockSpec((1, D_EXPERT, d), wsel)],
        out_specs=pl.BlockSpec((MOE_BLOCK, d), blk),
    )
    return pl.pallas_call(
        _expert_kernel,
        out_shape=jax.ShapeDtypeStruct((p_rows, d), F32),
        grid_spec=grid_spec,
        compiler_params=_cparams("arbitrary"),
        name="experts",
    )(block_expert, n_used, xs, w1, w3, w2)


def _combine_kernel(dest_ref, wts_ref, x1_ref, mod_ref, yb_ref, o_ref, ybuf_ref, sem, *, tc):
    def issue(t, _):
        _row_copy(yb_ref, dest_ref[2 * t], ybuf_ref.at[0], t, sem).start(priority=0)
        _row_copy(yb_ref, dest_ref[2 * t + 1], ybuf_ref.at[1], t, sem).start(priority=1)
        return 0

    lax.fori_loop(0, tc, issue, 0, unroll=DMA_UNROLL)

    def drain(t, _):
        _row_copy(yb_ref, 0, ybuf_ref.at[0], 0, sem).wait()
        _row_copy(yb_ref, 0, ybuf_ref.at[1], 0, sem).wait()
        return 0

    lax.fori_loop(0, tc, drain, 0, unroll=DMA_UNROLL)
    gt2 = mod_ref[0, 5:6, :]
    wts = wts_ref[...]
    y = ybuf_ref[0] * wts[:, 0:1] + ybuf_ref[1] * wts[:, 1:2]
    o_ref[...] = x1_ref[...] + gt2 * y


def _combine(dest_flat, wts, x1, mod3, yb, tc, seq):
    t, d = x1.shape
    tps = seq // tc
    return pl.pallas_call(
        functools.partial(_combine_kernel, tc=tc),
        out_shape=jax.ShapeDtypeStruct((t, d), F32),
        grid=(t // tc,),
        in_specs=[pl.BlockSpec((TOP_K * tc,), lambda i: (i,), memory_space=pltpu.SMEM),
                  pl.BlockSpec((tc, TOP_K), lambda i: (i, 0)),
                  pl.BlockSpec((tc, d), lambda i: (i, 0)),
                  pl.BlockSpec((1, 6, d), lambda i: (i // tps, 0, 0)),
                  pl.BlockSpec(memory_space=pl.ANY)],
        out_specs=pl.BlockSpec((tc, d), lambda i: (i, 0)),
        scratch_shapes=[pltpu.VMEM((2, tc, d), F32), pltpu.SemaphoreType.DMA(())],
        compiler_params=_cparams("arbitrary"),
        name="combine",
    )(dest_flat, wts, x1, mod3, yb)


def _tile(n, pref):
    t = pref
    while n % t:
        t //= 2
    return t


def kernel(x, c, positions, w_ada, b_ada, g_norm1, w_in, b_f, g_q_fox, g_k_fox, g_q_diff, g_k_diff,
           lam_q1, lam_k1, lam_q2, lam_k2, g_subln, w_proj_fox, w_proj_diff, w_out, g_norm2,
           w_router_group, b_router_group, w_router_expert, b_router_expert, w1, w3, w2):
    b, s, d = x.shape
    t = b * s
    assert d == D_MODEL and w_ada.shape[0] == 1 and s % LANES == 0
    tm = _tile(s, 512)
    tq = _tile(s, 512)

    wi = w_in[0]
    c_fk, c_fv, c_ff = FOX_WIDTH, 2 * FOX_WIDTH, 3 * FOX_WIDTH
    c_dq = c_ff + N_FOX_HEADS
    c_gate = c_dq + 3 * 512
    c_dv = c_dq + 2 * 512
    wqk = jnp.concatenate([wi[:, :c_fv], wi[:, c_dq:c_dv]], axis=1).astype(BF16)
    wvt = jnp.concatenate([wi[:, c_fv:c_ff], wi[:, c_dv:c_gate]], axis=1).T.astype(BF16)
    wff_t = jnp.zeros((16, d), BF16).at[:N_FOX_HEADS].set(wi[:, c_ff:c_dq].T.astype(BF16))
    wgate = wi[:, c_gate:].astype(BF16)
    bf_col = jnp.zeros((16, 1), F32).at[:N_FOX_HEADS, 0].set(b_f[0])
    head_id = jnp.arange(FOX_WIDTH) // HEAD_DIM
    bd = (head_id[:, None] == head_id[None, :]).astype(BF16)
    gvecs = jnp.stack([jnp.tile(g[0], FOX_WIDTH // HEAD_DIM)
                       for g in (g_q_fox, g_k_fox, g_q_diff, g_k_diff)])
    lam_vecs = jnp.stack([lam_q1[0], lam_k1[0], lam_q2[0], lam_k2[0]])
    inv_freq = ROPE_THETA ** (-jnp.arange(0, HEAD_DIM, 2, dtype=F32) / HEAD_DIM)
    invf_row = jnp.tile(inv_freq, LANES // (HEAD_DIM // 2)).reshape(1, LANES)
    wr = jnp.zeros((d, 2 * LANES), F32)
    wr = wr.at[:, :N_GROUPS].set(w_router_group[0]).at[:, LANES:LANES + N_EXPERTS].set(w_router_expert[0])
    wr_hi = wr.astype(BF16)
    wr_lo = (wr - wr_hi.astype(F32)).astype(BF16)
    br = jnp.zeros((1, 2 * LANES), F32)
    br = br.at[0, :N_GROUPS].set(b_router_group[0]).at[0, LANES:LANES + N_EXPERTS].set(b_router_expert[0])

    x2 = x.reshape(t, d)
    mod3 = _ada(c, w_ada[0], b_ada[0]).reshape(b, 6, d)
    cos_t, sin_t = _rope_tables(positions.reshape(t, 1), invf_row, tm)

    fq, fk, fv_t, dq, dk, dv_t, g0, g1, lf_t = _inproj(
        x2, mod3, g_norm1, wqk, wvt, wff_t, wgate, bd, gvecs, bf_col, cos_t, sin_t, tm, s)
    dec = _cumsum(lf_t, b, s)
    r3 = lambda a: a.reshape(b, s, a.shape[-1])
    o_f = _fox_attention(r3(fq), r3(fk), fv_t, dec, tq)
    o_d = _diff_attention(r3(dq), r3(dk), dv_t, lam_vecs, g_subln.reshape(2 * HEAD_DIM, 1), tq)

    x1, h_packed, logits = _merge(
        o_f.reshape(t, FOX_WIDTH), o_d.reshape(t, DIFF_WIDTH), g0, g1, x2, mod3, g_norm2,
        w_proj_fox[0].astype(BF16), w_proj_diff[0].astype(BF16), w_out[0].astype(BF16),
        wr_hi, wr_lo, br, tm, s)
    ids, wts, counts = _route(logits, _tile(t, 2048))

    a = t * TOP_K
    n_blocks = -(-a // MOE_BLOCK) + N_EXPERTS
    p_rows = n_blocks * MOE_BLOCK
    cnt = counts[0, :N_EXPERTS].astype(I32)
    padded = ((cnt + MOE_BLOCK - 1) // MOE_BLOCK) * MOE_BLOCK
    pad_end = jnp.cumsum(padded).astype(I32)
    pad_start = pad_end - padded
    block_start = jnp.arange(n_blocks, dtype=I32) * MOE_BLOCK
    block_expert = jnp.minimum(
        jnp.sum((pad_end[None, :] <= block_start[:, None]).astype(I32), axis=1), N_EXPERTS - 1)
    n_used = (pad_end[-1:] // MOE_BLOCK).astype(I32)
    pstart_row = jnp.zeros((1, LANES), F32).at[0, :N_EXPERTS].set(pad_start.astype(F32))

    dest = _rank(ids, pstart_row, _tile(t, 1024)).reshape(a)
    xs = _dispatch(pad_end, padded, n_used, dest, h_packed, p_rows, _tile(t, 512))
    yb = _experts(block_expert, n_used, xs, w1[0].astype(BF16), w3[0].astype(BF16), w2[0].astype(BF16))
    out = _combine(dest, wts, x1, mod3, yb, _tile(s, 256), s)
    return out.reshape(b, s, d)
```

```python
import functools
import math

import jax
import jax.numpy as jnp
from jax import lax
from jax.experimental import pallas as pl
from jax.experimental.pallas import tpu as pltpu

F32 = jnp.float32
BF16 = jnp.bfloat16
I32 = jnp.int32
U32 = jnp.uint32

D_MODEL = 1024
HEAD_DIM = 64
N_FOX_HEADS = 8
N_DIFF_HEADS = 4
FOX_WIDTH = 512
DIFF_WIDTH = 512
CHUNK = 64
CHUNK_SHIFT = CHUNK.bit_length() - 1
ROPE_THETA = 10000.0
N_GROUPS = 4
EXPERTS_PER_GROUP = 8
N_EXPERTS = 32
TOP_K = 2
D_EXPERT = 512
MOE_BLOCK = 256
EPS = 1e-6
NEG_INF = -1e30
LAM0 = 0.8 - 0.6 * math.exp(-0.3 * 0)
LOG2E = math.log2(math.e)
Q_SCALE = HEAD_DIM ** -0.5 * LOG2E

LANES = 128
VMEM_LIMIT = 56 * 1024 * 1024

DMA_UNROLL = 8
ONES_ROWS = 16
ROW_CHAINS = 2

_NT = (((1,), (1,)), ((), ()))


def _cparams(*sem):
    return pltpu.CompilerParams(dimension_semantics=sem, vmem_limit_bytes=VMEM_LIMIT)


def _split3(x):
    hi = x.astype(BF16)
    r1 = x - hi.astype(F32)
    mid = r1.astype(BF16)
    lo = (r1 - mid.astype(F32)).astype(BF16)
    return hi, mid, lo


def _ada_kernel(c_ref, w_ref, b_ref, o_ref):
    c = c_ref[...]
    ca = (c * jax.nn.sigmoid(c)).astype(BF16)
    o_ref[...] = jnp.dot(ca, w_ref[...].astype(BF16), preferred_element_type=F32) + b_ref[...]


def _ada(c, w_ada, b_ada):
    b, d = c.shape
    n = w_ada.shape[1]
    tn = 1024
    return pl.pallas_call(
        _ada_kernel,
        out_shape=jax.ShapeDtypeStruct((b, n), F32),
        grid=(n // tn,),
        in_specs=[pl.BlockSpec((b, d), lambda j: (0, 0)),
                  pl.BlockSpec((d, tn), lambda j: (0, j)),
                  pl.BlockSpec((1, tn), lambda j: (0, j))],
        out_specs=pl.BlockSpec((b, tn), lambda j: (0, j)),
        compiler_params=_cparams("arbitrary"),
        name="ada",
    )(c, w_ada, b_ada.reshape(1, n))


def _rope_kernel(pos_ref, invf_ref, cos_ref, sin_ref):
    ang = invf_ref[...] * pos_ref[...].astype(F32)
    c = jnp.cos(ang)
    s = jnp.sin(ang)
    cos_ref[...] = jnp.concatenate([c, c, c, c], axis=0)
    sin_ref[...] = jnp.concatenate([-s, s, -s, s], axis=0)


def _rope_tables(pos_row, invf_col, tm):
    t = pos_row.shape[1]
    tbl = jax.ShapeDtypeStruct((LANES, t), F32)
    return pl.pallas_call(
        _rope_kernel,
        out_shape=(tbl, tbl),
        grid=(t // tm,),
        in_specs=[pl.BlockSpec((1, tm), lambda i: (0, i)),
                  pl.BlockSpec((HEAD_DIM // 2, 1), lambda i: (0, 0))],
        out_specs=(pl.BlockSpec((LANES, tm), lambda i: (0, i)),
                   pl.BlockSpec((LANES, tm), lambda i: (0, i))),
        compiler_params=_cparams("arbitrary"),
        name="rope",
    )(pos_row, invf_col)


def _rms_rows(x, g):
    return x * lax.rsqrt(jnp.mean(x * x, axis=-1, keepdims=True) + EPS) * g


def _head_norm(z, bd, g):
    ss = jnp.dot((z * z).astype(BF16), bd, preferred_element_type=F32)
    return z * lax.rsqrt(ss * (1.0 / HEAD_DIM) + EPS) * g


def _rotate_half(z):
    n = z.shape[-1]
    lane = lax.broadcasted_iota(I32, z.shape, 1)
    fwd = pltpu.roll(z, n - HEAD_DIM // 2, 1)
    bwd = pltpu.roll(z, HEAD_DIM // 2, 1)
    return jnp.where((lane & (HEAD_DIM - 1)) < HEAD_DIM // 2, fwd, bwd)


def _inproj_kernel(x_ref, mod_ref, g1_ref, wqk_ref, wvt_ref, wff_ref, wg_ref, bd_ref, gv_ref, bf_ref,
                   cos_ref, sin_ref,
                   fq_ref, fk_ref, fv_ref, dq_ref, dk_ref, dv_ref, g0_ref, g1o_ref, lf_ref):
    x = x_ref[...]
    sh1 = mod_ref[0, 0:1, :]
    sc1 = mod_ref[0, 1:2, :]
    h = (_rms_rows(x, g1_ref[...]) * (1.0 + sc1) + sh1).astype(BF16)
    bd = bd_ref[...]
    w = FOX_WIDTH

    def proj(j):
        return jnp.dot(h, wqk_ref[:, j * w:(j + 1) * w], preferred_element_type=F32)

    def proj_t(j):
        return lax.dot_general(wvt_ref[j * w:(j + 1) * w, :], h, _NT, preferred_element_type=F32)

    fq_ref[...] = (_head_norm(proj(0), bd, gv_ref[0:1, :]) * Q_SCALE).astype(BF16)
    fk_ref[...] = _head_norm(proj(1), bd, gv_ref[1:2, :]).astype(BF16)
    fv_ref[...] = proj_t(0).astype(BF16)

    cos = jnp.concatenate([cos_ref[...].T] * (w // LANES), axis=1)
    sin = jnp.concatenate([sin_ref[...].T] * (w // LANES), axis=1)
    qn = _head_norm(proj(2), bd, gv_ref[2:3, :])
    dq_ref[...] = ((qn * cos + _rotate_half(qn) * sin) * Q_SCALE).astype(BF16)
    kn = _head_norm(proj(3), bd, gv_ref[3:4, :])
    dk_ref[...] = (kn * cos + _rotate_half(kn) * sin).astype(BF16)
    dv_ref[...] = proj_t(1).astype(BF16)

    for j in range(2):
        zg = jnp.dot(h, wg_ref[:, j * D_MODEL:(j + 1) * D_MODEL], preferred_element_type=F32)
        (g0_ref, g1o_ref)[j][...] = jax.nn.sigmoid(zg).astype(BF16)

    zf = lax.dot_general(wff_ref[...], h, _NT, preferred_element_type=F32) + bf_ref[...]
    lf_ref[...] = jnp.minimum(zf, 0.0) - jnp.log1p(jnp.exp(-jnp.abs(zf)))


def _inproj(x2, mod3, g1, wqk, wvt, wff_t, wgate, bd, gvecs, bf_col, cos_t, sin_t, tm, seq):
    t, d = x2.shape
    tps = seq // tm
    row = lambda i: (i, 0)
    full = lambda i: (0, 0)
    bsd = lambda n: jax.ShapeDtypeStruct((t, n), BF16)
    bsd_t = jax.ShapeDtypeStruct((512, t), BF16)
    tok = pl.BlockSpec((tm, 512), row)
    tok_t = pl.BlockSpec((512, tm), lambda i: (0, i))
    return pl.pallas_call(
        _inproj_kernel,
        out_shape=(bsd(512), bsd(512), bsd_t, bsd(512), bsd(512), bsd_t, bsd(d), bsd(d),
                   jax.ShapeDtypeStruct((16, t), F32)),
        grid=(t // tm,),
        in_specs=[pl.BlockSpec((tm, d), row),
                  pl.BlockSpec((1, 6, d), lambda i: (i // tps, 0, 0)),
                  pl.BlockSpec((1, d), full),
                  pl.BlockSpec(wqk.shape, full),
                  pl.BlockSpec(wvt.shape, full),
                  pl.BlockSpec(wff_t.shape, full),
                  pl.BlockSpec(wgate.shape, full),
                  pl.BlockSpec(bd.shape, full),
                  pl.BlockSpec(gvecs.shape, full),
                  pl.BlockSpec(bf_col.shape, full),
                  pl.BlockSpec((LANES, tm), lambda i: (0, i)),
                  pl.BlockSpec((LANES, tm), lambda i: (0, i))],
        out_specs=(tok, tok, tok_t, tok, tok, tok_t, pl.BlockSpec((tm, d), row), pl.BlockSpec((tm, d), row),
                   pl.BlockSpec((16, tm), lambda i: (0, i))),
        compiler_params=_cparams("arbitrary"),
        name="inproj",
    )(x2, mod3, g1, wqk, wvt, wff_t, wgate, bd, gvecs, bf_col, cos_t, sin_t)


def _cumsum_kernel(lf_ref, tri_ref, eye_ref, sel_ref, ones_ref, dec_ref, *, seq):
    tri = tri_ref[...]
    eye = eye_ref[...]
    sel = sel_ref[...]
    carry = jnp.zeros((16, 1), F32)
    for blk in range(seq // LANES):
        sl = slice(blk * LANES, (blk + 1) * LANES)
        hi, mid, lo = _split3(lf_ref[:, sl])
        local = (jnp.dot(hi, tri, preferred_element_type=F32)
                 + jnp.dot(mid, tri, preferred_element_type=F32)
                 + jnp.dot(lo, tri, preferred_element_type=F32))
        c = local + carry
        carry = carry + local[:, LANES - 1:LANES]
        pieces = jnp.concatenate(list(_split3(c * LOG2E)) + [jnp.zeros((16, LANES), BF16)], axis=0)
        cols = lax.dot_general(eye, pieces, _NT, preferred_element_type=F32).astype(BF16)
        dec_ref[0, sl, :] = (jnp.dot(cols, sel, preferred_element_type=F32) + ones_ref[...]).astype(BF16)


def _cumsum(lf_t, batch, seq):
    tri = (jnp.arange(LANES)[:, None] <= jnp.arange(LANES)[None, :]).astype(BF16)
    eye = jnp.eye(LANES, dtype=BF16)
    j, h = jnp.meshgrid(jnp.arange(3), jnp.arange(N_FOX_HEADS), indexing="ij")
    rows = (16 * j + h).reshape(-1)
    sel = jnp.zeros((4 * 16, LANES), F32)
    sel = sel.at[rows, (8 * h + j).reshape(-1)].set(1.0)
    sel = sel.at[rows, (HEAD_DIM + 8 * h + 3 + j).reshape(-1)].set(-1.0).astype(BF16)
    ones = jnp.zeros((1, LANES), F32)
    ones = ones.at[0, (8 * h + 3 + j).reshape(-1)].set(1.0).at[0, (HEAD_DIM + 8 * h + j).reshape(-1)].set(1.0)
    return pl.pallas_call(
        functools.partial(_cumsum_kernel, seq=seq),
        out_shape=jax.ShapeDtypeStruct((batch, seq, LANES), BF16),
        grid=(batch,),
        in_specs=[pl.BlockSpec((16, seq), lambda b: (0, b)),
                  pl.BlockSpec((LANES, LANES), lambda b: (0, 0)),
                  pl.BlockSpec((LANES, LANES), lambda b: (0, 0)),
                  pl.BlockSpec(sel.shape, lambda b: (0, 0)),
                  pl.BlockSpec((1, LANES), lambda b: (0, 0))],
        out_specs=pl.BlockSpec((1, seq, LANES), lambda b: (b, 0, 0)),
        compiler_params=_cparams("arbitrary"),
        name="cumsum",
    )(lf_t, tri, eye, sel, ones)


def _attend(n_chains, nq, q_of, k_of, vt_of, diag_mask, finalize, s_a, s_b, m_ref, acc_ref):
    n_steps = nq * (nq + 1) // 2

    def produce(qi, ki, s_ref):
        for c in range(n_chains):
            s_ref[c] = lax.dot_general(k_of(c, ki), q_of(c, qi), _NT, preferred_element_type=F32)

    def consume(qi, ki, s_ref, masked):
        par = lax.bitwise_and(qi, 1)
        for c in range(n_chains):
            s = s_ref[c]
            if masked:
                s = jnp.where(diag_mask, s, NEG_INF)
            m = jnp.where(ki == 0, NEG_INF, m_ref[par, c])
            m_new = jnp.maximum(m, jnp.max(s, axis=0, keepdims=True))
            p = jnp.exp2(s - m_new).astype(BF16)
            acc_ref[par, c] = (jnp.exp2(m - m_new) * acc_ref[par, c]
                               + jnp.dot(vt_of(c, ki), p, preferred_element_type=F32))
            m_ref[par, c] = m_new

    def advance(qi, ki):
        last = (ki == qi).astype(I32)
        return qi + last, (ki + 1) * (1 - last)

    def two_steps(qi, ki, has_next):
        q1, k1 = advance(qi, ki)
        q2, k2 = advance(q1, k1)
        d0 = ki == qi
        d1 = k1 == q1

        def block(mask0, mask1):
            produce(q1, k1, s_b)
            consume(qi, ki, s_a, mask0)
            if has_next:
                produce(q2, k2, s_a)
            consume(q1, k1, s_b, mask1)

        pl.when(d0)(lambda: block(True, False))
        pl.when(d1)(lambda: block(False, True))
        pl.when(jnp.logical_not(jnp.logical_or(d0, d1)))(lambda: block(False, False))
        pl.when(d0)(lambda: finalize(qi))
        pl.when(d1)(lambda: finalize(q1))
        return q2, k2

    m_ref[...] = jnp.full(m_ref.shape, NEG_INF, F32)
    acc_ref[...] = jnp.zeros(acc_ref.shape, F32)
    produce(0, 0, s_a)
    n_pairs = (n_steps - 1) // 2
    qi, ki = lax.fori_loop(0, n_pairs, lambda _, carry: two_steps(*carry, True),
                           (jnp.int32(0), jnp.int32(0)))
    if n_steps - 2 * n_pairs == 2:
        two_steps(qi, ki, False)
    else:
        consume(qi, ki, s_a, True)
        finalize(qi)


def _attn_scratch(tq, rows):
    return [pltpu.VMEM((2, tq, tq), F32), pltpu.VMEM((2, tq, tq), F32),
            pltpu.VMEM((2, 2, 1, tq), F32), pltpu.VMEM((2, 2, rows, tq), F32)]


def _fox_kernel(q_ref, k_ref, vt_ref, dec_ref, o_ref, qaug_ref, kaug_ref, vtaug_ref, s_a, s_b,
                m_ref, acc_ref, *, tq):
    hp = pl.program_id(1)
    seq = q_ref.shape[1]
    nq = seq // tq
    qdec = dec_ref[0, :, :HEAD_DIM]
    kdec = dec_ref[0, :, HEAD_DIM:].astype(F32)
    lane_head = lax.shift_right_logical(lax.broadcasted_iota(I32, kdec.shape, 1), 3)
    ones = jnp.ones((ONES_ROWS, tq), BF16)
    for hh in range(2):
        hs = slice(hh * HEAD_DIM, (hh + 1) * HEAD_DIM)
        kd = jnp.where(lane_head == 2 * hp + hh, kdec, 0.0).astype(BF16)
        kaug_ref[hh] = jnp.concatenate([k_ref[0, :, hs], kd], axis=-1)
        qaug_ref[hh] = jnp.concatenate([q_ref[0, :, hs], qdec], axis=-1)
        for j in range(nq):
            vtaug_ref[hh, j] = jnp.concatenate([vt_ref[hs, j * tq:(j + 1) * tq], ones], axis=0)

    key = lax.broadcasted_iota(I32, (tq, tq), 0)
    qry = lax.broadcasted_iota(I32, (tq, tq), 1)

    def tile(ref, c, i):
        return ref[c, pl.ds(pl.multiple_of(i * tq, tq), tq), :]

    def finalize(qi):
        outs = []
        for c in range(2):
            acc = acc_ref[lax.bitwise_and(qi, 1), c]
            outs.append(acc[:HEAD_DIM] / acc[HEAD_DIM:HEAD_DIM + 1])
        o_t = jnp.concatenate(outs, axis=0)
        o_ref[0, pl.ds(pl.multiple_of(qi * tq, tq), tq), :] = o_t.T.astype(BF16)

    _attend(2, nq, functools.partial(tile, qaug_ref), functools.partial(tile, kaug_ref),
            lambda c, ki: vtaug_ref[c, ki], key <= qry, finalize, s_a, s_b, m_ref, acc_ref)


def _fox_attention(fq, fk, fv_t, dec, tq):
    b, s, _ = fq.shape
    head_pair = pl.BlockSpec((1, s, LANES), lambda bi, hp: (bi, 0, hp))
    return pl.pallas_call(
        functools.partial(_fox_kernel, tq=tq),
        out_shape=jax.ShapeDtypeStruct((b, s, FOX_WIDTH), BF16),
        grid=(b, N_FOX_HEADS // 2),
        in_specs=[head_pair, head_pair,
                  pl.BlockSpec((LANES, s), lambda bi, hp: (hp, bi)),
                  pl.BlockSpec((1, s, LANES), lambda bi, hp: (bi, 0, 0))],
        out_specs=head_pair,
        scratch_shapes=[pltpu.VMEM((2, s, 2 * HEAD_DIM), BF16), pltpu.VMEM((2, s, 2 * HEAD_DIM), BF16),
                        pltpu.VMEM((2, s // tq, HEAD_DIM + ONES_ROWS, tq), BF16)]
        + _attn_scratch(tq, HEAD_DIM + ONES_ROWS),
        compiler_params=_cparams("arbitrary", "arbitrary"),
        name="fox",
    )(fq, fk, fv_t, dec)


def _diff_kernel(q_ref, k_ref, vt_ref, lam_ref, gs_ref, o_ref, vtaug_ref, s_a, s_b, m_ref, acc_ref, *, tq):
    dv = 2 * HEAD_DIM
    seq = q_ref.shape[1]
    nq = seq // tq
    ones = jnp.ones((ONES_ROWS, tq), BF16)
    for j in range(nq):
        vtaug_ref[j] = jnp.concatenate([vt_ref[:, j * tq:(j + 1) * tq], ones], axis=0)

    key = lax.broadcasted_iota(I32, (tq, tq), 0)
    qry = lax.broadcasted_iota(I32, (tq, tq), 1)
    chunk_causal = lax.shift_right_logical(key, CHUNK_SHIFT) <= lax.shift_right_logical(qry, CHUNK_SHIFT)
    lv = lam_ref[...]
    lam = (jnp.exp(jnp.sum(lv[0:1] * lv[1:2], axis=-1, keepdims=True))
           - jnp.exp(jnp.sum(lv[2:3] * lv[3:4], axis=-1, keepdims=True)) + LAM0)

    def rows(i):
        return pl.ds(pl.multiple_of(i * tq, tq), tq)

    def q_of(c, qi):
        return q_ref[0, rows(qi), c * HEAD_DIM:(c + 1) * HEAD_DIM]

    def k_of(c, ki):
        return k_ref[0, rows(ki), c * HEAD_DIM:(c + 1) * HEAD_DIM]

    def finalize(qi):
        par = lax.bitwise_and(qi, 1)
        outs = [acc_ref[par, c][:dv] / acc_ref[par, c][dv:dv + 1] for c in range(2)]
        o_t = outs[0] - lam * outs[1]
        inv = lax.rsqrt(jnp.mean(o_t * o_t, axis=0, keepdims=True) + EPS)
        o_t = o_t * inv * gs_ref[...] * (1.0 - LAM0)
        o_ref[0, rows(qi), :] = o_t.T.astype(BF16)

    _attend(2, nq, q_of, k_of, lambda c, ki: vtaug_ref[ki], chunk_causal, finalize,
            s_a, s_b, m_ref, acc_ref)


def _diff_attention(dq, dk, dv_t, lam_vecs, g_subln_col, tq):
    b, s, _ = dq.shape
    head = pl.BlockSpec((1, s, LANES), lambda bi, h: (bi, 0, h))
    rows = 2 * HEAD_DIM + ONES_ROWS
    return pl.pallas_call(
        functools.partial(_diff_kernel, tq=tq),
        out_shape=jax.ShapeDtypeStruct((b, s, DIFF_WIDTH), BF16),
        grid=(b, N_DIFF_HEADS),
        in_specs=[head, head,
                  pl.BlockSpec((LANES, s), lambda bi, h: (h, bi)),
                  pl.BlockSpec((4, HEAD_DIM), lambda bi, h: (0, 0)),
                  pl.BlockSpec((2 * HEAD_DIM, 1), lambda bi, h: (0, 0))],
        out_specs=head,
        scratch_shapes=[pltpu.VMEM((s // tq, rows, tq), BF16)] + _attn_scratch(tq, rows),
        compiler_params=_cparams("arbitrary", "arbitrary"),
        name="diff",
    )(dq, dk, dv_t, lam_vecs, g_subln_col)


def _merge_kernel(of_ref, od_ref, g0_ref, g1_ref, x_ref, mod_ref, g2_ref, wpf_ref, wpd_ref, wout_ref,
                  wrh_ref, wrl_ref, br_ref,
                  x1_ref, hp_ref, lg_ref):
    gt1 = mod_ref[0, 2:3, :]
    sh2 = mod_ref[0, 3:4, :]
    sc2 = mod_ref[0, 4:5, :]
    sub = x_ref.shape[0] // ROW_CHAINS
    for r in range(ROW_CHAINS):
        rows = pl.ds(r * sub, sub)
        a = jnp.dot(of_ref[rows, :], wpf_ref[...], preferred_element_type=F32)
        b = jnp.dot(od_ref[rows, :], wpd_ref[...], preferred_element_type=F32)
        merged = g0_ref[rows, :].astype(F32) * a + g1_ref[rows, :].astype(F32) * b
        x1 = x_ref[rows, :] + gt1 * jnp.dot(merged.astype(BF16), wout_ref[...], preferred_element_type=F32)
        x1_ref[rows, :] = x1
        h2 = _rms_rows(x1, g2_ref[...]) * (1.0 + sc2) + sh2

        hi = h2.astype(BF16)
        hf = hi.astype(F32)
        half = D_MODEL // 2
        lo_bits = pltpu.bitcast(hf[:, :half], U32) >> 16
        hi_bits = pltpu.bitcast(hf[:, half:], U32) & jnp.uint32(0xFFFF0000)
        hp_ref[rows, :] = lo_bits | hi_bits

        lo = (h2 - hf).astype(BF16)
        wrh = wrh_ref[...]
        lg_ref[rows, :] = (jnp.dot(hi, wrh, preferred_element_type=F32)
                           + jnp.dot(lo, wrh, preferred_element_type=F32)
                           + jnp.dot(hi, wrl_ref[...], preferred_element_type=F32)) + br_ref[...]


def _route_kernel(lg_ref, ids_ref, wts_ref, cnt_ref):
    i = pl.program_id(0)
    lg = lg_ref[:, :LANES]
    le = lg_ref[:, LANES:]
    lane = lax.broadcasted_iota(I32, lg.shape, 1)
    big = jnp.int32(1 << 20)

    def softmax_masked(z, mask):
        zm = jnp.where(mask, z, -jnp.inf)
        e = jnp.exp(zm - jnp.max(zm, axis=-1, keepdims=True))
        return e / jnp.sum(e, axis=-1, keepdims=True)

    def top1(p, mask):
        pm = jnp.where(mask, p, -1.0)
        best = jnp.max(pm, axis=-1, keepdims=True)
        idx = jnp.min(jnp.where(pm == best, lane, big), axis=-1, keepdims=True)
        return best, idx

    gmask = lane < N_GROUPS
    g_w, g_idx = top1(softmax_masked(lg, gmask), gmask)
    emask = (lane >> 3) == g_idx
    p_exp = softmax_masked(le, emask)
    p1, i1 = top1(p_exp, emask)
    p2, i2 = top1(p_exp, emask & (lane != i1))
    denom = p1 + p2
    w1 = g_w * (p1 / denom)
    w2 = g_w * (p2 / denom)
    ids_ref[...] = jnp.where(lane == 0, i1, i2)[:, :TOP_K]
    wts_ref[...] = jnp.where(lane == 0, w1, w2)[:, :TOP_K]

    @pl.when(i == 0)
    def _():
        cnt_ref[...] = jnp.zeros_like(cnt_ref)

    cnt_ref[...] += jnp.sum(((lane == i1) | (lane == i2)).astype(F32), axis=0, keepdims=True)


def _route(logits, tr):
    t = logits.shape[0]
    return pl.pallas_call(
        _route_kernel,
        out_shape=(jax.ShapeDtypeStruct((t, TOP_K), I32),
                   jax.ShapeDtypeStruct((t, TOP_K), F32),
                   jax.ShapeDtypeStruct((1, LANES), F32)),
        grid=(t // tr,),
        in_specs=[pl.BlockSpec((tr, 2 * LANES), lambda i: (i, 0))],
        out_specs=(pl.BlockSpec((tr, TOP_K), lambda i: (i, 0)),
                   pl.BlockSpec((tr, TOP_K), lambda i: (i, 0)),
                   pl.BlockSpec((1, LANES), lambda i: (0, 0))),
        compiler_params=_cparams("arbitrary"),
        name="route",
    )(logits)


def _merge(o_f, o_d, g0, g1, x2, mod3, g2, wpf, wpd, wout, wr_hi, wr_lo, br, tm, seq):
    t, d = x2.shape
    tps = seq // tm
    row = lambda i: (i, 0)
    full = lambda i: (0, 0)
    return pl.pallas_call(
        _merge_kernel,
        out_shape=(jax.ShapeDtypeStruct((t, d), F32),
                   jax.ShapeDtypeStruct((t, d // 2), U32),
                   jax.ShapeDtypeStruct((t, 2 * LANES), F32)),
        grid=(t // tm,),
        in_specs=[pl.BlockSpec((tm, 512), row), pl.BlockSpec((tm, 512), row),
                  pl.BlockSpec((tm, d), row), pl.BlockSpec((tm, d), row),
                  pl.BlockSpec((tm, d), row),
                  pl.BlockSpec((1, 6, d), lambda i: (i // tps, 0, 0)),
                  pl.BlockSpec((1, d), full),
                  pl.BlockSpec(wpf.shape, full), pl.BlockSpec(wpd.shape, full),
                  pl.BlockSpec(wout.shape, full),
                  pl.BlockSpec(wr_hi.shape, full), pl.BlockSpec(wr_lo.shape, full),
                  pl.BlockSpec(br.shape, full)],
        out_specs=(pl.BlockSpec((tm, d), row), pl.BlockSpec((tm, d // 2), row),
                   pl.BlockSpec((tm, 2 * LANES), row)),
        compiler_params=_cparams("arbitrary"),
        name="merge",
    )(o_f, o_d, g0, g1, x2, mod3, g2, wpf, wpd, wout, wr_hi, wr_lo, br)


def _rank_kernel(ids_ref, pstart_ref, ltri_ref, dest_ref, carry_ref):
    i = pl.program_id(0)

    @pl.when(i == 0)
    def _():
        carry_ref[...] = jnp.zeros_like(carry_ref)

    ids = ids_ref[...]
    lane = lax.broadcasted_iota(I32, (ids.shape[0], LANES), 1)
    oh0 = lane == ids[:, 0:1]
    oh1 = lane == ids[:, 1:2]
    both = (oh0 | oh1).astype(BF16)
    before = jnp.dot(ltri_ref[...], both, preferred_element_type=F32)
    base = before + carry_ref[...] + pstart_ref[...]
    d0 = jnp.sum(jnp.where(oh0, base, 0.0), axis=-1, keepdims=True)
    d1 = jnp.sum(jnp.where(oh1, base, 0.0), axis=-1, keepdims=True)
    dest_ref[...] = jnp.where(lane == 0, d0, d1)[:, :TOP_K].astype(I32)
    carry_ref[...] += jnp.sum(both.astype(F32), axis=0, keepdims=True)


def _rank(ids, pad_start_row, tt):
    t = ids.shape[0]
    ltri = (jnp.arange(tt)[:, None] > jnp.arange(tt)[None, :]).astype(BF16)
    return pl.pallas_call(
        _rank_kernel,
        out_shape=jax.ShapeDtypeStruct((t, TOP_K), I32),
        grid=(t // tt,),
        in_specs=[pl.BlockSpec((tt, TOP_K), lambda i: (i, 0)),
                  pl.BlockSpec((1, LANES), lambda i: (0, 0)),
                  pl.BlockSpec((tt, tt), lambda i: (0, 0))],
        out_specs=pl.BlockSpec((tt, TOP_K), lambda i: (i, 0)),
        scratch_shapes=[pltpu.VMEM((1, LANES), F32)],
        compiler_params=_cparams("arbitrary"),
        name="rank",
    )(ids, pad_start_row, ltri)


def _row_copy(src_ref, s, dst_ref, d, sem):
    return pltpu.make_async_copy(src_ref.at[pl.ds(s, 1)], dst_ref.at[pl.ds(d, 1)], sem)


def _dispatch_kernel(pend_ref, padded_ref, nused_ref, dest_ref, h_ref, xs_ref, zero_ref, sem, zsem,
                     *, td, n_blocks):
    i = pl.program_id(0)

    @pl.when(i == 0)
    def _():
        zero_ref[...] = jnp.zeros_like(zero_ref)

        def zero_block(start):
            cp = pltpu.make_async_copy(
                zero_ref, xs_ref.at[pl.ds(pl.multiple_of(start, MOE_BLOCK), MOE_BLOCK)], zsem)
            cp.start()
            cp.wait()

        for e in range(N_EXPERTS):
            @pl.when(padded_ref[e] > 0)
            def _(e=e):
                zero_block(pend_ref[e] - MOE_BLOCK)

            @pl.when(nused_ref[0] + e < n_blocks)
            def _(e=e):
                zero_block((nused_ref[0] + e) * MOE_BLOCK)

    def issue(t, _):
        _row_copy(h_ref, t, xs_ref, dest_ref[2 * t], sem).start(priority=0)
        _row_copy(h_ref, t, xs_ref, dest_ref[2 * t + 1], sem).start(priority=1)
        return 0

    lax.fori_loop(0, td, issue, 0, unroll=DMA_UNROLL)

    def drain(t, _):
        _row_copy(h_ref, 0, xs_ref, 0, sem).wait()
        _row_copy(h_ref, 0, xs_ref, 0, sem).wait()
        return 0

    lax.fori_loop(0, td, drain, 0, unroll=DMA_UNROLL)


def _dispatch(pad_end, padded, n_used, dest_flat, h_packed, p_rows, td):
    t, w = h_packed.shape
    grid_spec = pltpu.PrefetchScalarGridSpec(
        num_scalar_prefetch=3,
        grid=(t // td,),
        in_specs=[pl.BlockSpec((TOP_K * td,), lambda i, *_: (i,), memory_space=pltpu.SMEM),
                  pl.BlockSpec((td, w), lambda i, *_: (i, 0))],
        out_specs=pl.BlockSpec(memory_space=pl.ANY),
        scratch_shapes=[pltpu.VMEM((MOE_BLOCK, w), U32),
                        pltpu.SemaphoreType.DMA(()), pltpu.SemaphoreType.DMA(())],
    )
    return pl.pallas_call(
        functools.partial(_dispatch_kernel, td=td, n_blocks=p_rows // MOE_BLOCK),
        out_shape=jax.ShapeDtypeStruct((p_rows, w), U32),
        grid_spec=grid_spec,
        compiler_params=_cparams("arbitrary"),
        name="dispatch",
    )(pad_end, padded, n_used, dest_flat, h_packed)


def _expert_kernel(be_ref, nused_ref, xs_ref, w1_ref, w3_ref, w2_ref, y_ref, w1b_ref, w3b_ref, w2b_ref):
    p = pl.program_id(0)
    active = p < nused_ref[0]
    new_expert = jnp.logical_or(p == 0, be_ref[p] != be_ref[jnp.maximum(p - 1, 0)])

    @pl.when(jnp.logical_and(active, new_expert))
    def _():
        w1b_ref[...] = w1_ref[0].astype(BF16)
        w3b_ref[...] = w3_ref[0].astype(BF16)
        w2b_ref[...] = w2_ref[0].astype(BF16)

    @pl.when(active)
    def _():
        words = xs_ref[...]
        half = D_MODEL // 2
        xa = pltpu.bitcast(words << 16, F32).astype(BF16)
        xb = pltpu.bitcast(words & jnp.uint32(0xFFFF0000), F32).astype(BF16)

        def up(w_ref):
            return (jnp.dot(xa, w_ref[:half, :], preferred_element_type=F32)
                    + jnp.dot(xb, w_ref[half:, :], preferred_element_type=F32))

        g = up(w1b_ref)
        u = up(w3b_ref)
        act = (g * jax.nn.sigmoid(g) * u).astype(BF16)
        y_ref[...] = jnp.dot(act, w2b_ref[...], preferred_element_type=F32)

    @pl.when(p >= nused_ref[0])
    def _():
        y_ref[...] = jnp.zeros_like(y_ref)


def _experts(block_expert, n_used, xs, w1, w3, w2):
    p_rows, w = xs.shape
    nb = p_rows // MOE_BLOCK
    d = w1.shape[1]
    blk = lambda p, be, nu: (p, 0)
    wsel = lambda p, be, nu: (be[jnp.minimum(p, nu[0] - 1)], 0, 0)
    grid_spec = pltpu.PrefetchScalarGridSpec(
        num_scalar_prefetch=2,
        grid=(nb,),
        in_specs=[pl.BlockSpec((MOE_BLOCK, w), blk),
                  pl.BlockSpec((1, d, D_EXPERT), wsel),
                  pl.BlockSpec((1, d, D_EXPERT), wsel),
                  pl.BlockSpec((1, D_EXPERT, d), wsel)],
        out_specs=pl.BlockSpec((MOE_BLOCK, d), blk),
        scratch_shapes=[pltpu.VMEM((d, D_EXPERT), BF16), pltpu.VMEM((d, D_EXPERT), BF16),
                        pltpu.VMEM((D_EXPERT, d), BF16)],
    )
    return pl.pallas_call(
        _expert_kernel,
        out_shape=jax.ShapeDtypeStruct((p_rows, d), F32),
        grid_spec=grid_spec,
        compiler_params=_cparams("arbitrary"),
        name="experts",
    )(block_expert, n_used, xs, w1, w3, w2)


def _combine_kernel(dest_ref, wts_ref, x1_ref, mod_ref, yb_ref, o_ref, ybuf_ref, sem, *, tc):
    def issue(t, _):
        _row_copy(yb_ref, dest_ref[2 * t], ybuf_ref.at[0], t, sem).start(priority=0)
        _row_copy(yb_ref, dest_ref[2 * t + 1], ybuf_ref.at[1], t, sem).start(priority=1)
        return 0

    lax.fori_loop(0, tc, issue, 0, unroll=DMA_UNROLL)

    def drain(t, _):
        _row_copy(yb_ref, 0, ybuf_ref.at[0], 0, sem).wait()
        _row_copy(yb_ref, 0, ybuf_ref.at[1], 0, sem).wait()
        return 0

    lax.fori_loop(0, tc, drain, 0, unroll=DMA_UNROLL)
    gt2 = mod_ref[0, 5:6, :]
    wts = wts_ref[...]
    y = ybuf_ref[0] * wts[:, 0:1] + ybuf_ref[1] * wts[:, 1:2]
    o_ref[...] = x1_ref[...] + gt2 * y


def _combine(dest_flat, wts, x1, mod3, yb, tc, seq):
    t, d = x1.shape
    tps = seq // tc
    return pl.pallas_call(
        functools.partial(_combine_kernel, tc=tc),
        out_shape=jax.ShapeDtypeStruct((t, d), F32),
        grid=(t // tc,),
        in_specs=[pl.BlockSpec((TOP_K * tc,), lambda i: (i,), memory_space=pltpu.SMEM),
                  pl.BlockSpec((tc, TOP_K), lambda i: (i, 0)),
                  pl.BlockSpec((tc, d), lambda i: (i, 0)),
                  pl.BlockSpec((1, 6, d), lambda i: (i // tps, 0, 0)),
                  pl.BlockSpec(memory_space=pl.ANY)],
        out_specs=pl.BlockSpec((tc, d), lambda i: (i, 0)),
        scratch_shapes=[pltpu.VMEM((2, tc, d), F32), pltpu.SemaphoreType.DMA(())],
        compiler_params=_cparams("arbitrary"),
        name="combine",
    )(dest_flat, wts, x1, mod3, yb)


def _tile(n, pref):
    t = pref
    while n % t:
        t //= 2
    return t


def kernel(x, c, positions, w_ada, b_ada, g_norm1, w_in, b_f, g_q_fox, g_k_fox, g_q_diff, g_k_diff,
           lam_q1, lam_k1, lam_q2, lam_k2, g_subln, w_proj_fox, w_proj_diff, w_out, g_norm2,
           w_router_group, b_router_group, w_router_expert, b_router_expert, w1, w3, w2):
    b, s, d = x.shape
    t = b * s
    assert d == D_MODEL and w_ada.shape[0] == 1 and s % LANES == 0
    tm = _tile(s, 512)
    tq = _tile(s, 512)

    wi = w_in[0]
    c_fk, c_fv, c_ff = FOX_WIDTH, 2 * FOX_WIDTH, 3 * FOX_WIDTH
    c_dq = c_ff + N_FOX_HEADS
    c_gate = c_dq + 3 * 512
    c_dv = c_dq + 2 * 512
    wqk = jnp.concatenate([wi[:, :c_fv], wi[:, c_dq:c_dv]], axis=1).astype(BF16)
    wvt = jnp.concatenate([wi[:, c_fv:c_ff], wi[:, c_dv:c_gate]], axis=1).T.astype(BF16)
    wff_t = jnp.zeros((16, d), BF16).at[:N_FOX_HEADS].set(wi[:, c_ff:c_dq].T.astype(BF16))
    wgate = wi[:, c_gate:].astype(BF16)
    bf_col = jnp.zeros((16, 1), F32).at[:N_FOX_HEADS, 0].set(b_f[0])
    head_id = jnp.arange(FOX_WIDTH) // HEAD_DIM
    bd = (head_id[:, None] == head_id[None, :]).astype(BF16)
    gvecs = jnp.stack([jnp.tile(g[0], FOX_WIDTH // HEAD_DIM)
                       for g in (g_q_fox, g_k_fox, g_q_diff, g_k_diff)])
    lam_vecs = jnp.stack([lam_q1[0], lam_k1[0], lam_q2[0], lam_k2[0]])
    inv_freq = ROPE_THETA ** (-jnp.arange(0, HEAD_DIM, 2, dtype=F32) / HEAD_DIM)
    invf_col = inv_freq.reshape(HEAD_DIM // 2, 1)
    wr = jnp.zeros((d, 2 * LANES), F32)
    wr = wr.at[:, :N_GROUPS].set(w_router_group[0]).at[:, LANES:LANES + N_EXPERTS].set(w_router_expert[0])
    wr_hi = wr.astype(BF16)
    wr_lo = (wr - wr_hi.astype(F32)).astype(BF16)
    br = jnp.zeros((1, 2 * LANES), F32)
    br = br.at[0, :N_GROUPS].set(b_router_group[0]).at[0, LANES:LANES + N_EXPERTS].set(b_router_expert[0])

    x2 = x.reshape(t, d)
    mod3 = _ada(c, w_ada[0], b_ada[0]).reshape(b, 6, d)
    cos_t, sin_t = _rope_tables(positions.reshape(1, t), invf_col, _tile(t, 2048))

    fq, fk, fv_t, dq, dk, dv_t, g0, g1, lf_t = _inproj(
        x2, mod3, g_norm1, wqk, wvt, wff_t, wgate, bd, gvecs, bf_col, cos_t, sin_t, tm, s)
    dec = _cumsum(lf_t, b, s)
    r3 = lambda a: a.reshape(b, s, a.shape[-1])
    o_f = _fox_attention(r3(fq), r3(fk), fv_t, dec, tq)
    o_d = _diff_attention(r3(dq), r3(dk), dv_t, lam_vecs, g_subln.reshape(2 * HEAD_DIM, 1), tq)

    x1, h_packed, logits = _merge(
        o_f.reshape(t, FOX_WIDTH), o_d.reshape(t, DIFF_WIDTH), g0, g1, x2, mod3, g_norm2,
        w_proj_fox[0].astype(BF16), w_proj_diff[0].astype(BF16), w_out[0].astype(BF16),
        wr_hi, wr_lo, br, tm, s)
    ids, wts, counts = _route(logits, _tile(t, 2048))

    a = t * TOP_K
    n_blocks = -(-a // MOE_BLOCK) + N_EXPERTS
    p_rows = n_blocks * MOE_BLOCK
    cnt = counts[0, :N_EXPERTS].astype(I32)
    padded = ((cnt + MOE_BLOCK - 1) // MOE_BLOCK) * MOE_BLOCK
    pad_end = jnp.cumsum(padded).astype(I32)
    pad_start = pad_end - padded
    block_start = jnp.arange(n_blocks, dtype=I32) * MOE_BLOCK
    block_expert = jnp.minimum(
        jnp.sum((pad_end[None, :] <= block_start[:, None]).astype(I32), axis=1), N_EXPERTS - 1)
    n_used = (pad_end[-1:] // MOE_BLOCK).astype(I32)
    pstart_row = jnp.zeros((1, LANES), F32).at[0, :N_EXPERTS].set(pad_start.astype(F32))

    dest = _rank(ids, pstart_row, _tile(t, 1024)).reshape(a)
    xs = _dispatch(pad_end, padded, n_used, dest, h_packed, p_rows, _tile(t, 512))
    yb = _experts(block_expert, n_used, xs, w1[0], w3[0], w2[0])
    out = _combine(dest, wts, x1, mod3, yb, _tile(s, 256), s)
    return out.reshape(b, s, d)
```

```python
import functools
import math

import jax
import jax.numpy as jnp
from jax import lax
from jax.experimental import pallas as pl
from jax.experimental.pallas import tpu as pltpu

F32 = jnp.float32
BF16 = jnp.bfloat16
I32 = jnp.int32
U32 = jnp.uint32

D_MODEL = 1024
HEAD_DIM = 64
N_FOX_HEADS = 8
N_DIFF_HEADS = 4
FOX_WIDTH = 512
DIFF_WIDTH = 512
CHUNK = 64
CHUNK_SHIFT = CHUNK.bit_length() - 1
ROPE_THETA = 10000.0
N_GROUPS = 4
EXPERTS_PER_GROUP = 8
N_EXPERTS = 32
TOP_K = 2
D_EXPERT = 512
MOE_BLOCK = 512
EPS = 1e-6
NEG_INF = -1e30
LAM0 = 0.8 - 0.6 * math.exp(-0.3 * 0)
LOG2E = math.log2(math.e)
Q_SCALE = HEAD_DIM ** -0.5 * LOG2E

LANES = 128
MXU_TILE = 256
VMEM_LIMIT = 56 * 1024 * 1024

DMA_UNROLL = 8
ONES_ROWS = 16
ROW_CHAINS = 2

_NT = (((1,), (1,)), ((), ()))


def _cparams(*sem):
    return pltpu.CompilerParams(dimension_semantics=sem, vmem_limit_bytes=VMEM_LIMIT)


def _split3(x):
    hi = x.astype(BF16)
    r1 = x - hi.astype(F32)
    mid = r1.astype(BF16)
    lo = (r1 - mid.astype(F32)).astype(BF16)
    return hi, mid, lo


def _ada_kernel(c_ref, w_ref, b_ref, o_ref):
    c = c_ref[...]
    ca = (c * jax.nn.sigmoid(c)).astype(BF16)
    o_ref[...] = jnp.dot(ca, w_ref[...].astype(BF16), preferred_element_type=F32) + b_ref[...]


def _ada(c, w_ada, b_ada):
    b, d = c.shape
    n = w_ada.shape[1]
    tn = 1024
    return pl.pallas_call(
        _ada_kernel,
        out_shape=jax.ShapeDtypeStruct((b, n), F32),
        grid=(n // tn,),
        in_specs=[pl.BlockSpec((b, d), lambda j: (0, 0)),
                  pl.BlockSpec((d, tn), lambda j: (0, j)),
                  pl.BlockSpec((1, tn), lambda j: (0, j))],
        out_specs=pl.BlockSpec((b, tn), lambda j: (0, j)),
        compiler_params=_cparams("arbitrary"),
        name="ada",
    )(c, w_ada, b_ada.reshape(1, n))


def _rope_kernel(pos_ref, invf_ref, cos_ref, sin_ref):
    ang = invf_ref[...] * pos_ref[...].astype(F32)
    c = jnp.cos(ang)
    s = jnp.sin(ang)
    cos_ref[...] = jnp.concatenate([c, c, c, c], axis=0)
    sin_ref[...] = jnp.concatenate([-s, s, -s, s], axis=0)


def _rope_tables(pos_row, invf_col, tm):
    t = pos_row.shape[1]
    tbl = jax.ShapeDtypeStruct((LANES, t), F32)
    return pl.pallas_call(
        _rope_kernel,
        out_shape=(tbl, tbl),
        grid=(t // tm,),
        in_specs=[pl.BlockSpec((1, tm), lambda i: (0, i)),
                  pl.BlockSpec((HEAD_DIM // 2, 1), lambda i: (0, 0))],
        out_specs=(pl.BlockSpec((LANES, tm), lambda i: (0, i)),
                   pl.BlockSpec((LANES, tm), lambda i: (0, i))),
        compiler_params=_cparams("arbitrary"),
        name="rope",
    )(pos_row, invf_col)


def _rms_rows(x, g):
    return x * lax.rsqrt(jnp.mean(x * x, axis=-1, keepdims=True) + EPS) * g


def _head_norm(z, bd, g):
    zz = (z * z).astype(BF16)
    n = bd.shape[0]
    ss = jnp.concatenate([jnp.dot(zz[:, j:j + n], bd, preferred_element_type=F32)
                          for j in range(0, z.shape[1], n)], axis=1)
    return z * lax.rsqrt(ss * (1.0 / HEAD_DIM) + EPS) * g


def _rotate_half(z):
    n = z.shape[-1]
    lane = lax.broadcasted_iota(I32, z.shape, 1)
    fwd = pltpu.roll(z, n - HEAD_DIM // 2, 1)
    bwd = pltpu.roll(z, HEAD_DIM // 2, 1)
    return jnp.where((lane & (HEAD_DIM - 1)) < HEAD_DIM // 2, fwd, bwd)


def _inproj_kernel(x_ref, mod_ref, g1_ref, wqk_ref, wvt_ref, wff_ref, wg_ref, bd_ref, gv_ref, bf_ref,
                   cos_ref, sin_ref,
                   fq_ref, fk_ref, fv_ref, dq_ref, dk_ref, dv_ref, g0_ref, g1o_ref, lf_ref):
    x = x_ref[...]
    sh1 = mod_ref[0, 0:1, :]
    sc1 = mod_ref[0, 1:2, :]
    h = (_rms_rows(x, g1_ref[...]) * (1.0 + sc1) + sh1).astype(BF16)
    bd = bd_ref[...]
    w = FOX_WIDTH

    def proj(j):
        return jnp.dot(h, wqk_ref[:, j * w:(j + 1) * w], preferred_element_type=F32)

    def proj_t(j):
        return lax.dot_general(wvt_ref[j * w:(j + 1) * w, :], h, _NT, preferred_element_type=F32)

    fq_ref[...] = (_head_norm(proj(0), bd, gv_ref[0:1, :]) * Q_SCALE).astype(BF16)
    fk_ref[...] = _head_norm(proj(1), bd, gv_ref[1:2, :]).astype(BF16)
    fv_ref[...] = proj_t(0).astype(BF16)

    cos = jnp.concatenate([cos_ref[...].T] * (w // LANES), axis=1)
    sin = jnp.concatenate([sin_ref[...].T] * (w // LANES), axis=1)
    qn = _head_norm(proj(2), bd, gv_ref[2:3, :])
    dq_ref[...] = ((qn * cos + _rotate_half(qn) * sin) * Q_SCALE).astype(BF16)
    kn = _head_norm(proj(3), bd, gv_ref[3:4, :])
    dk_ref[...] = (kn * cos + _rotate_half(kn) * sin).astype(BF16)
    dv_ref[...] = proj_t(1).astype(BF16)

    for j in range(2):
        zg = jnp.dot(h, wg_ref[:, j * D_MODEL:(j + 1) * D_MODEL], preferred_element_type=F32)
        (g0_ref, g1o_ref)[j][...] = jax.nn.sigmoid(zg).astype(BF16)

    zf = lax.dot_general(wff_ref[...], h, _NT, preferred_element_type=F32) + bf_ref[...]
    lf_ref[...] = jnp.minimum(zf, 0.0) - jnp.log1p(jnp.exp(-jnp.abs(zf)))


def _inproj(x2, mod3, g1, wqk, wvt, wff_t, wgate, bd, gvecs, bf_col, cos_t, sin_t, tm, seq):
    t, d = x2.shape
    tps = seq // tm
    row = lambda i: (i, 0)
    full = lambda i: (0, 0)
    bsd = lambda n: jax.ShapeDtypeStruct((t, n), BF16)
    bsd_t = jax.ShapeDtypeStruct((512, t), BF16)
    tok = pl.BlockSpec((tm, 512), row)
    tok_t = pl.BlockSpec((512, tm), lambda i: (0, i))
    return pl.pallas_call(
        _inproj_kernel,
        out_shape=(bsd(512), bsd(512), bsd_t, bsd(512), bsd(512), bsd_t, bsd(d), bsd(d),
                   jax.ShapeDtypeStruct((16, t), F32)),
        grid=(t // tm,),
        in_specs=[pl.BlockSpec((tm, d), row),
                  pl.BlockSpec((1, 6, d), lambda i: (i // tps, 0, 0)),
                  pl.BlockSpec((1, d), full),
                  pl.BlockSpec(wqk.shape, full),
                  pl.BlockSpec(wvt.shape, full),
                  pl.BlockSpec(wff_t.shape, full),
                  pl.BlockSpec(wgate.shape, full),
                  pl.BlockSpec(bd.shape, full),
                  pl.BlockSpec(gvecs.shape, full),
                  pl.BlockSpec(bf_col.shape, full),
                  pl.BlockSpec((LANES, tm), lambda i: (0, i)),
                  pl.BlockSpec((LANES, tm), lambda i: (0, i))],
        out_specs=(tok, tok, tok_t, tok, tok, tok_t, pl.BlockSpec((tm, d), row), pl.BlockSpec((tm, d), row),
                   pl.BlockSpec((16, tm), lambda i: (0, i))),
        compiler_params=_cparams("arbitrary"),
        name="inproj",
    )(x2, mod3, g1, wqk, wvt, wff_t, wgate, bd, gvecs, bf_col, cos_t, sin_t)


def _cumsum_kernel(lf_ref, tri_ref, eye_ref, sel_ref, ones_ref, dec_ref, *, seq):
    tri = tri_ref[...]
    eye = eye_ref[...]
    sel = sel_ref[...]
    carry = jnp.zeros((16, 1), F32)
    for blk in range(seq // LANES):
        sl = slice(blk * LANES, (blk + 1) * LANES)
        hi, mid, lo = _split3(lf_ref[:, sl])
        local = (jnp.dot(hi, tri, preferred_element_type=F32)
                 + jnp.dot(mid, tri, preferred_element_type=F32)
                 + jnp.dot(lo, tri, preferred_element_type=F32))
        c = local + carry
        carry = carry + local[:, LANES - 1:LANES]
        pieces = jnp.concatenate(list(_split3(c * LOG2E)) + [jnp.zeros((16, LANES), BF16)], axis=0)
        cols = lax.dot_general(eye, pieces, _NT, preferred_element_type=F32).astype(BF16)
        dec_ref[0, sl, :] = (jnp.dot(cols, sel, preferred_element_type=F32) + ones_ref[...]).astype(BF16)


def _cumsum(lf_t, batch, seq):
    tri = (jnp.arange(LANES)[:, None] <= jnp.arange(LANES)[None, :]).astype(BF16)
    eye = jnp.eye(LANES, dtype=BF16)
    j, h = jnp.meshgrid(jnp.arange(3), jnp.arange(N_FOX_HEADS), indexing="ij")
    rows = (16 * j + h).reshape(-1)
    sel = jnp.zeros((4 * 16, LANES), F32)
    sel = sel.at[rows, (8 * h + j).reshape(-1)].set(1.0)
    sel = sel.at[rows, (HEAD_DIM + 8 * h + 3 + j).reshape(-1)].set(-1.0).astype(BF16)
    ones = jnp.zeros((1, LANES), F32)
    ones = ones.at[0, (8 * h + 3 + j).reshape(-1)].set(1.0).at[0, (HEAD_DIM + 8 * h + j).reshape(-1)].set(1.0)
    return pl.pallas_call(
        functools.partial(_cumsum_kernel, seq=seq),
        out_shape=jax.ShapeDtypeStruct((batch, seq, LANES), BF16),
        grid=(batch,),
        in_specs=[pl.BlockSpec((16, seq), lambda b: (0, b)),
                  pl.BlockSpec((LANES, LANES), lambda b: (0, 0)),
                  pl.BlockSpec((LANES, LANES), lambda b: (0, 0)),
                  pl.BlockSpec(sel.shape, lambda b: (0, 0)),
                  pl.BlockSpec((1, LANES), lambda b: (0, 0))],
        out_specs=pl.BlockSpec((1, seq, LANES), lambda b: (b, 0, 0)),
        compiler_params=_cparams("arbitrary"),
        name="cumsum",
    )(lf_t, tri, eye, sel, ones)


def _attend(n_chains, nq, q_of, k_of, vt_of, diag_mask, finalize, s_a, s_b, m_ref, acc_ref):
    n_steps = nq * (nq + 1) // 2

    def produce(qi, ki, s_ref):
        for c in range(n_chains):
            s_ref[c] = lax.dot_general(k_of(c, ki), q_of(c, qi), _NT, preferred_element_type=F32)

    def consume(qi, ki, s_ref, masked):
        par = lax.bitwise_and(qi, 1)
        for c in range(n_chains):
            s = s_ref[c]
            if masked:
                s = jnp.where(diag_mask, s, NEG_INF)
            m = jnp.where(ki == 0, NEG_INF, m_ref[par, c])
            m_new = jnp.maximum(m, jnp.max(s, axis=0, keepdims=True))
            p = jnp.exp2(s - m_new).astype(BF16)
            acc_ref[par, c] = (jnp.exp2(m - m_new) * acc_ref[par, c]
                               + jnp.dot(vt_of(c, ki), p, preferred_element_type=F32))
            m_ref[par, c] = m_new

    def advance(qi, ki):
        last = (ki == qi).astype(I32)
        return qi + last, (ki + 1) * (1 - last)

    def two_steps(qi, ki, has_next):
        q1, k1 = advance(qi, ki)
        q2, k2 = advance(q1, k1)
        d0 = ki == qi
        d1 = k1 == q1

        def block(mask0, mask1):
            produce(q1, k1, s_b)
            consume(qi, ki, s_a, mask0)
            if has_next:
                produce(q2, k2, s_a)
            consume(q1, k1, s_b, mask1)

        pl.when(d0)(lambda: block(True, False))
        pl.when(d1)(lambda: block(False, True))
        pl.when(jnp.logical_not(jnp.logical_or(d0, d1)))(lambda: block(False, False))
        pl.when(d0)(lambda: finalize(qi))
        pl.when(d1)(lambda: finalize(q1))
        return q2, k2

    m_ref[...] = jnp.full(m_ref.shape, NEG_INF, F32)
    acc_ref[...] = jnp.zeros(acc_ref.shape, F32)
    produce(0, 0, s_a)
    n_pairs = (n_steps - 1) // 2
    qi, ki = lax.fori_loop(0, n_pairs, lambda _, carry: two_steps(*carry, True),
                           (jnp.int32(0), jnp.int32(0)))
    if n_steps - 2 * n_pairs == 2:
        two_steps(qi, ki, False)
    else:
        consume(qi, ki, s_a, True)
        finalize(qi)


def _attn_scratch(tq, rows):
    return [pltpu.VMEM((2, tq, tq), F32), pltpu.VMEM((2, tq, tq), F32),
            pltpu.VMEM((2, 2, 1, tq), F32), pltpu.VMEM((2, 2, rows, tq), F32)]


def _fox_kernel(q_ref, k_ref, vt_ref, dec_ref, o_ref, qaug_ref, kaug_ref, vtaug_ref, s_a, s_b,
                m_ref, acc_ref, *, tq):
    hp = pl.program_id(1)
    seq = q_ref.shape[1]
    nq = seq // tq
    qdec = dec_ref[0, :, :HEAD_DIM]
    kdec = dec_ref[0, :, HEAD_DIM:].astype(F32)
    lane_head = lax.shift_right_logical(lax.broadcasted_iota(I32, kdec.shape, 1), 3)
    ones = jnp.ones((ONES_ROWS, tq), BF16)
    for hh in range(2):
        hs = slice(hh * HEAD_DIM, (hh + 1) * HEAD_DIM)
        kd = jnp.where(lane_head == 2 * hp + hh, kdec, 0.0).astype(BF16)
        kaug_ref[hh] = jnp.concatenate([k_ref[0, :, hs], kd], axis=-1)
        qaug_ref[hh] = jnp.concatenate([q_ref[0, :, hs], qdec], axis=-1)
        for j in range(nq):
            vtaug_ref[hh, j] = jnp.concatenate([vt_ref[hs, j * tq:(j + 1) * tq], ones], axis=0)

    key = lax.broadcasted_iota(I32, (tq, tq), 0)
    qry = lax.broadcasted_iota(I32, (tq, tq), 1)

    def tile(ref, c, i):
        return ref[c, pl.ds(pl.multiple_of(i * tq, tq), tq), :]

    def finalize(qi):
        outs = []
        for c in range(2):
            acc = acc_ref[lax.bitwise_and(qi, 1), c]
            outs.append(acc[:HEAD_DIM] / acc[HEAD_DIM:HEAD_DIM + 1])
        o_t = jnp.concatenate(outs, axis=0)
        o_ref[0, pl.ds(pl.multiple_of(qi * tq, tq), tq), :] = o_t.T.astype(BF16)

    _attend(2, nq, functools.partial(tile, qaug_ref), functools.partial(tile, kaug_ref),
            lambda c, ki: vtaug_ref[c, ki], key <= qry, finalize, s_a, s_b, m_ref, acc_ref)


def _fox_attention(fq, fk, fv_t, dec, tq):
    b, s, _ = fq.shape
    head_pair = pl.BlockSpec((1, s, LANES), lambda bi, hp: (bi, 0, hp))
    return pl.pallas_call(
        functools.partial(_fox_kernel, tq=tq),
        out_shape=jax.ShapeDtypeStruct((b, s, FOX_WIDTH), BF16),
        grid=(b, N_FOX_HEADS // 2),
        in_specs=[head_pair, head_pair,
                  pl.BlockSpec((LANES, s), lambda bi, hp: (hp, bi)),
                  pl.BlockSpec((1, s, LANES), lambda bi, hp: (bi, 0, 0))],
        out_specs=head_pair,
        scratch_shapes=[pltpu.VMEM((2, s, 2 * HEAD_DIM), BF16), pltpu.VMEM((2, s, 2 * HEAD_DIM), BF16),
                        pltpu.VMEM((2, s // tq, HEAD_DIM + ONES_ROWS, tq), BF16)]
        + _attn_scratch(tq, HEAD_DIM + ONES_ROWS),
        compiler_params=_cparams("arbitrary", "arbitrary"),
        name="fox",
    )(fq, fk, fv_t, dec)


def _diff_kernel(q_ref, k_ref, vt_ref, lam_ref, gs_ref, o_ref, vtaug_ref, s_a, s_b, m_ref, acc_ref, *, tq):
    dv = 2 * HEAD_DIM
    seq = q_ref.shape[1]
    nq = seq // tq
    ones = jnp.ones((ONES_ROWS, tq), BF16)
    for j in range(nq):
        vtaug_ref[j] = jnp.concatenate([vt_ref[:, j * tq:(j + 1) * tq], ones], axis=0)

    key = lax.broadcasted_iota(I32, (tq, tq), 0)
    qry = lax.broadcasted_iota(I32, (tq, tq), 1)
    chunk_causal = lax.shift_right_logical(key, CHUNK_SHIFT) <= lax.shift_right_logical(qry, CHUNK_SHIFT)
    lv = lam_ref[...]
    lam = (jnp.exp(jnp.sum(lv[0:1] * lv[1:2], axis=-1, keepdims=True))
           - jnp.exp(jnp.sum(lv[2:3] * lv[3:4], axis=-1, keepdims=True)) + LAM0)

    def rows(i):
        return pl.ds(pl.multiple_of(i * tq, tq), tq)

    def q_of(c, qi):
        return q_ref[0, rows(qi), c * HEAD_DIM:(c + 1) * HEAD_DIM]

    def k_of(c, ki):
        return k_ref[0, rows(ki), c * HEAD_DIM:(c + 1) * HEAD_DIM]

    def finalize(qi):
        par = lax.bitwise_and(qi, 1)
        outs = [acc_ref[par, c][:dv] / acc_ref[par, c][dv:dv + 1] for c in range(2)]
        o_t = outs[0] - lam * outs[1]
        inv = lax.rsqrt(jnp.mean(o_t * o_t, axis=0, keepdims=True) + EPS)
        o_t = o_t * inv * gs_ref[...] * (1.0 - LAM0)
        o_ref[0, rows(qi), :] = o_t.T.astype(BF16)

    _attend(2, nq, q_of, k_of, lambda c, ki: vtaug_ref[ki], chunk_causal, finalize,
            s_a, s_b, m_ref, acc_ref)


def _diff_attention(dq, dk, dv_t, lam_vecs, g_subln_col, tq):
    b, s, _ = dq.shape
    head = pl.BlockSpec((1, s, LANES), lambda bi, h: (bi, 0, h))
    rows = 2 * HEAD_DIM + ONES_ROWS
    return pl.pallas_call(
        functools.partial(_diff_kernel, tq=tq),
        out_shape=jax.ShapeDtypeStruct((b, s, DIFF_WIDTH), BF16),
        grid=(b, N_DIFF_HEADS),
        in_specs=[head, head,
                  pl.BlockSpec((LANES, s), lambda bi, h: (h, bi)),
                  pl.BlockSpec((4, HEAD_DIM), lambda bi, h: (0, 0)),
                  pl.BlockSpec((2 * HEAD_DIM, 1), lambda bi, h: (0, 0))],
        out_specs=head,
        scratch_shapes=[pltpu.VMEM((s // tq, rows, tq), BF16)] + _attn_scratch(tq, rows),
        compiler_params=_cparams("arbitrary", "arbitrary"),
        name="diff",
    )(dq, dk, dv_t, lam_vecs, g_subln_col)


def _merge_kernel(of_ref, od_ref, g0_ref, g1_ref, x_ref, mod_ref, g2_ref, wpf_ref, wpd_ref, wout_ref,
                  wrh_ref, wrl_ref, br_ref,
                  x1_ref, hp_ref, lg_ref):
    gt1 = mod_ref[0, 2:3, :]
    sh2 = mod_ref[0, 3:4, :]
    sc2 = mod_ref[0, 4:5, :]
    sub = x_ref.shape[0] // ROW_CHAINS
    for r in range(ROW_CHAINS):
        rows = pl.ds(r * sub, sub)
        a = jnp.dot(of_ref[rows, :], wpf_ref[...], preferred_element_type=F32)
        b = jnp.dot(od_ref[rows, :], wpd_ref[...], preferred_element_type=F32)
        merged = g0_ref[rows, :].astype(F32) * a + g1_ref[rows, :].astype(F32) * b
        x1 = x_ref[rows, :] + gt1 * jnp.dot(merged.astype(BF16), wout_ref[...], preferred_element_type=F32)
        x1_ref[rows, :] = x1
        h2 = _rms_rows(x1, g2_ref[...]) * (1.0 + sc2) + sh2

        hi = h2.astype(BF16)
        hf = hi.astype(F32)
        half = D_MODEL // 2
        lo_bits = pltpu.bitcast(hf[:, :half], U32) >> 16
        hi_bits = pltpu.bitcast(hf[:, half:], U32) & jnp.uint32(0xFFFF0000)
        hp_ref[rows, :] = lo_bits | hi_bits

        lo = (h2 - hf).astype(BF16)
        wrh = wrh_ref[...]
        lg_ref[rows, :] = (jnp.dot(hi, wrh, preferred_element_type=F32)
                           + jnp.dot(lo, wrh, preferred_element_type=F32)
                           + jnp.dot(hi, wrl_ref[...], preferred_element_type=F32)) + br_ref[...]


def _route_kernel(lg_ref, ids_ref, wts_ref, cnt_ref):
    i = pl.program_id(0)
    lg = lg_ref[:, :LANES]
    le = lg_ref[:, LANES:]
    lane = lax.broadcasted_iota(I32, lg.shape, 1)
    big = jnp.int32(1 << 20)

    def softmax_masked(z, mask):
        zm = jnp.where(mask, z, -jnp.inf)
        e = jnp.exp(zm - jnp.max(zm, axis=-1, keepdims=True))
        return e / jnp.sum(e, axis=-1, keepdims=True)

    def top1(p, mask):
        pm = jnp.where(mask, p, -1.0)
        best = jnp.max(pm, axis=-1, keepdims=True)
        idx = jnp.min(jnp.where(pm == best, lane, big), axis=-1, keepdims=True)
        return best, idx

    gmask = lane < N_GROUPS
    g_w, g_idx = top1(softmax_masked(lg, gmask), gmask)
    emask = (lane >> 3) == g_idx
    p_exp = softmax_masked(le, emask)
    p1, i1 = top1(p_exp, emask)
    p2, i2 = top1(p_exp, emask & (lane != i1))
    denom = p1 + p2
    w1 = g_w * (p1 / denom)
    w2 = g_w * (p2 / denom)
    ids_ref[...] = jnp.where(lane == 0, i1, i2)[:, :TOP_K]
    wts_ref[...] = jnp.where(lane == 0, w1, w2)[:, :TOP_K]

    @pl.when(i == 0)
    def _():
        cnt_ref[...] = jnp.zeros_like(cnt_ref)

    cnt_ref[...] += jnp.sum(((lane == i1) | (lane == i2)).astype(F32), axis=0, keepdims=True)


def _route(logits, tr):
    t = logits.shape[0]
    return pl.pallas_call(
        _route_kernel,
        out_shape=(jax.ShapeDtypeStruct((t, TOP_K), I32),
                   jax.ShapeDtypeStruct((t, TOP_K), F32),
                   jax.ShapeDtypeStruct((1, LANES), F32)),
        grid=(t // tr,),
        in_specs=[pl.BlockSpec((tr, 2 * LANES), lambda i: (i, 0))],
        out_specs=(pl.BlockSpec((tr, TOP_K), lambda i: (i, 0)),
                   pl.BlockSpec((tr, TOP_K), lambda i: (i, 0)),
                   pl.BlockSpec((1, LANES), lambda i: (0, 0))),
        compiler_params=_cparams("arbitrary"),
        name="route",
    )(logits)


def _merge(o_f, o_d, g0, g1, x2, mod3, g2, wpf, wpd, wout, wr_hi, wr_lo, br, tm, seq):
    t, d = x2.shape
    tps = seq // tm
    row = lambda i: (i, 0)
    full = lambda i: (0, 0)
    return pl.pallas_call(
        _merge_kernel,
        out_shape=(jax.ShapeDtypeStruct((t, d), F32),
                   jax.ShapeDtypeStruct((t, d // 2), U32),
                   jax.ShapeDtypeStruct((t, 2 * LANES), F32)),
        grid=(t // tm,),
        in_specs=[pl.BlockSpec((tm, 512), row), pl.BlockSpec((tm, 512), row),
                  pl.BlockSpec((tm, d), row), pl.BlockSpec((tm, d), row),
                  pl.BlockSpec((tm, d), row),
                  pl.BlockSpec((1, 6, d), lambda i: (i // tps, 0, 0)),
                  pl.BlockSpec((1, d), full),
                  pl.BlockSpec(wpf.shape, full), pl.BlockSpec(wpd.shape, full),
                  pl.BlockSpec(wout.shape, full),
                  pl.BlockSpec(wr_hi.shape, full), pl.BlockSpec(wr_lo.shape, full),
                  pl.BlockSpec(br.shape, full)],
        out_specs=(pl.BlockSpec((tm, d), row), pl.BlockSpec((tm, d // 2), row),
                   pl.BlockSpec((tm, 2 * LANES), row)),
        compiler_params=_cparams("arbitrary"),
        name="merge",
    )(o_f, o_d, g0, g1, x2, mod3, g2, wpf, wpd, wout, wr_hi, wr_lo, br)


def _rank_kernel(ids_ref, pstart_ref, ltri_ref, dest_ref, carry_ref):
    i = pl.program_id(0)

    @pl.when(i == 0)
    def _():
        carry_ref[...] = jnp.zeros_like(carry_ref)

    ids = ids_ref[...]
    lane = lax.broadcasted_iota(I32, (ids.shape[0], LANES), 1)
    oh0 = lane == ids[:, 0:1]
    oh1 = lane == ids[:, 1:2]
    both = (oh0 | oh1).astype(BF16)
    before = jnp.dot(ltri_ref[...], both, preferred_element_type=F32)
    base = before + carry_ref[...] + pstart_ref[...]
    d0 = jnp.sum(jnp.where(oh0, base, 0.0), axis=-1, keepdims=True)
    d1 = jnp.sum(jnp.where(oh1, base, 0.0), axis=-1, keepdims=True)
    dest_ref[...] = jnp.where(lane == 0, d0, d1)[:, :TOP_K].astype(I32)
    carry_ref[...] += jnp.sum(both.astype(F32), axis=0, keepdims=True)


def _rank(ids, pad_start_row, tt):
    t = ids.shape[0]
    ltri = (jnp.arange(tt)[:, None] > jnp.arange(tt)[None, :]).astype(BF16)
    return pl.pallas_call(
        _rank_kernel,
        out_shape=jax.ShapeDtypeStruct((t, TOP_K), I32),
        grid=(t // tt,),
        in_specs=[pl.BlockSpec((tt, TOP_K), lambda i: (i, 0)),
                  pl.BlockSpec((1, LANES), lambda i: (0, 0)),
                  pl.BlockSpec((tt, tt), lambda i: (0, 0))],
        out_specs=pl.BlockSpec((tt, TOP_K), lambda i: (i, 0)),
        scratch_shapes=[pltpu.VMEM((1, LANES), F32)],
        compiler_params=_cparams("arbitrary"),
        name="rank",
    )(ids, pad_start_row, ltri)


def _row_copy(src_ref, s, dst_ref, d, sem):
    return pltpu.make_async_copy(src_ref.at[pl.ds(s, 1)], dst_ref.at[pl.ds(d, 1)], sem)


def _dispatch_kernel(pend_ref, padded_ref, nused_ref, dest_ref, h_ref, xs_ref, zero_ref, sem, zsem,
                     *, td, n_blocks):
    i = pl.program_id(0)

    @pl.when(i == 0)
    def _():
        zero_ref[...] = jnp.zeros_like(zero_ref)

        def zero_block(start):
            cp = pltpu.make_async_copy(
                zero_ref, xs_ref.at[pl.ds(pl.multiple_of(start, MOE_BLOCK), MOE_BLOCK)], zsem)
            cp.start()
            cp.wait()

        for e in range(N_EXPERTS):
            @pl.when(padded_ref[e] > 0)
            def _(e=e):
                zero_block(pend_ref[e] - MOE_BLOCK)

            @pl.when(nused_ref[0] + e < n_blocks)
            def _(e=e):
                zero_block((nused_ref[0] + e) * MOE_BLOCK)

    def issue(t, _):
        _row_copy(h_ref, t, xs_ref, dest_ref[2 * t], sem).start(priority=0)
        _row_copy(h_ref, t, xs_ref, dest_ref[2 * t + 1], sem).start(priority=1)
        return 0

    lax.fori_loop(0, td, issue, 0, unroll=DMA_UNROLL)

    def drain(t, _):
        _row_copy(h_ref, 0, xs_ref, 0, sem).wait()
        _row_copy(h_ref, 0, xs_ref, 0, sem).wait()
        return 0

    lax.fori_loop(0, td, drain, 0, unroll=DMA_UNROLL)


def _dispatch(pad_end, padded, n_used, dest_flat, h_packed, p_rows, td):
    t, w = h_packed.shape
    grid_spec = pltpu.PrefetchScalarGridSpec(
        num_scalar_prefetch=3,
        grid=(t // td,),
        in_specs=[pl.BlockSpec((TOP_K * td,), lambda i, *_: (i,), memory_space=pltpu.SMEM),
                  pl.BlockSpec((td, w), lambda i, *_: (i, 0))],
        out_specs=pl.BlockSpec(memory_space=pl.ANY),
        scratch_shapes=[pltpu.VMEM((MOE_BLOCK, w), U32),
                        pltpu.SemaphoreType.DMA(()), pltpu.SemaphoreType.DMA(())],
    )
    return pl.pallas_call(
        functools.partial(_dispatch_kernel, td=td, n_blocks=p_rows // MOE_BLOCK),
        out_shape=jax.ShapeDtypeStruct((p_rows, w), U32),
        grid_spec=grid_spec,
        compiler_params=_cparams("arbitrary"),
        name="dispatch",
    )(pad_end, padded, n_used, dest_flat, h_packed)


def _expert_kernel(be_ref, nused_ref, xs_ref, w1_ref, w3_ref, w2_ref, y_ref, w1b_ref, w3b_ref, w2b_ref):
    p = pl.program_id(0)
    active = p < nused_ref[0]
    new_expert = jnp.logical_or(p == 0, be_ref[p] != be_ref[jnp.maximum(p - 1, 0)])

    @pl.when(jnp.logical_and(active, new_expert))
    def _():
        w1b_ref[...] = w1_ref[0].astype(BF16)
        w3b_ref[...] = w3_ref[0].astype(BF16)
        w2b_ref[...] = w2_ref[0].astype(BF16)

    @pl.when(active)
    def _():
        words = xs_ref[...]
        half = D_MODEL // 2
        xa = pltpu.bitcast(words << 16, F32).astype(BF16)
        xb = pltpu.bitcast(words & jnp.uint32(0xFFFF0000), F32).astype(BF16)

        def up(w_ref):
            return (jnp.dot(xa, w_ref[:half, :], preferred_element_type=F32)
                    + jnp.dot(xb, w_ref[half:, :], preferred_element_type=F32))

        g = up(w1b_ref)
        u = up(w3b_ref)
        act = (g * jax.nn.sigmoid(g) * u).astype(BF16)
        y_ref[...] = jnp.dot(act, w2b_ref[...], preferred_element_type=F32)

    @pl.when(p >= nused_ref[0])
    def _():
        y_ref[...] = jnp.zeros_like(y_ref)


def _experts(block_expert, n_used, xs, w1, w3, w2):
    p_rows, w = xs.shape
    nb = p_rows // MOE_BLOCK
    d = w1.shape[1]
    blk = lambda p, be, nu: (p, 0)
    wsel = lambda p, be, nu: (be[jnp.minimum(p, nu[0] - 1)], 0, 0)
    grid_spec = pltpu.PrefetchScalarGridSpec(
        num_scalar_prefetch=2,
        grid=(nb,),
        in_specs=[pl.BlockSpec((MOE_BLOCK, w), blk),
                  pl.BlockSpec((1, d, D_EXPERT), wsel),
                  pl.BlockSpec((1, d, D_EXPERT), wsel),
                  pl.BlockSpec((1, D_EXPERT, d), wsel)],
        out_specs=pl.BlockSpec((MOE_BLOCK, d), blk),
        scratch_shapes=[pltpu.VMEM((d, D_EXPERT), BF16), pltpu.VMEM((d, D_EXPERT), BF16),
                        pltpu.VMEM((D_EXPERT, d), BF16)],
    )
    return pl.pallas_call(
        _expert_kernel,
        out_shape=jax.ShapeDtypeStruct((p_rows, d), F32),
        grid_spec=grid_spec,
        compiler_params=_cparams("arbitrary"),
        name="experts",
    )(block_expert, n_used, xs, w1, w3, w2)


def _combine_kernel(dest_ref, wts_ref, x1_ref, mod_ref, yb_ref, o_ref, ybuf_ref, sem, *, tc):
    def issue(t, _):
        _row_copy(yb_ref, dest_ref[2 * t], ybuf_ref.at[0], t, sem).start(priority=0)
        _row_copy(yb_ref, dest_ref[2 * t + 1], ybuf_ref.at[1], t, sem).start(priority=1)
        return 0

    lax.fori_loop(0, tc, issue, 0, unroll=DMA_UNROLL)

    def drain(t, _):
        _row_copy(yb_ref, 0, ybuf_ref.at[0], 0, sem).wait()
        _row_copy(yb_ref, 0, ybuf_ref.at[1], 0, sem).wait()
        return 0

    lax.fori_loop(0, tc, drain, 0, unroll=DMA_UNROLL)
    gt2 = mod_ref[0, 5:6, :]
    wts = wts_ref[...]
    y = ybuf_ref[0] * wts[:, 0:1] + ybuf_ref[1] * wts[:, 1:2]
    o_ref[...] = x1_ref[...] + gt2 * y


def _combine(dest_flat, wts, x1, mod3, yb, tc, seq):
    t, d = x1.shape
    tps = seq // tc
    return pl.pallas_call(
        functools.partial(_combine_kernel, tc=tc),
        out_shape=jax.ShapeDtypeStruct((t, d), F32),
        grid=(t // tc,),
        in_specs=[pl.BlockSpec((TOP_K * tc,), lambda i: (i,), memory_space=pltpu.SMEM),
                  pl.BlockSpec((tc, TOP_K), lambda i: (i, 0)),
                  pl.BlockSpec((tc, d), lambda i: (i, 0)),
                  pl.BlockSpec((1, 6, d), lambda i: (i // tps, 0, 0)),
                  pl.BlockSpec(memory_space=pl.ANY)],
        out_specs=pl.BlockSpec((tc, d), lambda i: (i, 0)),
        scratch_shapes=[pltpu.VMEM((2, tc, d), F32), pltpu.SemaphoreType.DMA(())],
        compiler_params=_cparams("arbitrary"),
        name="combine",
    )(dest_flat, wts, x1, mod3, yb)


def _tile(n, pref):
    t = pref
    while n % t:
        t //= 2
    return t


def kernel(x, c, positions, w_ada, b_ada, g_norm1, w_in, b_f, g_q_fox, g_k_fox, g_q_diff, g_k_diff,
           lam_q1, lam_k1, lam_q2, lam_k2, g_subln, w_proj_fox, w_proj_diff, w_out, g_norm2,
           w_router_group, b_router_group, w_router_expert, b_router_expert, w1, w3, w2):
    b, s, d = x.shape
    t = b * s
    assert d == D_MODEL and w_ada.shape[0] == 1 and s % LANES == 0
    tm = _tile(s, 512)
    tq = _tile(s, 512)

    wi = w_in[0]
    c_fk, c_fv, c_ff = FOX_WIDTH, 2 * FOX_WIDTH, 3 * FOX_WIDTH
    c_dq = c_ff + N_FOX_HEADS
    c_gate = c_dq + 3 * 512
    c_dv = c_dq + 2 * 512
    wqk = jnp.concatenate([wi[:, :c_fv], wi[:, c_dq:c_dv]], axis=1).astype(BF16)
    wvt = jnp.concatenate([wi[:, c_fv:c_ff], wi[:, c_dv:c_gate]], axis=1).T.astype(BF16)
    wff_t = jnp.zeros((16, d), BF16).at[:N_FOX_HEADS].set(wi[:, c_ff:c_dq].T.astype(BF16))
    wgate = wi[:, c_gate:].astype(BF16)
    bf_col = jnp.zeros((16, 1), F32).at[:N_FOX_HEADS, 0].set(b_f[0])
    head_id = jnp.arange(MXU_TILE) // HEAD_DIM
    bd = (head_id[:, None] == head_id[None, :]).astype(BF16)
    gvecs = jnp.stack([jnp.tile(g[0], FOX_WIDTH // HEAD_DIM)
                       for g in (g_q_fox, g_k_fox, g_q_diff, g_k_diff)])
    lam_vecs = jnp.stack([lam_q1[0], lam_k1[0], lam_q2[0], lam_k2[0]])
    inv_freq = ROPE_THETA ** (-jnp.arange(0, HEAD_DIM, 2, dtype=F32) / HEAD_DIM)
    invf_col = inv_freq.reshape(HEAD_DIM // 2, 1)
    wr = jnp.zeros((d, 2 * LANES), F32)
    wr = wr.at[:, :N_GROUPS].set(w_router_group[0]).at[:, LANES:LANES + N_EXPERTS].set(w_router_expert[0])
    wr_hi = wr.astype(BF16)
    wr_lo = (wr - wr_hi.astype(F32)).astype(BF16)
    br = jnp.zeros((1, 2 * LANES), F32)
    br = br.at[0, :N_GROUPS].set(b_router_group[0]).at[0, LANES:LANES + N_EXPERTS].set(b_router_expert[0])

    x2 = x.reshape(t, d)
    mod3 = _ada(c, w_ada[0], b_ada[0]).reshape(b, 6, d)
    cos_t, sin_t = _rope_tables(positions.reshape(1, t), invf_col, _tile(t, 2048))

    fq, fk, fv_t, dq, dk, dv_t, g0, g1, lf_t = _inproj(
        x2, mod3, g_norm1, wqk, wvt, wff_t, wgate, bd, gvecs, bf_col, cos_t, sin_t, tm, s)
    dec = _cumsum(lf_t, b, s)
    r3 = lambda a: a.reshape(b, s, a.shape[-1])
    o_f = _fox_attention(r3(fq), r3(fk), fv_t, dec, tq)
    o_d = _diff_attention(r3(dq), r3(dk), dv_t, lam_vecs, g_subln.reshape(2 * HEAD_DIM, 1), tq)

    x1, h_packed, logits = _merge(
        o_f.reshape(t, FOX_WIDTH), o_d.reshape(t, DIFF_WIDTH), g0, g1, x2, mod3, g_norm2,
        w_proj_fox[0].astype(BF16), w_proj_diff[0].astype(BF16), w_out[0].astype(BF16),
        wr_hi, wr_lo, br, tm, s)
    ids, wts, counts = _route(logits, _tile(t, 2048))

    a = t * TOP_K
    n_blocks = -(-a // MOE_BLOCK) + N_EXPERTS
    p_rows = n_blocks * MOE_BLOCK
    cnt = counts[0, :N_EXPERTS].astype(I32)
    padded = ((cnt + MOE_BLOCK - 1) // MOE_BLOCK) * MOE_BLOCK
    pad_end = jnp.cumsum(padded).astype(I32)
    pad_start = pad_end - padded
    block_start = jnp.arange(n_blocks, dtype=I32) * MOE_BLOCK
    block_expert = jnp.minimum(
        jnp.sum((pad_end[None, :] <= block_start[:, None]).astype(I32), axis=1), N_EXPERTS - 1)
    n_used = (pad_end[-1:] // MOE_BLOCK).astype(I32)
    pstart_row = jnp.zeros((1, LANES), F32).at[0, :N_EXPERTS].set(pad_start.astype(F32))

    dest = _rank(ids, pstart_row, _tile(t, 1024)).reshape(a)
    xs = _dispatch(pad_end, padded, n_used, dest, h_packed, p_rows, _tile(t, 512))
    yb = _experts(block_expert, n_used, xs, w1[0], w3[0], w2[0])
    out = _combine(dest, wts, x1, mod3, yb, _tile(s, 256), s)
    return out.reshape(b, s, d)
```

```python
import functools
import math

import jax
import jax.numpy as jnp
from jax import lax
from jax.experimental import pallas as pl
from jax.experimental.pallas import tpu as pltpu

F32 = jnp.float32
BF16 = jnp.bfloat16
I32 = jnp.int32
U32 = jnp.uint32

D_MODEL = 1024
HEAD_DIM = 64
N_FOX_HEADS = 8
N_DIFF_HEADS = 4
FOX_WIDTH = 512
DIFF_WIDTH = 512
CHUNK = 64
CHUNK_SHIFT = CHUNK.bit_length() - 1
ROPE_THETA = 10000.0
N_GROUPS = 4
EXPERTS_PER_GROUP = 8
N_EXPERTS = 32
TOP_K = 2
D_EXPERT = 512
MOE_BLOCK = 512
EPS = 1e-6
NEG_INF = -1e30
LAM0 = 0.8 - 0.6 * math.exp(-0.3 * 0)
LOG2E = math.log2(math.e)
Q_SCALE = HEAD_DIM ** -0.5 * LOG2E

LANES = 128
MXU_TILE = 256
VMEM_LIMIT = 56 * 1024 * 1024

DMA_UNROLL = 8
ONES_ROWS = 16
ROW_CHAINS = 2

_NT = (((1,), (1,)), ((), ()))


def _cparams(*sem):
    return pltpu.CompilerParams(dimension_semantics=sem, vmem_limit_bytes=VMEM_LIMIT)


def _split3(x):
    hi = x.astype(BF16)
    r1 = x - hi.astype(F32)
    mid = r1.astype(BF16)
    lo = (r1 - mid.astype(F32)).astype(BF16)
    return hi, mid, lo


def _ada_kernel(c_ref, w_ref, b_ref, o_ref):
    c = c_ref[...]
    ca = (c * jax.nn.sigmoid(c)).astype(BF16)
    o_ref[...] = jnp.dot(ca, w_ref[...].astype(BF16), preferred_element_type=F32) + b_ref[...]


def _ada(c, w_ada, b_ada):
    b, d = c.shape
    n = w_ada.shape[1]
    tn = 1024
    return pl.pallas_call(
        _ada_kernel,
        out_shape=jax.ShapeDtypeStruct((b, n), F32),
        grid=(n // tn,),
        in_specs=[pl.BlockSpec((b, d), lambda j: (0, 0)),
                  pl.BlockSpec((d, tn), lambda j: (0, j)),
                  pl.BlockSpec((1, tn), lambda j: (0, j))],
        out_specs=pl.BlockSpec((b, tn), lambda j: (0, j)),
        compiler_params=_cparams("arbitrary"),
        name="ada",
    )(c, w_ada, b_ada.reshape(1, n))


def _rope_kernel(pos_ref, invf_ref, cos_ref, sin_ref):
    ang = invf_ref[...] * pos_ref[...].astype(F32)
    c = jnp.cos(ang)
    s = jnp.sin(ang)
    cos_ref[...] = jnp.concatenate([c, c, c, c], axis=0)
    sin_ref[...] = jnp.concatenate([-s, s, -s, s], axis=0)


def _rope_tables(pos_row, invf_col, tm):
    t = pos_row.shape[1]
    tbl = jax.ShapeDtypeStruct((LANES, t), F32)
    return pl.pallas_call(
        _rope_kernel,
        out_shape=(tbl, tbl),
        grid=(t // tm,),
        in_specs=[pl.BlockSpec((1, tm), lambda i: (0, i)),
                  pl.BlockSpec((HEAD_DIM // 2, 1), lambda i: (0, 0))],
        out_specs=(pl.BlockSpec((LANES, tm), lambda i: (0, i)),
                   pl.BlockSpec((LANES, tm), lambda i: (0, i))),
        compiler_params=_cparams("arbitrary"),
        name="rope",
    )(pos_row, invf_col)


def _rms_rows(x, g):
    return x * lax.rsqrt(jnp.mean(x * x, axis=-1, keepdims=True) + EPS) * g


def _head_norm(z, bd, g):
    zz = (z * z).astype(BF16)
    n = bd.shape[0]
    ss = jnp.concatenate([jnp.dot(zz[:, j:j + n], bd, preferred_element_type=F32)
                          for j in range(0, z.shape[1], n)], axis=1)
    return z * lax.rsqrt(ss * (1.0 / HEAD_DIM) + EPS) * g


def _rotate_half(z):
    n = z.shape[-1]
    lane = lax.broadcasted_iota(I32, z.shape, 1)
    fwd = pltpu.roll(z, n - HEAD_DIM // 2, 1)
    bwd = pltpu.roll(z, HEAD_DIM // 2, 1)
    return jnp.where((lane & (HEAD_DIM - 1)) < HEAD_DIM // 2, fwd, bwd)


def _inproj_kernel(x_ref, mod_ref, g1_ref, wqk_ref, wvt_ref, wff_ref, wg_ref, bd_ref, gv_ref, bf_ref,
                   cos_ref, sin_ref,
                   fq_ref, fk_ref, fv_ref, dq_ref, dk_ref, dv_ref, g0_ref, g1o_ref, lf_ref):
    x = x_ref[...]
    sh1 = mod_ref[0, 0:1, :]
    sc1 = mod_ref[0, 1:2, :]
    h = (_rms_rows(x, g1_ref[...]) * (1.0 + sc1) + sh1).astype(BF16)
    bd = bd_ref[...]
    w = FOX_WIDTH

    def proj(j):
        return jnp.dot(h, wqk_ref[:, j * w:(j + 1) * w], preferred_element_type=F32)

    def proj_t(j):
        return lax.dot_general(wvt_ref[j * w:(j + 1) * w, :], h, _NT, preferred_element_type=F32)

    fq_ref[...] = (_head_norm(proj(0), bd, gv_ref[0:1, :]) * Q_SCALE).astype(BF16)
    fk_ref[...] = _head_norm(proj(1), bd, gv_ref[1:2, :]).astype(BF16)
    fv_ref[...] = proj_t(0).astype(BF16)

    cos = jnp.concatenate([cos_ref[...].T] * (w // LANES), axis=1)
    sin = jnp.concatenate([sin_ref[...].T] * (w // LANES), axis=1)
    qn = _head_norm(proj(2), bd, gv_ref[2:3, :])
    dq_ref[...] = ((qn * cos + _rotate_half(qn) * sin) * Q_SCALE).astype(BF16)
    kn = _head_norm(proj(3), bd, gv_ref[3:4, :])
    dk_ref[...] = (kn * cos + _rotate_half(kn) * sin).astype(BF16)
    dv_ref[...] = proj_t(1).astype(BF16)

    for j in range(2):
        zg = jnp.dot(h, wg_ref[:, j * D_MODEL:(j + 1) * D_MODEL], preferred_element_type=F32)
        (g0_ref, g1o_ref)[j][...] = jax.nn.sigmoid(zg).astype(BF16)

    zf = lax.dot_general(wff_ref[...], h, _NT, preferred_element_type=F32) + bf_ref[...]
    lf_ref[...] = jnp.minimum(zf, 0.0) - jnp.log1p(jnp.exp(-jnp.abs(zf)))


def _inproj(x2, mod3, g1, wqk, wvt, wff_t, wgate, bd, gvecs, bf_col, cos_t, sin_t, tm, seq):
    t, d = x2.shape
    tps = seq // tm
    row = lambda i: (i, 0)
    full = lambda i: (0, 0)
    bsd = lambda n: jax.ShapeDtypeStruct((t, n), BF16)
    bsd_t = jax.ShapeDtypeStruct((512, t), BF16)
    tok = pl.BlockSpec((tm, 512), row)
    tok_t = pl.BlockSpec((512, tm), lambda i: (0, i))
    return pl.pallas_call(
        _inproj_kernel,
        out_shape=(bsd(512), bsd(512), bsd_t, bsd(512), bsd(512), bsd_t, bsd(d), bsd(d),
                   jax.ShapeDtypeStruct((16, t), F32)),
        grid=(t // tm,),
        in_specs=[pl.BlockSpec((tm, d), row),
                  pl.BlockSpec((1, 6, d), lambda i: (i // tps, 0, 0)),
                  pl.BlockSpec((1, d), full),
                  pl.BlockSpec(wqk.shape, full),
                  pl.BlockSpec(wvt.shape, full),
                  pl.BlockSpec(wff_t.shape, full),
                  pl.BlockSpec(wgate.shape, full),
                  pl.BlockSpec(bd.shape, full),
                  pl.BlockSpec(gvecs.shape, full),
                  pl.BlockSpec(bf_col.shape, full),
                  pl.BlockSpec((LANES, tm), lambda i: (0, i)),
                  pl.BlockSpec((LANES, tm), lambda i: (0, i))],
        out_specs=(tok, tok, tok_t, tok, tok, tok_t, pl.BlockSpec((tm, d), row), pl.BlockSpec((tm, d), row),
                   pl.BlockSpec((16, tm), lambda i: (0, i))),
        compiler_params=_cparams("arbitrary"),
        name="inproj",
    )(x2, mod3, g1, wqk, wvt, wff_t, wgate, bd, gvecs, bf_col, cos_t, sin_t)


def _cumsum_kernel(lf_ref, tri_ref, eye_ref, sel_ref, ones_ref, dec_ref, *, seq):
    tri = tri_ref[...]
    eye = eye_ref[...]
    sel = sel_ref[...]
    n_blk = seq // LANES
    split = []
    for blk in range(n_blk):
        split += list(_split3(lf_ref[:, blk * LANES:(blk + 1) * LANES]))
    loc = jnp.dot(jnp.concatenate(split, axis=0), tri, preferred_element_type=F32)
    carry = jnp.zeros((16, 1), F32)
    cols = []
    for blk in range(n_blk):
        r = 3 * 16 * blk
        local = loc[r:r + 16] + loc[r + 16:r + 32] + loc[r + 32:r + 48]
        c = local + carry
        carry = carry + local[:, LANES - 1:LANES]
        pieces = jnp.concatenate(list(_split3(c * LOG2E)) + [jnp.zeros((16, LANES), BF16)], axis=0)
        cols.append(lax.dot_general(eye, pieces, _NT, preferred_element_type=F32).astype(BF16))
    dec_ref[0] = (jnp.dot(jnp.concatenate(cols, axis=0), sel, preferred_element_type=F32)
                  + ones_ref[...]).astype(BF16)


def _cumsum(lf_t, batch, seq):
    tri = (jnp.arange(LANES)[:, None] <= jnp.arange(LANES)[None, :]).astype(BF16)
    eye = jnp.eye(LANES, dtype=BF16)
    j, h = jnp.meshgrid(jnp.arange(3), jnp.arange(N_FOX_HEADS), indexing="ij")
    rows = (16 * j + h).reshape(-1)
    sel = jnp.zeros((4 * 16, LANES), F32)
    sel = sel.at[rows, (8 * h + j).reshape(-1)].set(1.0)
    sel = sel.at[rows, (HEAD_DIM + 8 * h + 3 + j).reshape(-1)].set(-1.0).astype(BF16)
    ones = jnp.zeros((1, LANES), F32)
    ones = ones.at[0, (8 * h + 3 + j).reshape(-1)].set(1.0).at[0, (HEAD_DIM + 8 * h + j).reshape(-1)].set(1.0)
    return pl.pallas_call(
        functools.partial(_cumsum_kernel, seq=seq),
        out_shape=jax.ShapeDtypeStruct((batch, seq, LANES), BF16),
        grid=(batch,),
        in_specs=[pl.BlockSpec((16, seq), lambda b: (0, b)),
                  pl.BlockSpec((LANES, LANES), lambda b: (0, 0)),
                  pl.BlockSpec((LANES, LANES), lambda b: (0, 0)),
                  pl.BlockSpec(sel.shape, lambda b: (0, 0)),
                  pl.BlockSpec((1, LANES), lambda b: (0, 0))],
        out_specs=pl.BlockSpec((1, seq, LANES), lambda b: (b, 0, 0)),
        compiler_params=_cparams("arbitrary"),
        name="cumsum",
    )(lf_t, tri, eye, sel, ones)


def _attend(n_chains, nq, q_of, k_of, vt_of, diag_mask, finalize, s_a, s_b, m_ref, acc_ref):
    n_steps = nq * (nq + 1) // 2

    def produce(qi, ki, s_ref):
        for c in range(n_chains):
            s_ref[c] = lax.dot_general(k_of(c, ki), q_of(c, qi), _NT, preferred_element_type=F32)

    def consume(qi, ki, s_ref, masked):
        par = lax.bitwise_and(qi, 1)
        for c in range(n_chains):
            s = s_ref[c]
            if masked:
                s = jnp.where(diag_mask, s, NEG_INF)
            m = jnp.where(ki == 0, NEG_INF, m_ref[par, c])
            m_new = jnp.maximum(m, jnp.max(s, axis=0, keepdims=True))
            p = jnp.exp2(s - m_new).astype(BF16)
            acc_ref[par, c] = (jnp.exp2(m - m_new) * acc_ref[par, c]
                               + jnp.dot(vt_of(c, ki), p, preferred_element_type=F32))
            m_ref[par, c] = m_new

    def advance(qi, ki):
        last = (ki == qi).astype(I32)
        return qi + last, (ki + 1) * (1 - last)

    def two_steps(qi, ki, has_next):
        q1, k1 = advance(qi, ki)
        q2, k2 = advance(q1, k1)
        d0 = ki == qi
        d1 = k1 == q1

        def block(mask0, mask1):
            produce(q1, k1, s_b)
            consume(qi, ki, s_a, mask0)
            if has_next:
                produce(q2, k2, s_a)
            consume(q1, k1, s_b, mask1)

        pl.when(d0)(lambda: block(True, False))
        pl.when(d1)(lambda: block(False, True))
        pl.when(jnp.logical_not(jnp.logical_or(d0, d1)))(lambda: block(False, False))
        pl.when(d0)(lambda: finalize(qi))
        pl.when(d1)(lambda: finalize(q1))
        return q2, k2

    m_ref[...] = jnp.full(m_ref.shape, NEG_INF, F32)
    acc_ref[...] = jnp.zeros(acc_ref.shape, F32)
    produce(0, 0, s_a)
    n_pairs = (n_steps - 1) // 2
    qi, ki = lax.fori_loop(0, n_pairs, lambda _, carry: two_steps(*carry, True),
                           (jnp.int32(0), jnp.int32(0)))
    if n_steps - 2 * n_pairs == 2:
        two_steps(qi, ki, False)
    else:
        consume(qi, ki, s_a, True)
        finalize(qi)


def _attn_scratch(tq, rows):
    return [pltpu.VMEM((2, tq, tq), F32), pltpu.VMEM((2, tq, tq), F32),
            pltpu.VMEM((2, 2, 1, tq), F32), pltpu.VMEM((2, 2, rows, tq), F32)]


def _fox_kernel(q_ref, k_ref, vt_ref, dec_ref, o_ref, qaug_ref, kaug_ref, vtaug_ref, s_a, s_b,
                m_ref, acc_ref, *, tq):
    hp = pl.program_id(1)
    seq = q_ref.shape[1]
    nq = seq // tq
    qdec = dec_ref[0, :, :HEAD_DIM]
    kdec = dec_ref[0, :, HEAD_DIM:].astype(F32)
    lane_head = lax.shift_right_logical(lax.broadcasted_iota(I32, kdec.shape, 1), 3)
    ones = jnp.ones((ONES_ROWS, tq), BF16)
    for hh in range(2):
        hs = slice(hh * HEAD_DIM, (hh + 1) * HEAD_DIM)
        kd = jnp.where(lane_head == 2 * hp + hh, kdec, 0.0).astype(BF16)
        kaug_ref[hh] = jnp.concatenate([k_ref[0, :, hs], kd], axis=-1)
        qaug_ref[hh] = jnp.concatenate([q_ref[0, :, hs], qdec], axis=-1)
        for j in range(nq):
            vtaug_ref[hh, j] = jnp.concatenate([vt_ref[hs, j * tq:(j + 1) * tq], ones], axis=0)

    key = lax.broadcasted_iota(I32, (tq, tq), 0)
    qry = lax.broadcasted_iota(I32, (tq, tq), 1)

    def tile(ref, c, i):
        return ref[c, pl.ds(pl.multiple_of(i * tq, tq), tq), :]

    def finalize(qi):
        outs = []
        for c in range(2):
            acc = acc_ref[lax.bitwise_and(qi, 1), c]
            outs.append(acc[:HEAD_DIM] / acc[HEAD_DIM:HEAD_DIM + 1])
        o_t = jnp.concatenate(outs, axis=0)
        o_ref[0, pl.ds(pl.multiple_of(qi * tq, tq), tq), :] = o_t.T.astype(BF16)

    _attend(2, nq, functools.partial(tile, qaug_ref), functools.partial(tile, kaug_ref),
            lambda c, ki: vtaug_ref[c, ki], key <= qry, finalize, s_a, s_b, m_ref, acc_ref)


def _fox_attention(fq, fk, fv_t, dec, tq):
    b, s, _ = fq.shape
    head_pair = pl.BlockSpec((1, s, LANES), lambda bi, hp: (bi, 0, hp))
    return pl.pallas_call(
        functools.partial(_fox_kernel, tq=tq),
        out_shape=jax.ShapeDtypeStruct((b, s, FOX_WIDTH), BF16),
        grid=(b, N_FOX_HEADS // 2),
        in_specs=[head_pair, head_pair,
                  pl.BlockSpec((LANES, s), lambda bi, hp: (hp, bi)),
                  pl.BlockSpec((1, s, LANES), lambda bi, hp: (bi, 0, 0))],
        out_specs=head_pair,
        scratch_shapes=[pltpu.VMEM((2, s, 2 * HEAD_DIM), BF16), pltpu.VMEM((2, s, 2 * HEAD_DIM), BF16),
                        pltpu.VMEM((2, s // tq, HEAD_DIM + ONES_ROWS, tq), BF16)]
        + _attn_scratch(tq, HEAD_DIM + ONES_ROWS),
        compiler_params=_cparams("arbitrary", "arbitrary"),
        name="fox",
    )(fq, fk, fv_t, dec)


def _diff_kernel(q_ref, k_ref, vt_ref, lam_ref, gs_ref, o_ref, vtaug_ref, s_a, s_b, m_ref, acc_ref, *, tq):
    dv = 2 * HEAD_DIM
    seq = q_ref.shape[1]
    nq = seq // tq
    ones = jnp.ones((ONES_ROWS, tq), BF16)
    for j in range(nq):
        vtaug_ref[j] = jnp.concatenate([vt_ref[:, j * tq:(j + 1) * tq], ones], axis=0)

    key = lax.broadcasted_iota(I32, (tq, tq), 0)
    qry = lax.broadcasted_iota(I32, (tq, tq), 1)
    chunk_causal = lax.shift_right_logical(key, CHUNK_SHIFT) <= lax.shift_right_logical(qry, CHUNK_SHIFT)
    lv = lam_ref[...]
    lam = (jnp.exp(jnp.sum(lv[0:1] * lv[1:2], axis=-1, keepdims=True))
           - jnp.exp(jnp.sum(lv[2:3] * lv[3:4], axis=-1, keepdims=True)) + LAM0)

    def rows(i):
        return pl.ds(pl.multiple_of(i * tq, tq), tq)

    def q_of(c, qi):
        return q_ref[0, rows(qi), c * HEAD_DIM:(c + 1) * HEAD_DIM]

    def k_of(c, ki):
        return k_ref[0, rows(ki), c * HEAD_DIM:(c + 1) * HEAD_DIM]

    def finalize(qi):
        par = lax.bitwise_and(qi, 1)
        outs = [acc_ref[par, c][:dv] / acc_ref[par, c][dv:dv + 1] for c in range(2)]
        o_t = outs[0] - lam * outs[1]
        inv = lax.rsqrt(jnp.mean(o_t * o_t, axis=0, keepdims=True) + EPS)
        o_t = o_t * inv * gs_ref[...] * (1.0 - LAM0)
        o_ref[0, rows(qi), :] = o_t.T.astype(BF16)

    _attend(2, nq, q_of, k_of, lambda c, ki: vtaug_ref[ki], chunk_causal, finalize,
            s_a, s_b, m_ref, acc_ref)


def _diff_attention(dq, dk, dv_t, lam_vecs, g_subln_col, tq):
    b, s, _ = dq.shape
    head = pl.BlockSpec((1, s, LANES), lambda bi, h: (bi, 0, h))
    rows = 2 * HEAD_DIM + ONES_ROWS
    return pl.pallas_call(
        functools.partial(_diff_kernel, tq=tq),
        out_shape=jax.ShapeDtypeStruct((b, s, DIFF_WIDTH), BF16),
        grid=(b, N_DIFF_HEADS),
        in_specs=[head, head,
                  pl.BlockSpec((LANES, s), lambda bi, h: (h, bi)),
                  pl.BlockSpec((4, HEAD_DIM), lambda bi, h: (0, 0)),
                  pl.BlockSpec((2 * HEAD_DIM, 1), lambda bi, h: (0, 0))],
        out_specs=head,
        scratch_shapes=[pltpu.VMEM((s // tq, rows, tq), BF16)] + _attn_scratch(tq, rows),
        compiler_params=_cparams("arbitrary", "arbitrary"),
        name="diff",
    )(dq, dk, dv_t, lam_vecs, g_subln_col)


def _merge_kernel(of_ref, od_ref, g0_ref, g1_ref, x_ref, mod_ref, g2_ref, wpf_ref, wpd_ref, wout_ref,
                  wrh_ref, wrl_ref, br_ref,
                  x1_ref, hp_ref, lg_ref):
    gt1 = mod_ref[0, 2:3, :]
    sh2 = mod_ref[0, 3:4, :]
    sc2 = mod_ref[0, 4:5, :]
    sub = x_ref.shape[0] // ROW_CHAINS
    for r in range(ROW_CHAINS):
        rows = pl.ds(r * sub, sub)
        a = jnp.dot(of_ref[rows, :], wpf_ref[...], preferred_element_type=F32)
        b = jnp.dot(od_ref[rows, :], wpd_ref[...], preferred_element_type=F32)
        merged = g0_ref[rows, :].astype(F32) * a + g1_ref[rows, :].astype(F32) * b
        x1 = x_ref[rows, :] + gt1 * jnp.dot(merged.astype(BF16), wout_ref[...], preferred_element_type=F32)
        x1_ref[rows, :] = x1
        h2 = _rms_rows(x1, g2_ref[...]) * (1.0 + sc2) + sh2

        hi = h2.astype(BF16)
        hf = hi.astype(F32)
        half = D_MODEL // 2
        lo_bits = pltpu.bitcast(hf[:, :half], U32) >> 16
        hi_bits = pltpu.bitcast(hf[:, half:], U32) & jnp.uint32(0xFFFF0000)
        hp_ref[rows, :] = lo_bits | hi_bits

        lo = (h2 - hf).astype(BF16)
        wrh = wrh_ref[...]
        lg_ref[rows, :] = (jnp.dot(hi, wrh, preferred_element_type=F32)
                           + jnp.dot(lo, wrh, preferred_element_type=F32)
                           + jnp.dot(hi, wrl_ref[...], preferred_element_type=F32)) + br_ref[...]


def _route_kernel(lg_ref, ids_ref, wts_ref, cnt_ref):
    i = pl.program_id(0)
    lg = lg_ref[:, :LANES]
    le = lg_ref[:, LANES:]
    lane = lax.broadcasted_iota(I32, lg.shape, 1)
    big = jnp.int32(1 << 20)

    def softmax_masked(z, mask):
        zm = jnp.where(mask, z, -jnp.inf)
        e = jnp.exp(zm - jnp.max(zm, axis=-1, keepdims=True))
        return e / jnp.sum(e, axis=-1, keepdims=True)

    def top1(p, mask):
        pm = jnp.where(mask, p, -1.0)
        best = jnp.max(pm, axis=-1, keepdims=True)
        idx = jnp.min(jnp.where(pm == best, lane, big), axis=-1, keepdims=True)
        return best, idx

    gmask = lane < N_GROUPS
    g_w, g_idx = top1(softmax_masked(lg, gmask), gmask)
    emask = (lane >> 3) == g_idx
    p_exp = softmax_masked(le, emask)
    p1, i1 = top1(p_exp, emask)
    p2, i2 = top1(p_exp, emask & (lane != i1))
    denom = p1 + p2
    w1 = g_w * (p1 / denom)
    w2 = g_w * (p2 / denom)
    ids_ref[...] = jnp.where(lane == 0, i1, i2)[:, :TOP_K]
    wts_ref[...] = jnp.where(lane == 0, w1, w2)[:, :TOP_K]

    @pl.when(i == 0)
    def _():
        cnt_ref[...] = jnp.zeros_like(cnt_ref)

    cnt_ref[...] += jnp.sum(((lane == i1) | (lane == i2)).astype(F32), axis=0, keepdims=True)


def _route(logits, tr):
    t = logits.shape[0]
    return pl.pallas_call(
        _route_kernel,
        out_shape=(jax.ShapeDtypeStruct((t, TOP_K), I32),
                   jax.ShapeDtypeStruct((t, TOP_K), F32),
                   jax.ShapeDtypeStruct((1, LANES), F32)),
        grid=(t // tr,),
        in_specs=[pl.BlockSpec((tr, 2 * LANES), lambda i: (i, 0))],
        out_specs=(pl.BlockSpec((tr, TOP_K), lambda i: (i, 0)),
                   pl.BlockSpec((tr, TOP_K), lambda i: (i, 0)),
                   pl.BlockSpec((1, LANES), lambda i: (0, 0))),
        compiler_params=_cparams("arbitrary"),
        name="route",
    )(logits)


def _merge(o_f, o_d, g0, g1, x2, mod3, g2, wpf, wpd, wout, wr_hi, wr_lo, br, tm, seq):
    t, d = x2.shape
    tps = seq // tm
    row = lambda i: (i, 0)
    full = lambda i: (0, 0)
    return pl.pallas_call(
        _merge_kernel,
        out_shape=(jax.ShapeDtypeStruct((t, d), F32),
                   jax.ShapeDtypeStruct((t, d // 2), U32),
                   jax.ShapeDtypeStruct((t, 2 * LANES), F32)),
        grid=(t // tm,),
        in_specs=[pl.BlockSpec((tm, 512), row), pl.BlockSpec((tm, 512), row),
                  pl.BlockSpec((tm, d), row), pl.BlockSpec((tm, d), row),
                  pl.BlockSpec((tm, d), row),
                  pl.BlockSpec((1, 6, d), lambda i: (i // tps, 0, 0)),
                  pl.BlockSpec((1, d), full),
                  pl.BlockSpec(wpf.shape, full), pl.BlockSpec(wpd.shape, full),
                  pl.BlockSpec(wout.shape, full),
                  pl.BlockSpec(wr_hi.shape, full), pl.BlockSpec(wr_lo.shape, full),
                  pl.BlockSpec(br.shape, full)],
        out_specs=(pl.BlockSpec((tm, d), row), pl.BlockSpec((tm, d // 2), row),
                   pl.BlockSpec((tm, 2 * LANES), row)),
        compiler_params=_cparams("arbitrary"),
        name="merge",
    )(o_f, o_d, g0, g1, x2, mod3, g2, wpf, wpd, wout, wr_hi, wr_lo, br)


def _rank_kernel(ids_ref, pstart_ref, ltri_ref, dest_ref, carry_ref):
    i = pl.program_id(0)

    @pl.when(i == 0)
    def _():
        carry_ref[...] = jnp.zeros_like(carry_ref)

    ids = ids_ref[...]
    lane = lax.broadcasted_iota(I32, (ids.shape[0], LANES), 1)
    oh0 = lane == ids[:, 0:1]
    oh1 = lane == ids[:, 1:2]
    both = (oh0 | oh1).astype(BF16)
    before = jnp.dot(ltri_ref[...], both, preferred_element_type=F32)
    base = before + carry_ref[...] + pstart_ref[...]
    d0 = jnp.sum(jnp.where(oh0, base, 0.0), axis=-1, keepdims=True)
    d1 = jnp.sum(jnp.where(oh1, base, 0.0), axis=-1, keepdims=True)
    dest_ref[...] = jnp.where(lane == 0, d0, d1)[:, :TOP_K].astype(I32)
    carry_ref[...] += jnp.sum(both.astype(F32), axis=0, keepdims=True)


def _rank(ids, pad_start_row, tt):
    t = ids.shape[0]
    ltri = (jnp.arange(tt)[:, None] > jnp.arange(tt)[None, :]).astype(BF16)
    return pl.pallas_call(
        _rank_kernel,
        out_shape=jax.ShapeDtypeStruct((t, TOP_K), I32),
        grid=(t // tt,),
        in_specs=[pl.BlockSpec((tt, TOP_K), lambda i: (i, 0)),
                  pl.BlockSpec((1, LANES), lambda i: (0, 0)),
                  pl.BlockSpec((tt, tt), lambda i: (0, 0))],
        out_specs=pl.BlockSpec((tt, TOP_K), lambda i: (i, 0)),
        scratch_shapes=[pltpu.VMEM((1, LANES), F32)],
        compiler_params=_cparams("arbitrary"),
        name="rank",
    )(ids, pad_start_row, ltri)


def _row_copy(src_ref, s, dst_ref, d, sem):
    return pltpu.make_async_copy(src_ref.at[pl.ds(s, 1)], dst_ref.at[pl.ds(d, 1)], sem)


def _dispatch_kernel(pend_ref, padded_ref, nused_ref, dest_ref, h_ref, xs_ref, zero_ref, sem, zsem,
                     *, td, n_blocks):
    i = pl.program_id(0)

    @pl.when(i == 0)
    def _():
        zero_ref[...] = jnp.zeros_like(zero_ref)

        def zero_block(start):
            cp = pltpu.make_async_copy(
                zero_ref, xs_ref.at[pl.ds(pl.multiple_of(start, MOE_BLOCK), MOE_BLOCK)], zsem)
            cp.start()
            cp.wait()

        for e in range(N_EXPERTS):
            @pl.when(padded_ref[e] > 0)
            def _(e=e):
                zero_block(pend_ref[e] - MOE_BLOCK)

            @pl.when(nused_ref[0] + e < n_blocks)
            def _(e=e):
                zero_block((nused_ref[0] + e) * MOE_BLOCK)

    def issue(t, _):
        _row_copy(h_ref, t, xs_ref, dest_ref[2 * t], sem).start(priority=0)
        _row_copy(h_ref, t, xs_ref, dest_ref[2 * t + 1], sem).start(priority=1)
        return 0

    lax.fori_loop(0, td, issue, 0, unroll=DMA_UNROLL)

    def drain(t, _):
        _row_copy(h_ref, 0, xs_ref, 0, sem).wait()
        _row_copy(h_ref, 0, xs_ref, 0, sem).wait()
        return 0

    lax.fori_loop(0, td, drain, 0, unroll=DMA_UNROLL)


def _dispatch(pad_end, padded, n_used, dest_flat, h_packed, p_rows, td):
    t, w = h_packed.shape
    grid_spec = pltpu.PrefetchScalarGridSpec(
        num_scalar_prefetch=3,
        grid=(t // td,),
        in_specs=[pl.BlockSpec((TOP_K * td,), lambda i, *_: (i,), memory_space=pltpu.SMEM),
                  pl.BlockSpec((td, w), lambda i, *_: (i, 0))],
        out_specs=pl.BlockSpec(memory_space=pl.ANY),
        scratch_shapes=[pltpu.VMEM((MOE_BLOCK, w), U32),
                        pltpu.SemaphoreType.DMA(()), pltpu.SemaphoreType.DMA(())],
    )
    return pl.pallas_call(
        functools.partial(_dispatch_kernel, td=td, n_blocks=p_rows // MOE_BLOCK),
        out_shape=jax.ShapeDtypeStruct((p_rows, w), U32),
        grid_spec=grid_spec,
        compiler_params=_cparams("arbitrary"),
        name="dispatch",
    )(pad_end, padded, n_used, dest_flat, h_packed)


def _expert_kernel(be_ref, nused_ref, xs_ref, w1_ref, w3_ref, w2_ref, y_ref, w1b_ref, w3b_ref, w2b_ref):
    p = pl.program_id(0)
    active = p < nused_ref[0]
    new_expert = jnp.logical_or(p == 0, be_ref[p] != be_ref[jnp.maximum(p - 1, 0)])

    @pl.when(jnp.logical_and(active, new_expert))
    def _():
        w1b_ref[...] = w1_ref[0].astype(BF16)
        w3b_ref[...] = w3_ref[0].astype(BF16)
        w2b_ref[...] = w2_ref[0].astype(BF16)

    @pl.when(active)
    def _():
        words = xs_ref[...]
        half = D_MODEL // 2
        xa = pltpu.bitcast(words << 16, F32).astype(BF16)
        xb = pltpu.bitcast(words & jnp.uint32(0xFFFF0000), F32).astype(BF16)

        def up(w_ref):
            return (jnp.dot(xa, w_ref[:half, :], preferred_element_type=F32)
                    + jnp.dot(xb, w_ref[half:, :], preferred_element_type=F32))

        g = up(w1b_ref)
        u = up(w3b_ref)
        act = (g * jax.nn.sigmoid(g) * u).astype(BF16)
        y_ref[...] = jnp.dot(act, w2b_ref[...], preferred_element_type=F32)

    @pl.when(p >= nused_ref[0])
    def _():
        y_ref[...] = jnp.zeros_like(y_ref)


def _experts(block_expert, n_used, xs, w1, w3, w2):
    p_rows, w = xs.shape
    nb = p_rows // MOE_BLOCK
    d = w1.shape[1]
    blk = lambda p, be, nu: (p, 0)
    wsel = lambda p, be, nu: (be[jnp.minimum(p, nu[0] - 1)], 0, 0)
    grid_spec = pltpu.PrefetchScalarGridSpec(
        num_scalar_prefetch=2,
        grid=(nb,),
        in_specs=[pl.BlockSpec((MOE_BLOCK, w), blk),
                  pl.BlockSpec((1, d, D_EXPERT), wsel),
                  pl.BlockSpec((1, d, D_EXPERT), wsel),
                  pl.BlockSpec((1, D_EXPERT, d), wsel)],
        out_specs=pl.BlockSpec((MOE_BLOCK, d), blk),
        scratch_shapes=[pltpu.VMEM((d, D_EXPERT), BF16), pltpu.VMEM((d, D_EXPERT), BF16),
                        pltpu.VMEM((D_EXPERT, d), BF16)],
    )
    return pl.pallas_call(
        _expert_kernel,
        out_shape=jax.ShapeDtypeStruct((p_rows, d), F32),
        grid_spec=grid_spec,
        compiler_params=_cparams("arbitrary"),
        name="experts",
    )(block_expert, n_used, xs, w1, w3, w2)


def _combine_kernel(dcur_ref, dnxt_ref, wts_ref, x1_ref, mod_ref, yb_ref, o_ref, ybuf_ref, sems,
                    *, tc, n_tiles):
    i = pl.program_id(0)
    slot = lax.bitwise_and(i, 1)

    def start_gather(dest_ref, s):
        def issue(t, _):
            _row_copy(yb_ref, dest_ref[2 * t], ybuf_ref.at[s].at[0], t, sems.at[s]).start(priority=0)
            _row_copy(yb_ref, dest_ref[2 * t + 1], ybuf_ref.at[s].at[1], t, sems.at[s]).start(priority=1)
            return 0

        lax.fori_loop(0, tc, issue, 0, unroll=DMA_UNROLL)

    @pl.when(i == 0)
    def _():
        start_gather(dcur_ref, slot)

    @pl.when(i + 1 < n_tiles)
    def _():
        start_gather(dnxt_ref, 1 - slot)

    def drain(t, _):
        _row_copy(yb_ref, 0, ybuf_ref.at[slot].at[0], 0, sems.at[slot]).wait()
        _row_copy(yb_ref, 0, ybuf_ref.at[slot].at[1], 0, sems.at[slot]).wait()
        return 0

    lax.fori_loop(0, tc, drain, 0, unroll=DMA_UNROLL)
    gt2 = mod_ref[0, 5:6, :]
    wts = wts_ref[...]
    y = ybuf_ref[slot, 0] * wts[:, 0:1] + ybuf_ref[slot, 1] * wts[:, 1:2]
    o_ref[...] = x1_ref[...] + gt2 * y


def _combine(dest_flat, wts, x1, mod3, yb, tc, seq):
    t, d = x1.shape
    tps = seq // tc
    n_tiles = t // tc
    dest_block = lambda off: pl.BlockSpec((TOP_K * tc,), lambda i: (jnp.minimum(i + off, n_tiles - 1),),
                                          memory_space=pltpu.SMEM)
    return pl.pallas_call(
        functools.partial(_combine_kernel, tc=tc, n_tiles=n_tiles),
        out_shape=jax.ShapeDtypeStruct((t, d), F32),
        grid=(n_tiles,),
        in_specs=[dest_block(0), dest_block(1),
                  pl.BlockSpec((tc, TOP_K), lambda i: (i, 0)),
                  pl.BlockSpec((tc, d), lambda i: (i, 0)),
                  pl.BlockSpec((1, 6, d), lambda i: (i // tps, 0, 0)),
                  pl.BlockSpec(memory_space=pl.ANY)],
        out_specs=pl.BlockSpec((tc, d), lambda i: (i, 0)),
        scratch_shapes=[pltpu.VMEM((2, 2, tc, d), F32), pltpu.SemaphoreType.DMA((2,))],
        compiler_params=_cparams("arbitrary"),
        name="combine",
    )(dest_flat, dest_flat, wts, x1, mod3, yb)


def _tile(n, pref):
    t = pref
    while n % t:
        t //= 2
    return t


def kernel(x, c, positions, w_ada, b_ada, g_norm1, w_in, b_f, g_q_fox, g_k_fox, g_q_diff, g_k_diff,
           lam_q1, lam_k1, lam_q2, lam_k2, g_subln, w_proj_fox, w_proj_diff, w_out, g_norm2,
           w_router_group, b_router_group, w_router_expert, b_router_expert, w1, w3, w2):
    b, s, d = x.shape
    t = b * s
    assert d == D_MODEL and w_ada.shape[0] == 1 and s % LANES == 0
    tm = _tile(s, 512)
    tq = _tile(s, 512)

    wi = w_in[0]
    c_fk, c_fv, c_ff = FOX_WIDTH, 2 * FOX_WIDTH, 3 * FOX_WIDTH
    c_dq = c_ff + N_FOX_HEADS
    c_gate = c_dq + 3 * 512
    c_dv = c_dq + 2 * 512
    wqk = jnp.concatenate([wi[:, :c_fv], wi[:, c_dq:c_dv]], axis=1).astype(BF16)
    wvt = jnp.concatenate([wi[:, c_fv:c_ff], wi[:, c_dv:c_gate]], axis=1).T.astype(BF16)
    wff_t = jnp.zeros((16, d), BF16).at[:N_FOX_HEADS].set(wi[:, c_ff:c_dq].T.astype(BF16))
    wgate = wi[:, c_gate:].astype(BF16)
    bf_col = jnp.zeros((16, 1), F32).at[:N_FOX_HEADS, 0].set(b_f[0])
    head_id = jnp.arange(MXU_TILE) // HEAD_DIM
    bd = (head_id[:, None] == head_id[None, :]).astype(BF16)
    gvecs = jnp.stack([jnp.tile(g[0], FOX_WIDTH // HEAD_DIM)
                       for g in (g_q_fox, g_k_fox, g_q_diff, g_k_diff)])
    lam_vecs = jnp.stack([lam_q1[0], lam_k1[0], lam_q2[0], lam_k2[0]])
    inv_freq = ROPE_THETA ** (-jnp.arange(0, HEAD_DIM, 2, dtype=F32) / HEAD_DIM)
    invf_col = inv_freq.reshape(HEAD_DIM // 2, 1)
    wr = jnp.zeros((d, 2 * LANES), F32)
    wr = wr.at[:, :N_GROUPS].set(w_router_group[0]).at[:, LANES:LANES + N_EXPERTS].set(w_router_expert[0])
    wr_hi = wr.astype(BF16)
    wr_lo = (wr - wr_hi.astype(F32)).astype(BF16)
    br = jnp.zeros((1, 2 * LANES), F32)
    br = br.at[0, :N_GROUPS].set(b_router_group[0]).at[0, LANES:LANES + N_EXPERTS].set(b_router_expert[0])

    x2 = x.reshape(t, d)
    mod3 = _ada(c, w_ada[0], b_ada[0]).reshape(b, 6, d)
    cos_t, sin_t = _rope_tables(positions.reshape(1, t), invf_col, _tile(t, 2048))

    fq, fk, fv_t, dq, dk, dv_t, g0, g1, lf_t = _inproj(
        x2, mod3, g_norm1, wqk, wvt, wff_t, wgate, bd, gvecs, bf_col, cos_t, sin_t, tm, s)
    dec = _cumsum(lf_t, b, s)
    r3 = lambda a: a.reshape(b, s, a.shape[-1])
    o_f = _fox_attention(r3(fq), r3(fk), fv_t, dec, tq)
    o_d = _diff_attention(r3(dq), r3(dk), dv_t, lam_vecs, g_subln.reshape(2 * HEAD_DIM, 1), tq)

    x1, h_packed, logits = _merge(
        o_f.reshape(t, FOX_WIDTH), o_d.reshape(t, DIFF_WIDTH), g0, g1, x2, mod3, g_norm2,
        w_proj_fox[0].astype(BF16), w_proj_diff[0].astype(BF16), w_out[0].astype(BF16),
        wr_hi, wr_lo, br, tm, s)
    ids, wts, counts = _route(logits, _tile(t, 2048))

    a = t * TOP_K
    n_blocks = -(-a // MOE_BLOCK) + N_EXPERTS
    p_rows = n_blocks * MOE_BLOCK
    cnt = counts[0, :N_EXPERTS].astype(I32)
    padded = ((cnt + MOE_BLOCK - 1) // MOE_BLOCK) * MOE_BLOCK
    pad_end = jnp.cumsum(padded).astype(I32)
    pad_start = pad_end - padded
    block_start = jnp.arange(n_blocks, dtype=I32) * MOE_BLOCK
    block_expert = jnp.minimum(
        jnp.sum((pad_end[None, :] <= block_start[:, None]).astype(I32), axis=1), N_EXPERTS - 1)
    n_used = (pad_end[-1:] // MOE_BLOCK).astype(I32)
    pstart_row = jnp.zeros((1, LANES), F32).at[0, :N_EXPERTS].set(pad_start.astype(F32))

    dest = _rank(ids, pstart_row, _tile(t, 1024)).reshape(a)
    xs = _dispatch(pad_end, padded, n_used, dest, h_packed, p_rows, _tile(t, 512))
    yb = _experts(block_expert, n_used, xs, w1[0], w3[0], w2[0])
    out = _combine(dest, wts, x1, mod3, yb, _tile(s, 256), s)
    return out.reshape(b, s, d)
```

```python
import functools
import math

import jax
import jax.numpy as jnp
from jax import lax
from jax.experimental import pallas as pl
from jax.experimental.pallas import tpu as pltpu

F32 = jnp.float32
BF16 = jnp.bfloat16
I32 = jnp.int32
U32 = jnp.uint32

D_MODEL = 1024
HEAD_DIM = 64
N_FOX_HEADS = 8
N_DIFF_HEADS = 4
FOX_WIDTH = 512
DIFF_WIDTH = 512
CHUNK = 64
CHUNK_SHIFT = CHUNK.bit_length() - 1
ROPE_THETA = 10000.0
N_GROUPS = 4
EXPERTS_PER_GROUP = 8
N_EXPERTS = 32
TOP_K = 2
D_EXPERT = 512
MOE_BLOCK = 512
EPS = 1e-6
NEG_INF = -1e30
LAM0 = 0.8 - 0.6 * math.exp(-0.3 * 0)
LOG2E = math.log2(math.e)
Q_SCALE = HEAD_DIM ** -0.5 * LOG2E

LANES = 128
MXU_TILE = 256
VMEM_LIMIT = 56 * 1024 * 1024

DMA_UNROLL = 8
ONES_ROWS = 16
ROW_CHAINS = 2

_NT = (((1,), (1,)), ((), ()))


def _cparams(*sem):
    return pltpu.CompilerParams(dimension_semantics=sem, vmem_limit_bytes=VMEM_LIMIT)


def _split3(x):
    hi = x.astype(BF16)
    r1 = x - hi.astype(F32)
    mid = r1.astype(BF16)
    lo = (r1 - mid.astype(F32)).astype(BF16)
    return hi, mid, lo


def _ada_kernel(c_ref, w_ref, b_ref, o_ref):
    c = c_ref[...]
    ca = (c * jax.nn.sigmoid(c)).astype(BF16)
    o_ref[...] = jnp.dot(ca, w_ref[...].astype(BF16), preferred_element_type=F32) + b_ref[...]


def _ada(c, w_ada, b_ada):
    b, d = c.shape
    n = w_ada.shape[1]
    tn = 1024
    return pl.pallas_call(
        _ada_kernel,
        out_shape=jax.ShapeDtypeStruct((b, n), F32),
        grid=(n // tn,),
        in_specs=[pl.BlockSpec((b, d), lambda j: (0, 0)),
                  pl.BlockSpec((d, tn), lambda j: (0, j)),
                  pl.BlockSpec((1, tn), lambda j: (0, j))],
        out_specs=pl.BlockSpec((b, tn), lambda j: (0, j)),
        compiler_params=_cparams("arbitrary"),
        name="ada",
    )(c, w_ada, b_ada.reshape(1, n))


def _rope_kernel(pos_ref, invf_ref, cos_ref, sin_ref):
    ang = invf_ref[...] * pos_ref[...].astype(F32)
    c = jnp.cos(ang)
    s = jnp.sin(ang)
    cos_ref[...] = jnp.concatenate([c, c, c, c], axis=0)
    sin_ref[...] = jnp.concatenate([-s, s, -s, s], axis=0)


def _rope_tables(pos_row, invf_col, tm):
    t = pos_row.shape[1]
    tbl = jax.ShapeDtypeStruct((LANES, t), F32)
    return pl.pallas_call(
        _rope_kernel,
        out_shape=(tbl, tbl),
        grid=(t // tm,),
        in_specs=[pl.BlockSpec((1, tm), lambda i: (0, i)),
                  pl.BlockSpec((HEAD_DIM // 2, 1), lambda i: (0, 0))],
        out_specs=(pl.BlockSpec((LANES, tm), lambda i: (0, i)),
                   pl.BlockSpec((LANES, tm), lambda i: (0, i))),
        compiler_params=_cparams("arbitrary"),
        name="rope",
    )(pos_row, invf_col)


def _rms_rows(x, g):
    return x * lax.rsqrt(jnp.mean(x * x, axis=-1, keepdims=True) + EPS) * g


def _head_norm(z, bd, g):
    zz = (z * z).astype(BF16)
    n = bd.shape[0]
    ss = jnp.concatenate([jnp.dot(zz[:, j:j + n], bd, preferred_element_type=F32)
                          for j in range(0, z.shape[1], n)], axis=1)
    return z * lax.rsqrt(ss * (1.0 / HEAD_DIM) + EPS) * g


def _rotate_half(z):
    n = z.shape[-1]
    lane = lax.broadcasted_iota(I32, z.shape, 1)
    fwd = pltpu.roll(z, n - HEAD_DIM // 2, 1)
    bwd = pltpu.roll(z, HEAD_DIM // 2, 1)
    return jnp.where((lane & (HEAD_DIM - 1)) < HEAD_DIM // 2, fwd, bwd)


def _inproj_kernel(x_ref, mod_ref, g1_ref, wqk_ref, wvt_ref, wff_ref, wg_ref, bd_ref, gv_ref, bf_ref,
                   cos_ref, sin_ref,
                   fq_ref, fk_ref, fv_ref, dq_ref, dk_ref, dv_ref, g0_ref, g1o_ref, lf_ref):
    x = x_ref[...]
    sh1 = mod_ref[0, 0:1, :]
    sc1 = mod_ref[0, 1:2, :]
    h = (_rms_rows(x, g1_ref[...]) * (1.0 + sc1) + sh1).astype(BF16)
    bd = bd_ref[...]
    w = FOX_WIDTH

    def proj(j):
        return jnp.dot(h, wqk_ref[:, j * w:(j + 1) * w], preferred_element_type=F32)

    def proj_t(j):
        return lax.dot_general(wvt_ref[j * w:(j + 1) * w, :], h, _NT, preferred_element_type=F32)

    fq_ref[...] = (_head_norm(proj(0), bd, gv_ref[0:1, :]) * Q_SCALE).astype(BF16)
    fk_ref[...] = _head_norm(proj(1), bd, gv_ref[1:2, :]).astype(BF16)
    fv_ref[...] = proj_t(0).astype(BF16)

    cos = jnp.concatenate([cos_ref[...].T] * (w // LANES), axis=1)
    sin = jnp.concatenate([sin_ref[...].T] * (w // LANES), axis=1)
    qn = _head_norm(proj(2), bd, gv_ref[2:3, :])
    dq_ref[...] = ((qn * cos + _rotate_half(qn) * sin) * Q_SCALE).astype(BF16)
    kn = _head_norm(proj(3), bd, gv_ref[3:4, :])
    dk_ref[...] = (kn * cos + _rotate_half(kn) * sin).astype(BF16)
    dv_ref[...] = proj_t(1).astype(BF16)

    for j in range(2):
        zg = jnp.dot(h, wg_ref[:, j * D_MODEL:(j + 1) * D_MODEL], preferred_element_type=F32)
        (g0_ref, g1o_ref)[j][...] = jax.nn.sigmoid(zg).astype(BF16)

    zf = lax.dot_general(wff_ref[...], h, _NT, preferred_element_type=F32) + bf_ref[...]
    lf_ref[...] = jnp.minimum(zf, 0.0) - jnp.log1p(jnp.exp(-jnp.abs(zf)))


def _inproj(x2, mod3, g1, wqk, wvt, wff_t, wgate, bd, gvecs, bf_col, cos_t, sin_t, tm, seq):
    t, d = x2.shape
    tps = seq // tm
    row = lambda i: (i, 0)
    full = lambda i: (0, 0)
    bsd = lambda n: jax.ShapeDtypeStruct((t, n), BF16)
    bsd_t = jax.ShapeDtypeStruct((512, t), BF16)
    tok = pl.BlockSpec((tm, 512), row)
    tok_t = pl.BlockSpec((512, tm), lambda i: (0, i))
    return pl.pallas_call(
        _inproj_kernel,
        out_shape=(bsd(512), bsd(512), bsd_t, bsd(512), bsd(512), bsd_t, bsd(d), bsd(d),
                   jax.ShapeDtypeStruct((16, t), F32)),
        grid=(t // tm,),
        in_specs=[pl.BlockSpec((tm, d), row),
                  pl.BlockSpec((1, 6, d), lambda i: (i // tps, 0, 0)),
                  pl.BlockSpec((1, d), full),
                  pl.BlockSpec(wqk.shape, full),
                  pl.BlockSpec(wvt.shape, full),
                  pl.BlockSpec(wff_t.shape, full),
                  pl.BlockSpec(wgate.shape, full),
                  pl.BlockSpec(bd.shape, full),
                  pl.BlockSpec(gvecs.shape, full),
                  pl.BlockSpec(bf_col.shape, full),
                  pl.BlockSpec((LANES, tm), lambda i: (0, i)),
                  pl.BlockSpec((LANES, tm), lambda i: (0, i))],
        out_specs=(tok, tok, tok_t, tok, tok, tok_t, pl.BlockSpec((tm, d), row), pl.BlockSpec((tm, d), row),
                   pl.BlockSpec((16, tm), lambda i: (0, i))),
        compiler_params=_cparams("arbitrary"),
        name="inproj",
    )(x2, mod3, g1, wqk, wvt, wff_t, wgate, bd, gvecs, bf_col, cos_t, sin_t)


def _cumsum_kernel(lf_ref, tri_ref, eye_ref, sel_ref, ones_ref, dec_ref, *, seq):
    tri = tri_ref[...]
    eye = eye_ref[...]
    sel = sel_ref[...]
    n_blk = seq // LANES
    split = []
    for blk in range(n_blk):
        split += list(_split3(lf_ref[:, blk * LANES:(blk + 1) * LANES]))
    loc = jnp.dot(jnp.concatenate(split, axis=0), tri, preferred_element_type=F32)
    carry = jnp.zeros((16, 1), F32)
    cols = []
    for blk in range(n_blk):
        r = 3 * 16 * blk
        local = loc[r:r + 16] + loc[r + 16:r + 32] + loc[r + 32:r + 48]
        c = local + carry
        carry = carry + local[:, LANES - 1:LANES]
        pieces = jnp.concatenate(list(_split3(c * LOG2E)) + [jnp.zeros((16, LANES), BF16)], axis=0)
        cols.append(lax.dot_general(eye, pieces, _NT, preferred_element_type=F32).astype(BF16))
    dec_ref[0] = (jnp.dot(jnp.concatenate(cols, axis=0), sel, preferred_element_type=F32)
                  + ones_ref[...]).astype(BF16)


def _cumsum(lf_t, batch, seq):
    tri = (jnp.arange(LANES)[:, None] <= jnp.arange(LANES)[None, :]).astype(BF16)
    eye = jnp.eye(LANES, dtype=BF16)
    j, h = jnp.meshgrid(jnp.arange(3), jnp.arange(N_FOX_HEADS), indexing="ij")
    rows = (16 * j + h).reshape(-1)
    sel = jnp.zeros((4 * 16, LANES), F32)
    sel = sel.at[rows, (8 * h + j).reshape(-1)].set(1.0)
    sel = sel.at[rows, (HEAD_DIM + 8 * h + 3 + j).reshape(-1)].set(-1.0).astype(BF16)
    ones = jnp.zeros((1, LANES), F32)
    ones = ones.at[0, (8 * h + 3 + j).reshape(-1)].set(1.0).at[0, (HEAD_DIM + 8 * h + j).reshape(-1)].set(1.0)
    return pl.pallas_call(
        functools.partial(_cumsum_kernel, seq=seq),
        out_shape=jax.ShapeDtypeStruct((batch, seq, LANES), BF16),
        grid=(batch,),
        in_specs=[pl.BlockSpec((16, seq), lambda b: (0, b)),
                  pl.BlockSpec((LANES, LANES), lambda b: (0, 0)),
                  pl.BlockSpec((LANES, LANES), lambda b: (0, 0)),
                  pl.BlockSpec(sel.shape, lambda b: (0, 0)),
                  pl.BlockSpec((1, LANES), lambda b: (0, 0))],
        out_specs=pl.BlockSpec((1, seq, LANES), lambda b: (b, 0, 0)),
        compiler_params=_cparams("arbitrary"),
        name="cumsum",
    )(lf_t, tri, eye, sel, ones)


def _attend(n_chains, nq, q_of, k_of, vt_of, diag_mask, finalize, s_a, s_b, m_ref, acc_ref):
    n_steps = nq * (nq + 1) // 2

    def produce(qi, ki, s_ref):
        for c in range(n_chains):
            s_ref[c] = lax.dot_general(k_of(c, ki), q_of(c, qi), _NT, preferred_element_type=F32)

    def consume(qi, ki, s_ref, masked):
        par = lax.bitwise_and(qi, 1)
        for c in range(n_chains):
            s = s_ref[c]
            if masked:
                s = jnp.where(diag_mask, s, NEG_INF)
            m = jnp.where(ki == 0, NEG_INF, m_ref[par, c])
            m_new = jnp.maximum(m, jnp.max(s, axis=0, keepdims=True))
            p = jnp.exp2(s - m_new).astype(BF16)
            acc_ref[par, c] = (jnp.exp2(m - m_new) * acc_ref[par, c]
                               + jnp.dot(vt_of(c, ki), p, preferred_element_type=F32))
            m_ref[par, c] = m_new

    def advance(qi, ki):
        last = (ki == qi).astype(I32)
        return qi + last, (ki + 1) * (1 - last)

    def two_steps(qi, ki, has_next):
        q1, k1 = advance(qi, ki)
        q2, k2 = advance(q1, k1)
        d0 = ki == qi
        d1 = k1 == q1

        def block(mask0, mask1):
            produce(q1, k1, s_b)
            consume(qi, ki, s_a, mask0)
            if has_next:
                produce(q2, k2, s_a)
            consume(q1, k1, s_b, mask1)

        pl.when(d0)(lambda: block(True, False))
        pl.when(d1)(lambda: block(False, True))
        pl.when(jnp.logical_not(jnp.logical_or(d0, d1)))(lambda: block(False, False))
        pl.when(d0)(lambda: finalize(qi))
        pl.when(d1)(lambda: finalize(q1))
        return q2, k2

    m_ref[...] = jnp.full(m_ref.shape, NEG_INF, F32)
    acc_ref[...] = jnp.zeros(acc_ref.shape, F32)
    produce(0, 0, s_a)
    n_pairs = (n_steps - 1) // 2
    qi, ki = lax.fori_loop(0, n_pairs, lambda _, carry: two_steps(*carry, True),
                           (jnp.int32(0), jnp.int32(0)))
    if n_steps - 2 * n_pairs == 2:
        two_steps(qi, ki, False)
    else:
        consume(qi, ki, s_a, True)
        finalize(qi)


def _attn_scratch(tq, rows):
    return [pltpu.VMEM((2, tq, tq), F32), pltpu.VMEM((2, tq, tq), F32),
            pltpu.VMEM((2, 2, 1, tq), F32), pltpu.VMEM((2, 2, rows, tq), F32)]


def _fox_kernel(q_ref, k_ref, vt_ref, dec_ref, o_ref, qaug_ref, kaug_ref, vtaug_ref, s_a, s_b,
                m_ref, acc_ref, *, tq):
    hp = pl.program_id(1)
    seq = q_ref.shape[1]
    nq = seq // tq
    qdec = dec_ref[0, :, :HEAD_DIM]
    kdec = dec_ref[0, :, HEAD_DIM:].astype(F32)
    lane_head = lax.shift_right_logical(lax.broadcasted_iota(I32, kdec.shape, 1), 3)
    ones = jnp.ones((ONES_ROWS, tq), BF16)
    for hh in range(2):
        hs = slice(hh * HEAD_DIM, (hh + 1) * HEAD_DIM)
        kd = jnp.where(lane_head == 2 * hp + hh, kdec, 0.0).astype(BF16)
        kaug_ref[hh] = jnp.concatenate([k_ref[0, :, hs], kd], axis=-1)
        qaug_ref[hh] = jnp.concatenate([q_ref[0, :, hs], qdec], axis=-1)
        for j in range(nq):
            vtaug_ref[hh, j] = jnp.concatenate([vt_ref[hs, j * tq:(j + 1) * tq], ones], axis=0)

    key = lax.broadcasted_iota(I32, (tq, tq), 0)
    qry = lax.broadcasted_iota(I32, (tq, tq), 1)

    def tile(ref, c, i):
        return ref[c, pl.ds(pl.multiple_of(i * tq, tq), tq), :]

    def finalize(qi):
        outs = []
        for c in range(2):
            acc = acc_ref[lax.bitwise_and(qi, 1), c]
            outs.append(acc[:HEAD_DIM] / acc[HEAD_DIM:HEAD_DIM + 1])
        o_t = jnp.concatenate(outs, axis=0)
        o_ref[0, pl.ds(pl.multiple_of(qi * tq, tq), tq), :] = o_t.T.astype(BF16)

    _attend(2, nq, functools.partial(tile, qaug_ref), functools.partial(tile, kaug_ref),
            lambda c, ki: vtaug_ref[c, ki], key <= qry, finalize, s_a, s_b, m_ref, acc_ref)


def _fox_attention(fq, fk, fv_t, dec, tq):
    b, s, _ = fq.shape
    head_pair = pl.BlockSpec((1, s, LANES), lambda bi, hp: (bi, 0, hp))
    return pl.pallas_call(
        functools.partial(_fox_kernel, tq=tq),
        out_shape=jax.ShapeDtypeStruct((b, s, FOX_WIDTH), BF16),
        grid=(b, N_FOX_HEADS // 2),
        in_specs=[head_pair, head_pair,
                  pl.BlockSpec((LANES, s), lambda bi, hp: (hp, bi)),
                  pl.BlockSpec((1, s, LANES), lambda bi, hp: (bi, 0, 0))],
        out_specs=head_pair,
        scratch_shapes=[pltpu.VMEM((2, s, 2 * HEAD_DIM), BF16), pltpu.VMEM((2, s, 2 * HEAD_DIM), BF16),
                        pltpu.VMEM((2, s // tq, HEAD_DIM + ONES_ROWS, tq), BF16)]
        + _attn_scratch(tq, HEAD_DIM + ONES_ROWS),
        compiler_params=_cparams("arbitrary", "arbitrary"),
        name="fox",
    )(fq, fk, fv_t, dec)


def _diff_kernel(q_ref, k_ref, vt_ref, lam_ref, gs_ref, o_ref, vtaug_ref, s_a, s_b, m_ref, acc_ref, *, tq):
    dv = 2 * HEAD_DIM
    seq = q_ref.shape[1]
    nq = seq // tq
    ones = jnp.ones((ONES_ROWS, tq), BF16)
    for j in range(nq):
        vtaug_ref[j] = jnp.concatenate([vt_ref[:, j * tq:(j + 1) * tq], ones], axis=0)

    key = lax.broadcasted_iota(I32, (tq, tq), 0)
    qry = lax.broadcasted_iota(I32, (tq, tq), 1)
    chunk_causal = lax.shift_right_logical(key, CHUNK_SHIFT) <= lax.shift_right_logical(qry, CHUNK_SHIFT)
    lv = lam_ref[...]
    lam = (jnp.exp(jnp.sum(lv[0:1] * lv[1:2], axis=-1, keepdims=True))
           - jnp.exp(jnp.sum(lv[2:3] * lv[3:4], axis=-1, keepdims=True)) + LAM0)

    def rows(i):
        return pl.ds(pl.multiple_of(i * tq, tq), tq)

    def q_of(c, qi):
        return q_ref[0, rows(qi), c * HEAD_DIM:(c + 1) * HEAD_DIM]

    def k_of(c, ki):
        return k_ref[0, rows(ki), c * HEAD_DIM:(c + 1) * HEAD_DIM]

    def finalize(qi):
        par = lax.bitwise_and(qi, 1)
        outs = [acc_ref[par, c][:dv] / acc_ref[par, c][dv:dv + 1] for c in range(2)]
        o_t = outs[0] - lam * outs[1]
        inv = lax.rsqrt(jnp.mean(o_t * o_t, axis=0, keepdims=True) + EPS)
        o_t = o_t * inv * gs_ref[...] * (1.0 - LAM0)
        o_ref[0, rows(qi), :] = o_t.T.astype(BF16)

    _attend(2, nq, q_of, k_of, lambda c, ki: vtaug_ref[ki], chunk_causal, finalize,
            s_a, s_b, m_ref, acc_ref)


def _diff_attention(dq, dk, dv_t, lam_vecs, g_subln_col, tq):
    b, s, _ = dq.shape
    head = pl.BlockSpec((1, s, LANES), lambda bi, h: (bi, 0, h))
    rows = 2 * HEAD_DIM + ONES_ROWS
    return pl.pallas_call(
        functools.partial(_diff_kernel, tq=tq),
        out_shape=jax.ShapeDtypeStruct((b, s, DIFF_WIDTH), BF16),
        grid=(b, N_DIFF_HEADS),
        in_specs=[head, head,
                  pl.BlockSpec((LANES, s), lambda bi, h: (h, bi)),
                  pl.BlockSpec((4, HEAD_DIM), lambda bi, h: (0, 0)),
                  pl.BlockSpec((2 * HEAD_DIM, 1), lambda bi, h: (0, 0))],
        out_specs=head,
        scratch_shapes=[pltpu.VMEM((s // tq, rows, tq), BF16)] + _attn_scratch(tq, rows),
        compiler_params=_cparams("arbitrary", "arbitrary"),
        name="diff",
    )(dq, dk, dv_t, lam_vecs, g_subln_col)


def _merge_kernel(of_ref, od_ref, g0_ref, g1_ref, x_ref, mod_ref, g2_ref, wpf_ref, wpd_ref, wout_ref,
                  wrh_ref, wrl_ref, br_ref,
                  x1_ref, hp_ref, lg_ref):
    gt1 = mod_ref[0, 2:3, :]
    sh2 = mod_ref[0, 3:4, :]
    sc2 = mod_ref[0, 4:5, :]
    sub = x_ref.shape[0] // ROW_CHAINS
    for r in range(ROW_CHAINS):
        rows = pl.ds(r * sub, sub)
        a = jnp.dot(of_ref[rows, :], wpf_ref[...], preferred_element_type=F32)
        b = jnp.dot(od_ref[rows, :], wpd_ref[...], preferred_element_type=F32)
        merged = g0_ref[rows, :].astype(F32) * a + g1_ref[rows, :].astype(F32) * b
        x1 = x_ref[rows, :] + gt1 * jnp.dot(merged.astype(BF16), wout_ref[...], preferred_element_type=F32)
        x1_ref[rows, :] = x1
        h2 = _rms_rows(x1, g2_ref[...]) * (1.0 + sc2) + sh2

        hi = h2.astype(BF16)
        hf = hi.astype(F32)
        half = D_MODEL // 2
        lo_bits = pltpu.bitcast(hf[:, :half], U32) >> 16
        hi_bits = pltpu.bitcast(hf[:, half:], U32) & jnp.uint32(0xFFFF0000)
        hp_ref[rows, :] = lo_bits | hi_bits

        lo = (h2 - hf).astype(BF16)
        wrh = wrh_ref[...]
        lg_ref[rows, :] = (jnp.dot(hi, wrh, preferred_element_type=F32)
                           + jnp.dot(lo, wrh, preferred_element_type=F32)
                           + jnp.dot(hi, wrl_ref[...], preferred_element_type=F32)) + br_ref[...]


def _route_kernel(lg_ref, ids_ref, wts_ref, cnt_ref):
    i = pl.program_id(0)
    lg = lg_ref[:, :LANES]
    le = lg_ref[:, LANES:]
    lane = lax.broadcasted_iota(I32, lg.shape, 1)
    big = jnp.int32(1 << 20)

    def softmax_masked(z, mask):
        zm = jnp.where(mask, z, -jnp.inf)
        e = jnp.exp(zm - jnp.max(zm, axis=-1, keepdims=True))
        return e / jnp.sum(e, axis=-1, keepdims=True)

    def top1(p, mask):
        pm = jnp.where(mask, p, -1.0)
        best = jnp.max(pm, axis=-1, keepdims=True)
        idx = jnp.min(jnp.where(pm == best, lane, big), axis=-1, keepdims=True)
        return best, idx

    gmask = lane < N_GROUPS
    g_w, g_idx = top1(softmax_masked(lg, gmask), gmask)
    emask = (lane >> 3) == g_idx
    p_exp = softmax_masked(le, emask)
    p1, i1 = top1(p_exp, emask)
    p2, i2 = top1(p_exp, emask & (lane != i1))
    denom = p1 + p2
    w1 = g_w * (p1 / denom)
    w2 = g_w * (p2 / denom)
    ids_ref[...] = jnp.where(lane == 0, i1, i2)[:, :TOP_K]
    wts_ref[...] = jnp.where(lane == 0, w1, w2)[:, :TOP_K]

    @pl.when(i == 0)
    def _():
        cnt_ref[...] = jnp.zeros_like(cnt_ref)

    cnt_ref[...] += jnp.sum(((lane == i1) | (lane == i2)).astype(F32), axis=0, keepdims=True)


def _route(logits, tr):
    t = logits.shape[0]
    return pl.pallas_call(
        _route_kernel,
        out_shape=(jax.ShapeDtypeStruct((t, TOP_K), I32),
                   jax.ShapeDtypeStruct((t, TOP_K), F32),
                   jax.ShapeDtypeStruct((1, LANES), F32)),
        grid=(t // tr,),
        in_specs=[pl.BlockSpec((tr, 2 * LANES), lambda i: (i, 0))],
        out_specs=(pl.BlockSpec((tr, TOP_K), lambda i: (i, 0)),
                   pl.BlockSpec((tr, TOP_K), lambda i: (i, 0)),
                   pl.BlockSpec((1, LANES), lambda i: (0, 0))),
        compiler_params=_cparams("arbitrary"),
        name="route",
    )(logits)


def _merge(o_f, o_d, g0, g1, x2, mod3, g2, wpf, wpd, wout, wr_hi, wr_lo, br, tm, seq):
    t, d = x2.shape
    tps = seq // tm
    row = lambda i: (i, 0)
    full = lambda i: (0, 0)
    return pl.pallas_call(
        _merge_kernel,
        out_shape=(jax.ShapeDtypeStruct((t, d), F32),
                   jax.ShapeDtypeStruct((t, d // 2), U32),
                   jax.ShapeDtypeStruct((t, 2 * LANES), F32)),
        grid=(t // tm,),
        in_specs=[pl.BlockSpec((tm, 512), row), pl.BlockSpec((tm, 512), row),
                  pl.BlockSpec((tm, d), row), pl.BlockSpec((tm, d), row),
                  pl.BlockSpec((tm, d), row),
                  pl.BlockSpec((1, 6, d), lambda i: (i // tps, 0, 0)),
                  pl.BlockSpec((1, d), full),
                  pl.BlockSpec(wpf.shape, full), pl.BlockSpec(wpd.shape, full),
                  pl.BlockSpec(wout.shape, full),
                  pl.BlockSpec(wr_hi.shape, full), pl.BlockSpec(wr_lo.shape, full),
                  pl.BlockSpec(br.shape, full)],
        out_specs=(pl.BlockSpec((tm, d), row), pl.BlockSpec((tm, d // 2), row),
                   pl.BlockSpec((tm, 2 * LANES), row)),
        compiler_params=_cparams("arbitrary"),
        name="merge",
    )(o_f, o_d, g0, g1, x2, mod3, g2, wpf, wpd, wout, wr_hi, wr_lo, br)


def _rank_kernel(ids_ref, pstart_ref, ltri_ref, dest_ref, carry_ref):
    i = pl.program_id(0)

    @pl.when(i == 0)
    def _():
        carry_ref[...] = jnp.zeros_like(carry_ref)

    ids = ids_ref[...]
    lane = lax.broadcasted_iota(I32, (ids.shape[0], LANES), 1)
    oh0 = lane == ids[:, 0:1]
    oh1 = lane == ids[:, 1:2]
    both = (oh0 | oh1).astype(BF16)
    before = jnp.dot(ltri_ref[...], both, preferred_element_type=F32)
    base = before + carry_ref[...] + pstart_ref[...]
    d0 = jnp.sum(jnp.where(oh0, base, 0.0), axis=-1, keepdims=True)
    d1 = jnp.sum(jnp.where(oh1, base, 0.0), axis=-1, keepdims=True)
    dest_ref[...] = jnp.where(lane == 0, d0, d1)[:, :TOP_K].astype(I32)
    carry_ref[...] += jnp.sum(both.astype(F32), axis=0, keepdims=True)


def _rank(ids, pad_start_row, tt):
    t = ids.shape[0]
    ltri = (jnp.arange(tt)[:, None] > jnp.arange(tt)[None, :]).astype(BF16)
    return pl.pallas_call(
        _rank_kernel,
        out_shape=jax.ShapeDtypeStruct((t, TOP_K), I32),
        grid=(t // tt,),
        in_specs=[pl.BlockSpec((tt, TOP_K), lambda i: (i, 0)),
                  pl.BlockSpec((1, LANES), lambda i: (0, 0)),
                  pl.BlockSpec((tt, tt), lambda i: (0, 0))],
        out_specs=pl.BlockSpec((tt, TOP_K), lambda i: (i, 0)),
        scratch_shapes=[pltpu.VMEM((1, LANES), F32)],
        compiler_params=_cparams("arbitrary"),
        name="rank",
    )(ids, pad_start_row, ltri)


def _row_copy(src_ref, s, dst_ref, d, sem):
    return pltpu.make_async_copy(src_ref.at[pl.ds(s, 1)], dst_ref.at[pl.ds(d, 1)], sem)


def _dispatch_kernel(pend_ref, padded_ref, nused_ref, dest_ref, h_ref, xs_ref, zero_ref, hbuf_ref, sems, zsem,
                     *, td, n_blocks, n_tiles):
    i = pl.program_id(0)

    @pl.when(i == 0)
    def _():
        zero_ref[...] = jnp.zeros_like(zero_ref)

        def zero_block(start):
            cp = pltpu.make_async_copy(
                zero_ref, xs_ref.at[pl.ds(pl.multiple_of(start, MOE_BLOCK), MOE_BLOCK)], zsem)
            cp.start()
            cp.wait()

        for e in range(N_EXPERTS):
            @pl.when(padded_ref[e] > 0)
            def _(e=e):
                zero_block(pend_ref[e] - MOE_BLOCK)

            @pl.when(nused_ref[0] + e < n_blocks)
            def _(e=e):
                zero_block((nused_ref[0] + e) * MOE_BLOCK)

    slot = lax.bitwise_and(i, 1)
    hbuf_ref[slot] = h_ref[...]

    def issue(t, _):
        _row_copy(hbuf_ref.at[slot], t, xs_ref, dest_ref[2 * t], sems.at[slot]).start(priority=0)
        _row_copy(hbuf_ref.at[slot], t, xs_ref, dest_ref[2 * t + 1], sems.at[slot]).start(priority=1)
        return 0

    lax.fori_loop(0, td, issue, 0, unroll=DMA_UNROLL)

    def drain_slot(s):
        def drain(t, _):
            _row_copy(hbuf_ref.at[s], 0, xs_ref, 0, sems.at[s]).wait()
            _row_copy(hbuf_ref.at[s], 0, xs_ref, 0, sems.at[s]).wait()
            return 0

        lax.fori_loop(0, td, drain, 0, unroll=DMA_UNROLL)

    @pl.when(i > 0)
    def _():
        drain_slot(1 - slot)

    @pl.when(i == n_tiles - 1)
    def _():
        drain_slot(slot)


def _dispatch(pad_end, padded, n_used, dest_flat, h_packed, p_rows, td):
    t, w = h_packed.shape
    grid_spec = pltpu.PrefetchScalarGridSpec(
        num_scalar_prefetch=3,
        grid=(t // td,),
        in_specs=[pl.BlockSpec((TOP_K * td,), lambda i, *_: (i,), memory_space=pltpu.SMEM),
                  pl.BlockSpec((td, w), lambda i, *_: (i, 0))],
        out_specs=pl.BlockSpec(memory_space=pl.ANY),
        scratch_shapes=[pltpu.VMEM((MOE_BLOCK, w), U32), pltpu.VMEM((2, td, w), U32),
                        pltpu.SemaphoreType.DMA((2,)), pltpu.SemaphoreType.DMA(())],
    )
    return pl.pallas_call(
        functools.partial(_dispatch_kernel, td=td, n_blocks=p_rows // MOE_BLOCK, n_tiles=t // td),
        out_shape=jax.ShapeDtypeStruct((p_rows, w), U32),
        grid_spec=grid_spec,
        compiler_params=_cparams("arbitrary"),
        name="dispatch",
    )(pad_end, padded, n_used, dest_flat, h_packed)


def _expert_kernel(be_ref, nused_ref, xs_ref, w1_ref, w3_ref, w2_ref, y_ref, w1b_ref, w3b_ref, w2b_ref):
    p = pl.program_id(0)
    active = p < nused_ref[0]
    new_expert = jnp.logical_or(p == 0, be_ref[p] != be_ref[jnp.maximum(p - 1, 0)])

    @pl.when(jnp.logical_and(active, new_expert))
    def _():
        w1b_ref[...] = w1_ref[0].astype(BF16)
        w3b_ref[...] = w3_ref[0].astype(BF16)
        w2b_ref[...] = w2_ref[0].astype(BF16)

    @pl.when(active)
    def _():
        words = xs_ref[...]
        half = D_MODEL // 2
        xa = pltpu.bitcast(words << 16, F32).astype(BF16)
        xb = pltpu.bitcast(words & jnp.uint32(0xFFFF0000), F32).astype(BF16)

        def up(w_ref):
            return (jnp.dot(xa, w_ref[:half, :], preferred_element_type=F32)
                    + jnp.dot(xb, w_ref[half:, :], preferred_element_type=F32))

        g = up(w1b_ref)
        u = up(w3b_ref)
        act = (g * jax.nn.sigmoid(g) * u).astype(BF16)
        y_ref[...] = jnp.dot(act, w2b_ref[...], preferred_element_type=F32)

    @pl.when(p >= nused_ref[0])
    def _():
        y_ref[...] = jnp.zeros_like(y_ref)


def _experts(block_expert, n_used, xs, w1, w3, w2):
    p_rows, w = xs.shape
    nb = p_rows // MOE_BLOCK
    d = w1.shape[1]
    blk = lambda p, be, nu: (p, 0)
    wsel = lambda p, be, nu: (be[jnp.minimum(p, nu[0] - 1)], 0, 0)
    grid_spec = pltpu.PrefetchScalarGridSpec(
        num_scalar_prefetch=2,
        grid=(nb,),
        in_specs=[pl.BlockSpec((MOE_BLOCK, w), blk),
                  pl.BlockSpec((1, d, D_EXPERT), wsel),
                  pl.BlockSpec((1, d, D_EXPERT), wsel),
                  pl.BlockSpec((1, D_EXPERT, d), wsel)],
        out_specs=pl.BlockSpec((MOE_BLOCK, d), blk),
        scratch_shapes=[pltpu.VMEM((d, D_EXPERT), BF16), pltpu.VMEM((d, D_EXPERT), BF16),
                        pltpu.VMEM((D_EXPERT, d), BF16)],
    )
    return pl.pallas_call(
        _expert_kernel,
        out_shape=jax.ShapeDtypeStruct((p_rows, d), F32),
        grid_spec=grid_spec,
        compiler_params=_cparams("arbitrary"),
        name="experts",
    )(block_expert, n_used, xs, w1, w3, w2)


def _combine_kernel(dcur_ref, dnxt_ref, wts_ref, x1_ref, mod_ref, yb_ref, o_ref, ybuf_ref, sems,
                    *, tc, n_tiles):
    i = pl.program_id(0)
    slot = lax.bitwise_and(i, 1)

    def start_gather(dest_ref, s):
        def issue(t, _):
            _row_copy(yb_ref, dest_ref[2 * t], ybuf_ref.at[s].at[0], t, sems.at[s]).start(priority=0)
            _row_copy(yb_ref, dest_ref[2 * t + 1], ybuf_ref.at[s].at[1], t, sems.at[s]).start(priority=1)
            return 0

        lax.fori_loop(0, tc, issue, 0, unroll=DMA_UNROLL)

    @pl.when(i == 0)
    def _():
        start_gather(dcur_ref, slot)

    @pl.when(i + 1 < n_tiles)
    def _():
        start_gather(dnxt_ref, 1 - slot)

    def drain(t, _):
        _row_copy(yb_ref, 0, ybuf_ref.at[slot].at[0], 0, sems.at[slot]).wait()
        _row_copy(yb_ref, 0, ybuf_ref.at[slot].at[1], 0, sems.at[slot]).wait()
        return 0

    lax.fori_loop(0, tc, drain, 0, unroll=DMA_UNROLL)
    gt2 = mod_ref[0, 5:6, :]
    wts = wts_ref[...]
    y = ybuf_ref[slot, 0] * wts[:, 0:1] + ybuf_ref[slot, 1] * wts[:, 1:2]
    o_ref[...] = x1_ref[...] + gt2 * y


def _combine(dest_flat, wts, x1, mod3, yb, tc, seq):
    t, d = x1.shape
    tps = seq // tc
    n_tiles = t // tc
    dest_block = lambda off: pl.BlockSpec((TOP_K * tc,), lambda i: (jnp.minimum(i + off, n_tiles - 1),),
                                          memory_space=pltpu.SMEM)
    return pl.pallas_call(
        functools.partial(_combine_kernel, tc=tc, n_tiles=n_tiles),
        out_shape=jax.ShapeDtypeStruct((t, d), F32),
        grid=(n_tiles,),
        in_specs=[dest_block(0), dest_block(1),
                  pl.BlockSpec((tc, TOP_K), lambda i: (i, 0)),
                  pl.BlockSpec((tc, d), lambda i: (i, 0)),
                  pl.BlockSpec((1, 6, d), lambda i: (i // tps, 0, 0)),
                  pl.BlockSpec(memory_space=pl.ANY)],
        out_specs=pl.BlockSpec((tc, d), lambda i: (i, 0)),
        scratch_shapes=[pltpu.VMEM((2, 2, tc, d), F32), pltpu.SemaphoreType.DMA((2,))],
        compiler_params=_cparams("arbitrary"),
        name="combine",
    )(dest_flat, dest_flat, wts, x1, mod3, yb)


def _tile(n, pref):
    t = pref
    while n % t:
        t //= 2
    return t


def kernel(x, c, positions, w_ada, b_ada, g_norm1, w_in, b_f, g_q_fox, g_k_fox, g_q_diff, g_k_diff,
           lam_q1, lam_k1, lam_q2, lam_k2, g_subln, w_proj_fox, w_proj_diff, w_out, g_norm2,
           w_router_group, b_router_group, w_router_expert, b_router_expert, w1, w3, w2):
    b, s, d = x.shape
    t = b * s
    assert d == D_MODEL and w_ada.shape[0] == 1 and s % LANES == 0
    tm = _tile(s, 512)
    tq = _tile(s, 512)

    wi = w_in[0]
    c_fk, c_fv, c_ff = FOX_WIDTH, 2 * FOX_WIDTH, 3 * FOX_WIDTH
    c_dq = c_ff + N_FOX_HEADS
    c_gate = c_dq + 3 * 512
    c_dv = c_dq + 2 * 512
    wqk = jnp.concatenate([wi[:, :c_fv], wi[:, c_dq:c_dv]], axis=1).astype(BF16)
    wvt = jnp.concatenate([wi[:, c_fv:c_ff], wi[:, c_dv:c_gate]], axis=1).T.astype(BF16)
    wff_t = jnp.zeros((16, d), BF16).at[:N_FOX_HEADS].set(wi[:, c_ff:c_dq].T.astype(BF16))
    wgate = wi[:, c_gate:].astype(BF16)
    bf_col = jnp.zeros((16, 1), F32).at[:N_FOX_HEADS, 0].set(b_f[0])
    head_id = jnp.arange(MXU_TILE) // HEAD_DIM
    bd = (head_id[:, None] == head_id[None, :]).astype(BF16)
    gvecs = jnp.stack([jnp.tile(g[0], FOX_WIDTH // HEAD_DIM)
                       for g in (g_q_fox, g_k_fox, g_q_diff, g_k_diff)])
    lam_vecs = jnp.stack([lam_q1[0], lam_k1[0], lam_q2[0], lam_k2[0]])
    inv_freq = ROPE_THETA ** (-jnp.arange(0, HEAD_DIM, 2, dtype=F32) / HEAD_DIM)
    invf_col = inv_freq.reshape(HEAD_DIM // 2, 1)
    wr = jnp.zeros((d, 2 * LANES), F32)
    wr = wr.at[:, :N_GROUPS].set(w_router_group[0]).at[:, LANES:LANES + N_EXPERTS].set(w_router_expert[0])
    wr_hi = wr.astype(BF16)
    wr_lo = (wr - wr_hi.astype(F32)).astype(BF16)
    br = jnp.zeros((1, 2 * LANES), F32)
    br = br.at[0, :N_GROUPS].set(b_router_group[0]).at[0, LANES:LANES + N_EXPERTS].set(b_router_expert[0])

    x2 = x.reshape(t, d)
    mod3 = _ada(c, w_ada[0], b_ada[0]).reshape(b, 6, d)
    cos_t, sin_t = _rope_tables(positions.reshape(1, t), invf_col, _tile(t, 2048))

    fq, fk, fv_t, dq, dk, dv_t, g0, g1, lf_t = _inproj(
        x2, mod3, g_norm1, wqk, wvt, wff_t, wgate, bd, gvecs, bf_col, cos_t, sin_t, tm, s)
    dec = _cumsum(lf_t, b, s)
    r3 = lambda a: a.reshape(b, s, a.shape[-1])
    o_f = _fox_attention(r3(fq), r3(fk), fv_t, dec, tq)
    o_d = _diff_attention(r3(dq), r3(dk), dv_t, lam_vecs, g_subln.reshape(2 * HEAD_DIM, 1), tq)

    x1, h_packed, logits = _merge(
        o_f.reshape(t, FOX_WIDTH), o_d.reshape(t, DIFF_WIDTH), g0, g1, x2, mod3, g_norm2,
        w_proj_fox[0].astype(BF16), w_proj_diff[0].astype(BF16), w_out[0].astype(BF16),
        wr_hi, wr_lo, br, tm, s)
    ids, wts, counts = _route(logits, _tile(t, 2048))

    a = t * TOP_K
    n_blocks = -(-a // MOE_BLOCK) + N_EXPERTS
    p_rows = n_blocks * MOE_BLOCK
    cnt = counts[0, :N_EXPERTS].astype(I32)
    padded = ((cnt + MOE_BLOCK - 1) // MOE_BLOCK) * MOE_BLOCK
    pad_end = jnp.cumsum(padded).astype(I32)
    pad_start = pad_end - padded
    block_start = jnp.arange(n_blocks, dtype=I32) * MOE_BLOCK
    block_expert = jnp.minimum(
        jnp.sum((pad_end[None, :] <= block_start[:, None]).astype(I32), axis=1), N_EXPERTS - 1)
    n_used = (pad_end[-1:] // MOE_BLOCK).astype(I32)
    pstart_row = jnp.zeros((1, LANES), F32).at[0, :N_EXPERTS].set(pad_start.astype(F32))

    dest = _rank(ids, pstart_row, _tile(t, 1024)).reshape(a)
    xs = _dispatch(pad_end, padded, n_used, dest, h_packed, p_rows, _tile(t, 512))
    yb = _experts(block_expert, n_used, xs, w1[0], w3[0], w2[0])
    out = _combine(dest, wts, x1, mod3, yb, _tile(s, 256), s)
    return out.reshape(b, s, d)
```

```python
import functools
import math

import jax
import jax.numpy as jnp
from jax import lax
from jax.experimental import pallas as pl
from jax.experimental.pallas import tpu as pltpu

F32 = jnp.float32
BF16 = jnp.bfloat16
I32 = jnp.int32
U32 = jnp.uint32

D_MODEL = 1024
HEAD_DIM = 64
N_FOX_HEADS = 8
N_DIFF_HEADS = 4
FOX_WIDTH = 512
DIFF_WIDTH = 512
CHUNK = 64
CHUNK_SHIFT = CHUNK.bit_length() - 1
ROPE_THETA = 10000.0
N_GROUPS = 4
EXPERTS_PER_GROUP = 8
N_EXPERTS = 32
TOP_K = 2
D_EXPERT = 512
MOE_BLOCK = 512
EPS = 1e-6
NEG_INF = -1e30
LAM0 = 0.8 - 0.6 * math.exp(-0.3 * 0)
LOG2E = math.log2(math.e)
Q_SCALE = HEAD_DIM ** -0.5 * LOG2E

LANES = 128
MXU_TILE = 256
VMEM_LIMIT = 56 * 1024 * 1024

DMA_UNROLL = 8
ONES_ROWS = 16
ROW_CHAINS = 2

_NT = (((1,), (1,)), ((), ()))


def _cparams(*sem):
    return pltpu.CompilerParams(dimension_semantics=sem, vmem_limit_bytes=VMEM_LIMIT)


def _split3(x):
    hi = x.astype(BF16)
    r1 = x - hi.astype(F32)
    mid = r1.astype(BF16)
    lo = (r1 - mid.astype(F32)).astype(BF16)
    return hi, mid, lo


def _ada_kernel(c_ref, w_ref, b_ref, o_ref):
    c = c_ref[...]
    ca = (c * jax.nn.sigmoid(c)).astype(BF16)
    o_ref[...] = jnp.dot(ca, w_ref[...].astype(BF16), preferred_element_type=F32) + b_ref[...]


def _ada(c, w_ada, b_ada):
    b, d = c.shape
    n = w_ada.shape[1]
    tn = 1024
    return pl.pallas_call(
        _ada_kernel,
        out_shape=jax.ShapeDtypeStruct((b, n), F32),
        grid=(n // tn,),
        in_specs=[pl.BlockSpec((b, d), lambda j: (0, 0)),
                  pl.BlockSpec((d, tn), lambda j: (0, j)),
                  pl.BlockSpec((1, tn), lambda j: (0, j))],
        out_specs=pl.BlockSpec((b, tn), lambda j: (0, j)),
        compiler_params=_cparams("arbitrary"),
        name="ada",
    )(c, w_ada, b_ada.reshape(1, n))


def _rope_kernel(pos_ref, invf_ref, cos_ref, sin_ref):
    ang = invf_ref[...] * pos_ref[...].astype(F32)
    c = jnp.cos(ang)
    s = jnp.sin(ang)
    cos_ref[...] = jnp.concatenate([c, c, c, c], axis=0)
    sin_ref[...] = jnp.concatenate([-s, s, -s, s], axis=0)


def _rope_tables(pos_row, invf_col, tm):
    t = pos_row.shape[1]
    tbl = jax.ShapeDtypeStruct((LANES, t), F32)
    return pl.pallas_call(
        _rope_kernel,
        out_shape=(tbl, tbl),
        grid=(t // tm,),
        in_specs=[pl.BlockSpec((1, tm), lambda i: (0, i)),
                  pl.BlockSpec((HEAD_DIM // 2, 1), lambda i: (0, 0))],
        out_specs=(pl.BlockSpec((LANES, tm), lambda i: (0, i)),
                   pl.BlockSpec((LANES, tm), lambda i: (0, i))),
        compiler_params=_cparams("arbitrary"),
        name="rope",
    )(pos_row, invf_col)


def _rms_rows(x, g):
    return x * lax.rsqrt(jnp.mean(x * x, axis=-1, keepdims=True) + EPS) * g


def _head_norm(z, bd, g):
    zz = (z * z).astype(BF16)
    n = bd.shape[0]
    ss = jnp.concatenate([jnp.dot(zz[:, j:j + n], bd, preferred_element_type=F32)
                          for j in range(0, z.shape[1], n)], axis=1)
    return z * lax.rsqrt(ss * (1.0 / HEAD_DIM) + EPS) * g


def _rotate_half(z):
    n = z.shape[-1]
    lane = lax.broadcasted_iota(I32, z.shape, 1)
    fwd = pltpu.roll(z, n - HEAD_DIM // 2, 1)
    bwd = pltpu.roll(z, HEAD_DIM // 2, 1)
    return jnp.where((lane & (HEAD_DIM - 1)) < HEAD_DIM // 2, fwd, bwd)


def _inproj_kernel(x_ref, mod_ref, g1_ref, wqk_ref, wvt_ref, wff_ref, wg_ref, bd_ref, gv_ref, bf_ref,
                   cos_ref, sin_ref,
                   fq_ref, fk_ref, fv_ref, dq_ref, dk_ref, dv_ref, g0_ref, g1o_ref, lf_ref):
    x = x_ref[...]
    sh1 = mod_ref[0, 0:1, :]
    sc1 = mod_ref[0, 1:2, :]
    h = (_rms_rows(x, g1_ref[...]) * (1.0 + sc1) + sh1).astype(BF16)
    bd = bd_ref[...]
    w = FOX_WIDTH

    def proj(j):
        return jnp.dot(h, wqk_ref[:, j * w:(j + 1) * w], preferred_element_type=F32)

    def proj_t(j):
        return lax.dot_general(wvt_ref[j * w:(j + 1) * w, :], h, _NT, preferred_element_type=F32)

    fq_ref[...] = (_head_norm(proj(0), bd, gv_ref[0:1, :]) * Q_SCALE).astype(BF16)
    fk_ref[...] = _head_norm(proj(1), bd, gv_ref[1:2, :]).astype(BF16)
    fv_ref[...] = proj_t(0).astype(BF16)

    cos = jnp.concatenate([cos_ref[...].T] * (w // LANES), axis=1)
    sin = jnp.concatenate([sin_ref[...].T] * (w // LANES), axis=1)
    qn = _head_norm(proj(2), bd, gv_ref[2:3, :])
    dq_ref[...] = ((qn * cos + _rotate_half(qn) * sin) * Q_SCALE).astype(BF16)
    kn = _head_norm(proj(3), bd, gv_ref[3:4, :])
    dk_ref[...] = (kn * cos + _rotate_half(kn) * sin).astype(BF16)
    dv_ref[...] = proj_t(1).astype(BF16)

    for j in range(2):
        zg = jnp.dot(h, wg_ref[:, j * D_MODEL:(j + 1) * D_MODEL], preferred_element_type=F32)
        (g0_ref, g1o_ref)[j][...] = jax.nn.sigmoid(zg).astype(BF16)

    zf = lax.dot_general(wff_ref[...], h, _NT, preferred_element_type=F32) + bf_ref[...]
    lf_ref[...] = jnp.minimum(zf, 0.0) - jnp.log1p(jnp.exp(-jnp.abs(zf)))


def _inproj(x2, mod3, g1, wqk, wvt, wff_t, wgate, bd, gvecs, bf_col, cos_t, sin_t, tm, seq):
    t, d = x2.shape
    tps = seq // tm
    row = lambda i: (i, 0)
    full = lambda i: (0, 0)
    bsd = lambda n: jax.ShapeDtypeStruct((t, n), BF16)
    bsd_t = jax.ShapeDtypeStruct((512, t), BF16)
    tok = pl.BlockSpec((tm, 512), row)
    tok_t = pl.BlockSpec((512, tm), lambda i: (0, i))
    return pl.pallas_call(
        _inproj_kernel,
        out_shape=(bsd(512), bsd(512), bsd_t, bsd(512), bsd(512), bsd_t, bsd(d), bsd(d),
                   jax.ShapeDtypeStruct((16, t), F32)),
        grid=(t // tm,),
        in_specs=[pl.BlockSpec((tm, d), row),
                  pl.BlockSpec((1, 6, d), lambda i: (i // tps, 0, 0)),
                  pl.BlockSpec((1, d), full),
                  pl.BlockSpec(wqk.shape, full),
                  pl.BlockSpec(wvt.shape, full),
                  pl.BlockSpec(wff_t.shape, full),
                  pl.BlockSpec(wgate.shape, full),
                  pl.BlockSpec(bd.shape, full),
                  pl.BlockSpec(gvecs.shape, full),
                  pl.BlockSpec(bf_col.shape, full),
                  pl.BlockSpec((LANES, tm), lambda i: (0, i)),
                  pl.BlockSpec((LANES, tm), lambda i: (0, i))],
        out_specs=(tok, tok, tok_t, tok, tok, tok_t, pl.BlockSpec((tm, d), row), pl.BlockSpec((tm, d), row),
                   pl.BlockSpec((16, tm), lambda i: (0, i))),
        compiler_params=_cparams("arbitrary"),
        name="inproj",
    )(x2, mod3, g1, wqk, wvt, wff_t, wgate, bd, gvecs, bf_col, cos_t, sin_t)


def _cumsum_kernel(lf_ref, tri_ref, eye_ref, sel_ref, ones_ref, dec_ref, *, seq):
    tri = tri_ref[...]
    eye = eye_ref[...]
    sel = sel_ref[...]
    n_blk = seq // LANES
    split = []
    for blk in range(n_blk):
        split += list(_split3(lf_ref[:, blk * LANES:(blk + 1) * LANES]))
    loc = jnp.dot(jnp.concatenate(split, axis=0), tri, preferred_element_type=F32)
    carry = jnp.zeros((16, 1), F32)
    cols = []
    for blk in range(n_blk):
        r = 3 * 16 * blk
        local = loc[r:r + 16] + loc[r + 16:r + 32] + loc[r + 32:r + 48]
        c = local + carry
        carry = carry + local[:, LANES - 1:LANES]
        pieces = jnp.concatenate(list(_split3(c * LOG2E)) + [jnp.zeros((16, LANES), BF16)], axis=0)
        cols.append(lax.dot_general(eye, pieces, _NT, preferred_element_type=F32).astype(BF16))
    dec_ref[0] = (jnp.dot(jnp.concatenate(cols, axis=0), sel, preferred_element_type=F32)
                  + ones_ref[...]).astype(BF16)


def _cumsum(lf_t, batch, seq):
    tri = (jnp.arange(LANES)[:, None] <= jnp.arange(LANES)[None, :]).astype(BF16)
    eye = jnp.eye(LANES, dtype=BF16)
    j, h = jnp.meshgrid(jnp.arange(3), jnp.arange(N_FOX_HEADS), indexing="ij")
    rows = (16 * j + h).reshape(-1)
    sel = jnp.zeros((4 * 16, LANES), F32)
    sel = sel.at[rows, (8 * h + j).reshape(-1)].set(1.0)
    sel = sel.at[rows, (HEAD_DIM + 8 * h + 3 + j).reshape(-1)].set(-1.0).astype(BF16)
    ones = jnp.zeros((1, LANES), F32)
    ones = ones.at[0, (8 * h + 3 + j).reshape(-1)].set(1.0).at[0, (HEAD_DIM + 8 * h + j).reshape(-1)].set(1.0)
    return pl.pallas_call(
        functools.partial(_cumsum_kernel, seq=seq),
        out_shape=jax.ShapeDtypeStruct((batch, seq, LANES), BF16),
        grid=(batch,),
        in_specs=[pl.BlockSpec((16, seq), lambda b: (0, b)),
                  pl.BlockSpec((LANES, LANES), lambda b: (0, 0)),
                  pl.BlockSpec((LANES, LANES), lambda b: (0, 0)),
                  pl.BlockSpec(sel.shape, lambda b: (0, 0)),
                  pl.BlockSpec((1, LANES), lambda b: (0, 0))],
        out_specs=pl.BlockSpec((1, seq, LANES), lambda b: (b, 0, 0)),
        compiler_params=_cparams("arbitrary"),
        name="cumsum",
    )(lf_t, tri, eye, sel, ones)


def _attend(n_chains, nq, q_of, k_of, vt_of, diag_mask, finalize, s_a, s_b, m_ref, acc_ref):
    n_steps = nq * (nq + 1) // 2

    def produce(qi, ki, s_ref):
        for c in range(n_chains):
            s_ref[c] = lax.dot_general(k_of(c, ki), q_of(c, qi), _NT, preferred_element_type=F32)

    def consume(qi, ki, s_ref, masked):
        par = lax.bitwise_and(qi, 1)
        for c in range(n_chains):
            s = s_ref[c]
            if masked:
                s = jnp.where(diag_mask, s, NEG_INF)
            m = jnp.where(ki == 0, NEG_INF, m_ref[par, c])
            m_new = jnp.maximum(m, jnp.max(s, axis=0, keepdims=True))
            p = jnp.exp2(s - m_new).astype(BF16)
            acc_ref[par, c] = (jnp.exp2(m - m_new) * acc_ref[par, c]
                               + jnp.dot(vt_of(c, ki), p, preferred_element_type=F32))
            m_ref[par, c] = m_new

    def advance(qi, ki):
        last = (ki == qi).astype(I32)
        return qi + last, (ki + 1) * (1 - last)

    def two_steps(qi, ki, has_next):
        q1, k1 = advance(qi, ki)
        q2, k2 = advance(q1, k1)
        d0 = ki == qi
        d1 = k1 == q1

        def block(mask0, mask1):
            produce(q1, k1, s_b)
            consume(qi, ki, s_a, mask0)
            if has_next:
                produce(q2, k2, s_a)
            consume(q1, k1, s_b, mask1)

        pl.when(d0)(lambda: block(True, False))
        pl.when(d1)(lambda: block(False, True))
        pl.when(jnp.logical_not(jnp.logical_or(d0, d1)))(lambda: block(False, False))
        pl.when(d0)(lambda: finalize(qi))
        pl.when(d1)(lambda: finalize(q1))
        return q2, k2

    m_ref[...] = jnp.full(m_ref.shape, NEG_INF, F32)
    acc_ref[...] = jnp.zeros(acc_ref.shape, F32)
    produce(0, 0, s_a)
    n_pairs = (n_steps - 1) // 2
    qi, ki = lax.fori_loop(0, n_pairs, lambda _, carry: two_steps(*carry, True),
                           (jnp.int32(0), jnp.int32(0)))
    if n_steps - 2 * n_pairs == 2:
        two_steps(qi, ki, False)
    else:
        consume(qi, ki, s_a, True)
        finalize(qi)


def _attn_scratch(tq, rows):
    return [pltpu.VMEM((2, tq, tq), F32), pltpu.VMEM((2, tq, tq), F32),
            pltpu.VMEM((2, 2, 1, tq), F32), pltpu.VMEM((2, 2, rows, tq), F32)]


def _fox_kernel(q_ref, k_ref, vt_ref, dec_ref, o_ref, qaug_ref, kaug_ref, vtaug_ref, s_a, s_b,
                m_ref, acc_ref, *, tq):
    hp = pl.program_id(1)
    seq = q_ref.shape[1]
    nq = seq // tq
    qdec = dec_ref[0, :, :HEAD_DIM]
    kdec = dec_ref[0, :, HEAD_DIM:].astype(F32)
    lane_head = lax.shift_right_logical(lax.broadcasted_iota(I32, kdec.shape, 1), 3)
    ones = jnp.ones((ONES_ROWS, tq), BF16)
    for hh in range(2):
        hs = slice(hh * HEAD_DIM, (hh + 1) * HEAD_DIM)
        kd = jnp.where(lane_head == 2 * hp + hh, kdec, 0.0).astype(BF16)
        kaug_ref[hh] = jnp.concatenate([k_ref[0, :, hs], kd], axis=-1)
        qaug_ref[hh] = jnp.concatenate([q_ref[0, :, hs], qdec], axis=-1)
        for j in range(nq):
            vtaug_ref[hh, j] = jnp.concatenate([vt_ref[hs, j * tq:(j + 1) * tq], ones], axis=0)

    key = lax.broadcasted_iota(I32, (tq, tq), 0)
    qry = lax.broadcasted_iota(I32, (tq, tq), 1)

    def tile(ref, c, i):
        return ref[c, pl.ds(pl.multiple_of(i * tq, tq), tq), :]

    def finalize(qi):
        outs = []
        for c in range(2):
            acc = acc_ref[lax.bitwise_and(qi, 1), c]
            outs.append(acc[:HEAD_DIM] / acc[HEAD_DIM:HEAD_DIM + 1])
        o_t = jnp.concatenate(outs, axis=0)
        o_ref[0, pl.ds(pl.multiple_of(qi * tq, tq), tq), :] = o_t.T.astype(BF16)

    _attend(2, nq, functools.partial(tile, qaug_ref), functools.partial(tile, kaug_ref),
            lambda c, ki: vtaug_ref[c, ki], key <= qry, finalize, s_a, s_b, m_ref, acc_ref)


def _fox_attention(fq, fk, fv_t, dec, tq):
    b, s, _ = fq.shape
    head_pair = pl.BlockSpec((1, s, LANES), lambda bi, hp: (bi, 0, hp))
    return pl.pallas_call(
        functools.partial(_fox_kernel, tq=tq),
        out_shape=jax.ShapeDtypeStruct((b, s, FOX_WIDTH), BF16),
        grid=(b, N_FOX_HEADS // 2),
        in_specs=[head_pair, head_pair,
                  pl.BlockSpec((LANES, s), lambda bi, hp: (hp, bi)),
                  pl.BlockSpec((1, s, LANES), lambda bi, hp: (bi, 0, 0))],
        out_specs=head_pair,
        scratch_shapes=[pltpu.VMEM((2, s, 2 * HEAD_DIM), BF16), pltpu.VMEM((2, s, 2 * HEAD_DIM), BF16),
                        pltpu.VMEM((2, s // tq, HEAD_DIM + ONES_ROWS, tq), BF16)]
        + _attn_scratch(tq, HEAD_DIM + ONES_ROWS),
        compiler_params=_cparams("arbitrary", "arbitrary"),
        name="fox",
    )(fq, fk, fv_t, dec)


def _diff_kernel(q_ref, k_ref, vt_ref, lam_ref, gs_ref, o_ref, vtaug_ref, s_a, s_b, m_ref, acc_ref, *, tq):
    dv = 2 * HEAD_DIM
    seq = q_ref.shape[1]
    nq = seq // tq
    ones = jnp.ones((ONES_ROWS, tq), BF16)
    for j in range(nq):
        vtaug_ref[j] = jnp.concatenate([vt_ref[:, j * tq:(j + 1) * tq], ones], axis=0)

    key = lax.broadcasted_iota(I32, (tq, tq), 0)
    qry = lax.broadcasted_iota(I32, (tq, tq), 1)
    chunk_causal = lax.shift_right_logical(key, CHUNK_SHIFT) <= lax.shift_right_logical(qry, CHUNK_SHIFT)
    lv = lam_ref[...]
    lam = (jnp.exp(jnp.sum(lv[0:1] * lv[1:2], axis=-1, keepdims=True))
           - jnp.exp(jnp.sum(lv[2:3] * lv[3:4], axis=-1, keepdims=True)) + LAM0)

    def rows(i):
        return pl.ds(pl.multiple_of(i * tq, tq), tq)

    def q_of(c, qi):
        return q_ref[0, rows(qi), c * HEAD_DIM:(c + 1) * HEAD_DIM]

    def k_of(c, ki):
        return k_ref[0, rows(ki), c * HEAD_DIM:(c + 1) * HEAD_DIM]

    def finalize(qi):
        par = lax.bitwise_and(qi, 1)
        outs = [acc_ref[par, c][:dv] / acc_ref[par, c][dv:dv + 1] for c in range(2)]
        o_t = outs[0] - lam * outs[1]
        inv = lax.rsqrt(jnp.mean(o_t * o_t, axis=0, keepdims=True) + EPS)
        o_t = o_t * inv * gs_ref[...] * (1.0 - LAM0)
        o_ref[0, rows(qi), :] = o_t.T.astype(BF16)

    _attend(2, nq, q_of, k_of, lambda c, ki: vtaug_ref[ki], chunk_causal, finalize,
            s_a, s_b, m_ref, acc_ref)


def _diff_attention(dq, dk, dv_t, lam_vecs, g_subln_col, tq):
    b, s, _ = dq.shape
    head = pl.BlockSpec((1, s, LANES), lambda bi, h: (bi, 0, h))
    rows = 2 * HEAD_DIM + ONES_ROWS
    return pl.pallas_call(
        functools.partial(_diff_kernel, tq=tq),
        out_shape=jax.ShapeDtypeStruct((b, s, DIFF_WIDTH), BF16),
        grid=(b, N_DIFF_HEADS),
        in_specs=[head, head,
                  pl.BlockSpec((LANES, s), lambda bi, h: (h, bi)),
                  pl.BlockSpec((4, HEAD_DIM), lambda bi, h: (0, 0)),
                  pl.BlockSpec((2 * HEAD_DIM, 1), lambda bi, h: (0, 0))],
        out_specs=head,
        scratch_shapes=[pltpu.VMEM((s // tq, rows, tq), BF16)] + _attn_scratch(tq, rows),
        compiler_params=_cparams("arbitrary", "arbitrary"),
        name="diff",
    )(dq, dk, dv_t, lam_vecs, g_subln_col)


def _merge_kernel(of_ref, od_ref, g0_ref, g1_ref, x_ref, mod_ref, g2_ref, wpf_ref, wpd_ref, wout_ref,
                  wrh_ref, wrl_ref, br_ref,
                  x1_ref, hp_ref, lg_ref):
    gt1 = mod_ref[0, 2:3, :]
    sh2 = mod_ref[0, 3:4, :]
    sc2 = mod_ref[0, 4:5, :]
    sub = x_ref.shape[0] // ROW_CHAINS
    for r in range(ROW_CHAINS):
        rows = pl.ds(r * sub, sub)
        a = jnp.dot(of_ref[rows, :], wpf_ref[...], preferred_element_type=F32)
        b = jnp.dot(od_ref[rows, :], wpd_ref[...], preferred_element_type=F32)
        merged = g0_ref[rows, :].astype(F32) * a + g1_ref[rows, :].astype(F32) * b
        x1 = x_ref[rows, :] + gt1 * jnp.dot(merged.astype(BF16), wout_ref[...], preferred_element_type=F32)
        x1_ref[rows, :] = x1
        h2 = _rms_rows(x1, g2_ref[...]) * (1.0 + sc2) + sh2

        hi = h2.astype(BF16)
        hf = hi.astype(F32)
        half = D_MODEL // 2
        lo_bits = pltpu.bitcast(hf[:, :half], U32) >> 16
        hi_bits = pltpu.bitcast(hf[:, half:], U32) & jnp.uint32(0xFFFF0000)
        hp_ref[rows, :] = lo_bits | hi_bits

        lo = (h2 - hf).astype(BF16)
        wrh = wrh_ref[...]
        lg_ref[rows, :] = (jnp.dot(hi, wrh, preferred_element_type=F32)
                           + jnp.dot(lo, wrh, preferred_element_type=F32)
                           + jnp.dot(hi, wrl_ref[...], preferred_element_type=F32)) + br_ref[...]


def _route_kernel(lg_ref, ids_ref, wts_ref, cnt_ref):
    i = pl.program_id(0)
    lg = lg_ref[:, :LANES]
    le = lg_ref[:, LANES:]
    lane = lax.broadcasted_iota(I32, lg.shape, 1)
    big = jnp.int32(1 << 20)

    def softmax_masked(z, mask):
        zm = jnp.where(mask, z, -jnp.inf)
        e = jnp.exp(zm - jnp.max(zm, axis=-1, keepdims=True))
        return e / jnp.sum(e, axis=-1, keepdims=True)

    def top1(p, mask):
        pm = jnp.where(mask, p, -1.0)
        best = jnp.max(pm, axis=-1, keepdims=True)
        idx = jnp.min(jnp.where(pm == best, lane, big), axis=-1, keepdims=True)
        return best, idx

    gmask = lane < N_GROUPS
    g_w, g_idx = top1(softmax_masked(lg, gmask), gmask)
    emask = (lane >> 3) == g_idx
    p_exp = softmax_masked(le, emask)
    p1, i1 = top1(p_exp, emask)
    p2, i2 = top1(p_exp, emask & (lane != i1))
    denom = p1 + p2
    w1 = g_w * (p1 / denom)
    w2 = g_w * (p2 / denom)
    ids_ref[...] = jnp.where(lane == 0, i1, i2)[:, :TOP_K]
    wts_ref[...] = jnp.where(lane == 0, w1, w2)[:, :TOP_K]

    @pl.when(i == 0)
    def _():
        cnt_ref[...] = jnp.zeros_like(cnt_ref)

    cnt_ref[...] += jnp.sum(((lane == i1) | (lane == i2)).astype(F32), axis=0, keepdims=True)


def _route(logits, tr):
    t = logits.shape[0]
    return pl.pallas_call(
        _route_kernel,
        out_shape=(jax.ShapeDtypeStruct((t, TOP_K), I32),
                   jax.ShapeDtypeStruct((t, TOP_K), F32),
                   jax.ShapeDtypeStruct((1, LANES), F32)),
        grid=(t // tr,),
        in_specs=[pl.BlockSpec((tr, 2 * LANES), lambda i: (i, 0))],
        out_specs=(pl.BlockSpec((tr, TOP_K), lambda i: (i, 0)),
                   pl.BlockSpec((tr, TOP_K), lambda i: (i, 0)),
                   pl.BlockSpec((1, LANES), lambda i: (0, 0))),
        compiler_params=_cparams("arbitrary"),
        name="route",
    )(logits)


def _merge(o_f, o_d, g0, g1, x2, mod3, g2, wpf, wpd, wout, wr_hi, wr_lo, br, tm, seq):
    t, d = x2.shape
    tps = seq // tm
    row = lambda i: (i, 0)
    full = lambda i: (0, 0)
    return pl.pallas_call(
        _merge_kernel,
        out_shape=(jax.ShapeDtypeStruct((t, d), F32),
                   jax.ShapeDtypeStruct((t, d // 2), U32),
                   jax.ShapeDtypeStruct((t, 2 * LANES), F32)),
        grid=(t // tm,),
        in_specs=[pl.BlockSpec((tm, 512), row), pl.BlockSpec((tm, 512), row),
                  pl.BlockSpec((tm, d), row), pl.BlockSpec((tm, d), row),
                  pl.BlockSpec((tm, d), row),
                  pl.BlockSpec((1, 6, d), lambda i: (i // tps, 0, 0)),
                  pl.BlockSpec((1, d), full),
                  pl.BlockSpec(wpf.shape, full), pl.BlockSpec(wpd.shape, full),
                  pl.BlockSpec(wout.shape, full),
                  pl.BlockSpec(wr_hi.shape, full), pl.BlockSpec(wr_lo.shape, full),
                  pl.BlockSpec(br.shape, full)],
        out_specs=(pl.BlockSpec((tm, d), row), pl.BlockSpec((tm, d // 2), row),
                   pl.BlockSpec((tm, 2 * LANES), row)),
        compiler_params=_cparams("arbitrary"),
        name="merge",
    )(o_f, o_d, g0, g1, x2, mod3, g2, wpf, wpd, wout, wr_hi, wr_lo, br)


def _rank_kernel(ids_ref, pstart_ref, ltri_ref, dest_ref, carry_ref):
    i = pl.program_id(0)

    @pl.when(i == 0)
    def _():
        carry_ref[...] = jnp.zeros_like(carry_ref)

    ids = ids_ref[...]
    lane = lax.broadcasted_iota(I32, (ids.shape[0], LANES), 1)
    oh0 = lane == ids[:, 0:1]
    oh1 = lane == ids[:, 1:2]
    both = (oh0 | oh1).astype(BF16)
    before = jnp.dot(ltri_ref[...], both, preferred_element_type=F32)
    base = before + carry_ref[...] + pstart_ref[...]
    d0 = jnp.sum(jnp.where(oh0, base, 0.0), axis=-1, keepdims=True)
    d1 = jnp.sum(jnp.where(oh1, base, 0.0), axis=-1, keepdims=True)
    dest_ref[...] = jnp.where(lane == 0, d0, d1)[:, :TOP_K].astype(I32)
    carry_ref[...] += jnp.sum(both.astype(F32), axis=0, keepdims=True)


def _rank(ids, pad_start_row, tt):
    t = ids.shape[0]
    ltri = (jnp.arange(tt)[:, None] > jnp.arange(tt)[None, :]).astype(BF16)
    return pl.pallas_call(
        _rank_kernel,
        out_shape=jax.ShapeDtypeStruct((t, TOP_K), I32),
        grid=(t // tt,),
        in_specs=[pl.BlockSpec((tt, TOP_K), lambda i: (i, 0)),
                  pl.BlockSpec((1, LANES), lambda i: (0, 0)),
                  pl.BlockSpec((tt, tt), lambda i: (0, 0))],
        out_specs=pl.BlockSpec((tt, TOP_K), lambda i: (i, 0)),
        scratch_shapes=[pltpu.VMEM((1, LANES), F32)],
        compiler_params=_cparams("arbitrary"),
        name="rank",
    )(ids, pad_start_row, ltri)


def _row_copy(src_ref, s, dst_ref, d, sem):
    return pltpu.make_async_copy(src_ref.at[pl.ds(s, 1)], dst_ref.at[pl.ds(d, 1)], sem)


def _dispatch_kernel(pend_ref, padded_ref, nused_ref, dest_ref, h_ref, xs_ref, zero_ref, sem, zsem,
                     *, td, n_blocks):
    i = pl.program_id(0)

    @pl.when(i == 0)
    def _():
        zero_ref[...] = jnp.zeros_like(zero_ref)

        def zero_block(start):
            cp = pltpu.make_async_copy(
                zero_ref, xs_ref.at[pl.ds(pl.multiple_of(start, MOE_BLOCK), MOE_BLOCK)], zsem)
            cp.start()
            cp.wait()

        for e in range(N_EXPERTS):
            @pl.when(padded_ref[e] > 0)
            def _(e=e):
                zero_block(pend_ref[e] - MOE_BLOCK)

            @pl.when(nused_ref[0] + e < n_blocks)
            def _(e=e):
                zero_block((nused_ref[0] + e) * MOE_BLOCK)

    def issue(t, _):
        _row_copy(h_ref, t, xs_ref, dest_ref[2 * t], sem).start(priority=0)
        _row_copy(h_ref, t, xs_ref, dest_ref[2 * t + 1], sem).start(priority=1)
        return 0

    lax.fori_loop(0, td, issue, 0, unroll=DMA_UNROLL)

    def drain(t, _):
        _row_copy(h_ref, 0, xs_ref, 0, sem).wait()
        _row_copy(h_ref, 0, xs_ref, 0, sem).wait()
        return 0

    lax.fori_loop(0, td, drain, 0, unroll=DMA_UNROLL)


def _dispatch(pad_end, padded, n_used, dest_flat, h_packed, p_rows, td):
    t, w = h_packed.shape
    grid_spec = pltpu.PrefetchScalarGridSpec(
        num_scalar_prefetch=3,
        grid=(t // td,),
        in_specs=[pl.BlockSpec((TOP_K * td,), lambda i, *_: (i,), memory_space=pltpu.SMEM),
                  pl.BlockSpec((td, w), lambda i, *_: (i, 0))],
        out_specs=pl.BlockSpec(memory_space=pl.ANY),
        scratch_shapes=[pltpu.VMEM((MOE_BLOCK, w), U32),
                        pltpu.SemaphoreType.DMA(()), pltpu.SemaphoreType.DMA(())],
    )
    return pl.pallas_call(
        functools.partial(_dispatch_kernel, td=td, n_blocks=p_rows // MOE_BLOCK),
        out_shape=jax.ShapeDtypeStruct((p_rows, w), U32),
        grid_spec=grid_spec,
        compiler_params=_cparams("arbitrary"),
        name="dispatch",
    )(pad_end, padded, n_used, dest_flat, h_packed)


def _expert_kernel(be_ref, nused_ref, xs_ref, w1_ref, w3_ref, w2_ref, y_ref, w1b_ref, w3b_ref, w2b_ref):
    p = pl.program_id(0)
    active = p < nused_ref[0]
    new_expert = jnp.logical_or(p == 0, be_ref[p] != be_ref[jnp.maximum(p - 1, 0)])

    @pl.when(jnp.logical_and(active, new_expert))
    def _():
        w1b_ref[...] = w1_ref[0].astype(BF16)
        w3b_ref[...] = w3_ref[0].astype(BF16)
        w2b_ref[...] = w2_ref[0].astype(BF16)

    @pl.when(active)
    def _():
        words = xs_ref[...]
        half = D_MODEL // 2
        xa = pltpu.bitcast(words << 16, F32).astype(BF16)
        xb = pltpu.bitcast(words & jnp.uint32(0xFFFF0000), F32).astype(BF16)

        def up(w_ref):
            return (jnp.dot(xa, w_ref[:half, :], preferred_element_type=F32)
                    + jnp.dot(xb, w_ref[half:, :], preferred_element_type=F32))

        g = up(w1b_ref)
        u = up(w3b_ref)
        act = (g * jax.nn.sigmoid(g) * u).astype(BF16)
        y_ref[...] = jnp.dot(act, w2b_ref[...], preferred_element_type=F32)

    @pl.when(p >= nused_ref[0])
    def _():
        y_ref[...] = jnp.zeros_like(y_ref)


def _experts(block_expert, n_used, xs, w1, w3, w2):
    p_rows, w = xs.shape
    nb = p_rows // MOE_BLOCK
    d = w1.shape[1]
    blk = lambda p, be, nu: (p, 0)
    wsel = lambda p, be, nu: (be[jnp.minimum(p, nu[0] - 1)], 0, 0)
    grid_spec = pltpu.PrefetchScalarGridSpec(
        num_scalar_prefetch=2,
        grid=(nb,),
        in_specs=[pl.BlockSpec((MOE_BLOCK, w), blk),
                  pl.BlockSpec((1, d, D_EXPERT), wsel),
                  pl.BlockSpec((1, d, D_EXPERT), wsel),
                  pl.BlockSpec((1, D_EXPERT, d), wsel)],
        out_specs=pl.BlockSpec((MOE_BLOCK, d), blk),
        scratch_shapes=[pltpu.VMEM((d, D_EXPERT), BF16), pltpu.VMEM((d, D_EXPERT), BF16),
                        pltpu.VMEM((D_EXPERT, d), BF16)],
    )
    return pl.pallas_call(
        _expert_kernel,
        out_shape=jax.ShapeDtypeStruct((p_rows, d), F32),
        grid_spec=grid_spec,
        compiler_params=_cparams("arbitrary"),
        name="experts",
    )(block_expert, n_used, xs, w1, w3, w2)


def _combine_kernel(dcur_ref, dnxt_ref, wts_ref, x1_ref, mod_ref, yb_ref, o_ref, ybuf_ref, sems,
                    *, tc, n_tiles):
    i = pl.program_id(0)
    slot = lax.bitwise_and(i, 1)

    def start_gather(dest_ref, s):
        def issue(t, _):
            _row_copy(yb_ref, dest_ref[2 * t], ybuf_ref.at[s].at[0], t, sems.at[s]).start(priority=0)
            _row_copy(yb_ref, dest_ref[2 * t + 1], ybuf_ref.at[s].at[1], t, sems.at[s]).start(priority=1)
            return 0

        lax.fori_loop(0, tc, issue, 0, unroll=DMA_UNROLL)

    @pl.when(i == 0)
    def _():
        start_gather(dcur_ref, slot)

    @pl.when(i + 1 < n_tiles)
    def _():
        start_gather(dnxt_ref, 1 - slot)

    def drain(t, _):
        _row_copy(yb_ref, 0, ybuf_ref.at[slot].at[0], 0, sems.at[slot]).wait()
        _row_copy(yb_ref, 0, ybuf_ref.at[slot].at[1], 0, sems.at[slot]).wait()
        return 0

    lax.fori_loop(0, tc, drain, 0, unroll=DMA_UNROLL)
    gt2 = mod_ref[0, 5:6, :]
    wts = wts_ref[...]
    y = ybuf_ref[slot, 0] * wts[:, 0:1] + ybuf_ref[slot, 1] * wts[:, 1:2]
    o_ref[...] = x1_ref[...] + gt2 * y


def _combine(dest_flat, wts, x1, mod3, yb, tc, seq):
    t, d = x1.shape
    tps = seq // tc
    n_tiles = t // tc
    dest_block = lambda off: pl.BlockSpec((TOP_K * tc,), lambda i: (jnp.minimum(i + off, n_tiles - 1),),
                                          memory_space=pltpu.SMEM)
    return pl.pallas_call(
        functools.partial(_combine_kernel, tc=tc, n_tiles=n_tiles),
        out_shape=jax.ShapeDtypeStruct((t, d), F32),
        grid=(n_tiles,),
        in_specs=[dest_block(0), dest_block(1),
                  pl.BlockSpec((tc, TOP_K), lambda i: (i, 0)),
                  pl.BlockSpec((tc, d), lambda i: (i, 0)),
                  pl.BlockSpec((1, 6, d), lambda i: (i // tps, 0, 0)),
                  pl.BlockSpec(memory_space=pl.ANY)],
        out_specs=pl.BlockSpec((tc, d), lambda i: (i, 0)),
        scratch_shapes=[pltpu.VMEM((2, 2, tc, d), F32), pltpu.SemaphoreType.DMA((2,))],
        compiler_params=_cparams("arbitrary"),
        name="combine",
    )(dest_flat, dest_flat, wts, x1, mod3, yb)


def _tile(n, pref):
    t = pref
    while n % t:
        t //= 2
    return t


def kernel(x, c, positions, w_ada, b_ada, g_norm1, w_in, b_f, g_q_fox, g_k_fox, g_q_diff, g_k_diff,
           lam_q1, lam_k1, lam_q2, lam_k2, g_subln, w_proj_fox, w_proj_diff, w_out, g_norm2,
           w_router_group, b_router_group, w_router_expert, b_router_expert, w1, w3, w2):
    b, s, d = x.shape
    t = b * s
    assert d == D_MODEL and w_ada.shape[0] == 1 and s % LANES == 0
    tm = _tile(s, 512)
    tq = _tile(s, 512)

    wi = w_in[0]
    c_fk, c_fv, c_ff = FOX_WIDTH, 2 * FOX_WIDTH, 3 * FOX_WIDTH
    c_dq = c_ff + N_FOX_HEADS
    c_gate = c_dq + 3 * 512
    c_dv = c_dq + 2 * 512
    wqk = jnp.concatenate([wi[:, :c_fv], wi[:, c_dq:c_dv]], axis=1).astype(BF16)
    wvt = jnp.concatenate([wi[:, c_fv:c_ff], wi[:, c_dv:c_gate]], axis=1).T.astype(BF16)
    wff_t = jnp.zeros((16, d), BF16).at[:N_FOX_HEADS].set(wi[:, c_ff:c_dq].T.astype(BF16))
    wgate = wi[:, c_gate:].astype(BF16)
    bf_col = jnp.zeros((16, 1), F32).at[:N_FOX_HEADS, 0].set(b_f[0])
    head_id = jnp.arange(MXU_TILE) // HEAD_DIM
    bd = (head_id[:, None] == head_id[None, :]).astype(BF16)
    gvecs = jnp.stack([jnp.tile(g[0], FOX_WIDTH // HEAD_DIM)
                       for g in (g_q_fox, g_k_fox, g_q_diff, g_k_diff)])
    lam_vecs = jnp.stack([lam_q1[0], lam_k1[0], lam_q2[0], lam_k2[0]])
    inv_freq = ROPE_THETA ** (-jnp.arange(0, HEAD_DIM, 2, dtype=F32) / HEAD_DIM)
    invf_col = inv_freq.reshape(HEAD_DIM // 2, 1)
    wr = jnp.zeros((d, 2 * LANES), F32)
    wr = wr.at[:, :N_GROUPS].set(w_router_group[0]).at[:, LANES:LANES + N_EXPERTS].set(w_router_expert[0])
    wr_hi = wr.astype(BF16)
    wr_lo = (wr - wr_hi.astype(F32)).astype(BF16)
    br = jnp.zeros((1, 2 * LANES), F32)
    br = br.at[0, :N_GROUPS].set(b_router_group[0]).at[0, LANES:LANES + N_EXPERTS].set(b_router_expert[0])

    x2 = x.reshape(t, d)
    mod3 = _ada(c, w_ada[0], b_ada[0]).reshape(b, 6, d)
    cos_t, sin_t = _rope_tables(positions.reshape(1, t), invf_col, _tile(t, 2048))

    fq, fk, fv_t, dq, dk, dv_t, g0, g1, lf_t = _inproj(
        x2, mod3, g_norm1, wqk, wvt, wff_t, wgate, bd, gvecs, bf_col, cos_t, sin_t, tm, s)
    dec = _cumsum(lf_t, b, s)
    r3 = lambda a: a.reshape(b, s, a.shape[-1])
    o_f = _fox_attention(r3(fq), r3(fk), fv_t, dec, tq)
    o_d = _diff_attention(r3(dq), r3(dk), dv_t, lam_vecs, g_subln.reshape(2 * HEAD_DIM, 1), tq)

    x1, h_packed, logits = _merge(
        o_f.reshape(t, FOX_WIDTH), o_d.reshape(t, DIFF_WIDTH), g0, g1, x2, mod3, g_norm2,
        w_proj_fox[0].astype(BF16), w_proj_diff[0].astype(BF16), w_out[0].astype(BF16),
        wr_hi, wr_lo, br, tm, s)
    ids, wts, counts = _route(logits, _tile(t, 2048))

    a = t * TOP_K
    n_blocks = -(-a // MOE_BLOCK) + N_EXPERTS
    p_rows = n_blocks * MOE_BLOCK
    cnt = counts[0, :N_EXPERTS].astype(I32)
    padded = ((cnt + MOE_BLOCK - 1) // MOE_BLOCK) * MOE_BLOCK
    pad_end = jnp.cumsum(padded).astype(I32)
    pad_start = pad_end - padded
    block_start = jnp.arange(n_blocks, dtype=I32) * MOE_BLOCK
    block_expert = jnp.minimum(
        jnp.sum((pad_end[None, :] <= block_start[:, None]).astype(I32), axis=1), N_EXPERTS - 1)
    n_used = (pad_end[-1:] // MOE_BLOCK).astype(I32)
    pstart_row = jnp.zeros((1, LANES), F32).at[0, :N_EXPERTS].set(pad_start.astype(F32))

    dest = _rank(ids, pstart_row, _tile(t, 1024)).reshape(a)
    xs = _dispatch(pad_end, padded, n_used, dest, h_packed, p_rows, _tile(t, 512))
    yb = _experts(block_expert, n_used, xs, w1[0], w3[0], w2[0])
    out = _combine(dest, wts, x1, mod3, yb, _tile(s, 256), s)
    return out.reshape(b, s, d)
```

```python
import functools
import math

import jax
import jax.numpy as jnp
from jax import lax
from jax.experimental import pallas as pl
from jax.experimental.pallas import tpu as pltpu

F32 = jnp.float32
BF16 = jnp.bfloat16
I32 = jnp.int32
U32 = jnp.uint32

D_MODEL = 1024
HEAD_DIM = 64
N_FOX_HEADS = 8
N_DIFF_HEADS = 4
FOX_WIDTH = 512
DIFF_WIDTH = 512
CHUNK = 64
CHUNK_SHIFT = CHUNK.bit_length() - 1
ROPE_THETA = 10000.0
N_GROUPS = 4
EXPERTS_PER_GROUP = 8
N_EXPERTS = 32
TOP_K = 2
D_EXPERT = 512
MOE_BLOCK = 512
EPS = 1e-6
NEG_INF = -1e30
LAM0 = 0.8 - 0.6 * math.exp(-0.3 * 0)
LOG2E = math.log2(math.e)
Q_SCALE = HEAD_DIM ** -0.5 * LOG2E

LANES = 128
MXU_TILE = 256
VMEM_LIMIT = 56 * 1024 * 1024

DMA_UNROLL = 8
ONES_ROWS = 16
ROW_CHAINS = 2

_NT = (((1,), (1,)), ((), ()))


def _cparams(*sem):
    return pltpu.CompilerParams(dimension_semantics=sem, vmem_limit_bytes=VMEM_LIMIT)


def _split3(x):
    hi = x.astype(BF16)
    r1 = x - hi.astype(F32)
    mid = r1.astype(BF16)
    lo = (r1 - mid.astype(F32)).astype(BF16)
    return hi, mid, lo


def _ada_kernel(c_ref, w_ref, b_ref, o_ref):
    c = c_ref[...]
    ca = (c * jax.nn.sigmoid(c)).astype(BF16)
    o_ref[...] = jnp.dot(ca, w_ref[...].astype(BF16), preferred_element_type=F32) + b_ref[...]


def _ada(c, w_ada, b_ada):
    b, d = c.shape
    n = w_ada.shape[1]
    tn = 1024
    return pl.pallas_call(
        _ada_kernel,
        out_shape=jax.ShapeDtypeStruct((b, n), F32),
        grid=(n // tn,),
        in_specs=[pl.BlockSpec((b, d), lambda j: (0, 0)),
                  pl.BlockSpec((d, tn), lambda j: (0, j)),
                  pl.BlockSpec((1, tn), lambda j: (0, j))],
        out_specs=pl.BlockSpec((b, tn), lambda j: (0, j)),
        compiler_params=_cparams("arbitrary"),
        name="ada",
    )(c, w_ada, b_ada.reshape(1, n))


def _rope_kernel(pos_ref, invf_ref, cos_ref, sin_ref):
    ang = invf_ref[...] * pos_ref[...].astype(F32)
    c = jnp.cos(ang)
    s = jnp.sin(ang)
    cos_ref[...] = jnp.concatenate([c, c, c, c], axis=0)
    sin_ref[...] = jnp.concatenate([-s, s, -s, s], axis=0)


def _rope_tables(pos_row, invf_col, tm):
    t = pos_row.shape[1]
    tbl = jax.ShapeDtypeStruct((LANES, t), F32)
    return pl.pallas_call(
        _rope_kernel,
        out_shape=(tbl, tbl),
        grid=(t // tm,),
        in_specs=[pl.BlockSpec((1, tm), lambda i: (0, i)),
                  pl.BlockSpec((HEAD_DIM // 2, 1), lambda i: (0, 0))],
        out_specs=(pl.BlockSpec((LANES, tm), lambda i: (0, i)),
                   pl.BlockSpec((LANES, tm), lambda i: (0, i))),
        compiler_params=_cparams("arbitrary"),
        name="rope",
    )(pos_row, invf_col)


def _rms_rows(x, g):
    return x * lax.rsqrt(jnp.mean(x * x, axis=-1, keepdims=True) + EPS) * g


def _head_norm(z, bd, g):
    zz = (z * z).astype(BF16)
    n = bd.shape[0]
    ss = jnp.concatenate([jnp.dot(zz[:, j:j + n], bd, preferred_element_type=F32)
                          for j in range(0, z.shape[1], n)], axis=1)
    return z * lax.rsqrt(ss * (1.0 / HEAD_DIM) + EPS) * g


def _rotate_half(z):
    n = z.shape[-1]
    lane = lax.broadcasted_iota(I32, z.shape, 1)
    fwd = pltpu.roll(z, n - HEAD_DIM // 2, 1)
    bwd = pltpu.roll(z, HEAD_DIM // 2, 1)
    return jnp.where((lane & (HEAD_DIM - 1)) < HEAD_DIM // 2, fwd, bwd)


def _inproj_kernel(x_ref, mod_ref, g1_ref, wqk_ref, wvt_ref, wff_ref, wg_ref, bd_ref, gv_ref, bf_ref,
                   cos_ref, sin_ref,
                   fq_ref, fk_ref, fv_ref, dq_ref, dk_ref, dv_ref, g0_ref, g1o_ref, lf_ref):
    x = x_ref[...]
    sh1 = mod_ref[0, 0:1, :]
    sc1 = mod_ref[0, 1:2, :]
    h = (_rms_rows(x, g1_ref[...]) * (1.0 + sc1) + sh1).astype(BF16)
    bd = bd_ref[...]
    w = FOX_WIDTH

    def proj(j):
        return jnp.dot(h, wqk_ref[:, j * w:(j + 1) * w], preferred_element_type=F32)

    def proj_t(j):
        return lax.dot_general(wvt_ref[j * w:(j + 1) * w, :], h, _NT, preferred_element_type=F32)

    fq_ref[...] = (_head_norm(proj(0), bd, gv_ref[0:1, :]) * Q_SCALE).astype(BF16)
    fk_ref[...] = _head_norm(proj(1), bd, gv_ref[1:2, :]).astype(BF16)
    fv_ref[...] = proj_t(0).astype(BF16)

    cos = jnp.concatenate([cos_ref[...].T] * (w // LANES), axis=1)
    sin = jnp.concatenate([sin_ref[...].T] * (w // LANES), axis=1)
    qn = _head_norm(proj(2), bd, gv_ref[2:3, :])
    dq_ref[...] = ((qn * cos + _rotate_half(qn) * sin) * Q_SCALE).astype(BF16)
    kn = _head_norm(proj(3), bd, gv_ref[3:4, :])
    dk_ref[...] = (kn * cos + _rotate_half(kn) * sin).astype(BF16)
    dv_ref[...] = proj_t(1).astype(BF16)

    for j in range(2):
        zg = jnp.dot(h, wg_ref[:, j * D_MODEL:(j + 1) * D_MODEL], preferred_element_type=F32)
        (g0_ref, g1o_ref)[j][...] = jax.nn.sigmoid(zg).astype(BF16)

    zf = lax.dot_general(wff_ref[...], h, _NT, preferred_element_type=F32) + bf_ref[...]
    lf_ref[...] = jnp.minimum(zf, 0.0) - jnp.log1p(jnp.exp(-jnp.abs(zf)))


def _inproj(x2, mod3, g1, wqk, wvt, wff_t, wgate, bd, gvecs, bf_col, cos_t, sin_t, tm, seq):
    t, d = x2.shape
    tps = seq // tm
    row = lambda i: (i, 0)
    full = lambda i: (0, 0)
    bsd = lambda n: jax.ShapeDtypeStruct((t, n), BF16)
    bsd_t = jax.ShapeDtypeStruct((512, t), BF16)
    tok = pl.BlockSpec((tm, 512), row)
    tok_t = pl.BlockSpec((512, tm), lambda i: (0, i))
    return pl.pallas_call(
        _inproj_kernel,
        out_shape=(bsd(512), bsd(512), bsd_t, bsd(512), bsd(512), bsd_t, bsd(d), bsd(d),
                   jax.ShapeDtypeStruct((16, t), F32)),
        grid=(t // tm,),
        in_specs=[pl.BlockSpec((tm, d), row),
                  pl.BlockSpec((1, 6, d), lambda i: (i // tps, 0, 0)),
                  pl.BlockSpec((1, d), full),
                  pl.BlockSpec(wqk.shape, full),
                  pl.BlockSpec(wvt.shape, full),
                  pl.BlockSpec(wff_t.shape, full),
                  pl.BlockSpec(wgate.shape, full),
                  pl.BlockSpec(bd.shape, full),
                  pl.BlockSpec(gvecs.shape, full),
                  pl.BlockSpec(bf_col.shape, full),
                  pl.BlockSpec((LANES, tm), lambda i: (0, i)),
                  pl.BlockSpec((LANES, tm), lambda i: (0, i))],
        out_specs=(tok, tok, tok_t, tok, tok, tok_t, pl.BlockSpec((tm, d), row), pl.BlockSpec((tm, d), row),
                   pl.BlockSpec((16, tm), lambda i: (0, i))),
        compiler_params=_cparams("arbitrary"),
        name="inproj",
    )(x2, mod3, g1, wqk, wvt, wff_t, wgate, bd, gvecs, bf_col, cos_t, sin_t)


def _cumsum_kernel(lf_ref, tri_ref, eye_ref, sel_ref, ones_ref, dec_ref, *, seq):
    tri = tri_ref[...]
    eye = eye_ref[...]
    sel = sel_ref[...]
    n_blk = seq // LANES
    split = []
    for blk in range(n_blk):
        split += list(_split3(lf_ref[:, blk * LANES:(blk + 1) * LANES]))
    loc = jnp.dot(jnp.concatenate(split, axis=0), tri, preferred_element_type=F32)
    carry = jnp.zeros((16, 1), F32)
    cols = []
    for blk in range(n_blk):
        r = 3 * 16 * blk
        local = loc[r:r + 16] + loc[r + 16:r + 32] + loc[r + 32:r + 48]
        c = local + carry
        carry = carry + local[:, LANES - 1:LANES]
        pieces = jnp.concatenate(list(_split3(c * LOG2E)) + [jnp.zeros((16, LANES), BF16)], axis=0)
        cols.append(lax.dot_general(eye, pieces, _NT, preferred_element_type=F32).astype(BF16))
    dec_ref[0] = (jnp.dot(jnp.concatenate(cols, axis=0), sel, preferred_element_type=F32)
                  + ones_ref[...]).astype(BF16)


def _cumsum(lf_t, batch, seq):
    tri = (jnp.arange(LANES)[:, None] <= jnp.arange(LANES)[None, :]).astype(BF16)
    eye = jnp.eye(LANES, dtype=BF16)
    j, h = jnp.meshgrid(jnp.arange(3), jnp.arange(N_FOX_HEADS), indexing="ij")
    rows = (16 * j + h).reshape(-1)
    sel = jnp.zeros((4 * 16, LANES), F32)
    sel = sel.at[rows, (8 * h + j).reshape(-1)].set(1.0)
    sel = sel.at[rows, (HEAD_DIM + 8 * h + 3 + j).reshape(-1)].set(-1.0).astype(BF16)
    ones = jnp.zeros((1, LANES), F32)
    ones = ones.at[0, (8 * h + 3 + j).reshape(-1)].set(1.0).at[0, (HEAD_DIM + 8 * h + j).reshape(-1)].set(1.0)
    return pl.pallas_call(
        functools.partial(_cumsum_kernel, seq=seq),
        out_shape=jax.ShapeDtypeStruct((batch, seq, LANES), BF16),
        grid=(batch,),
        in_specs=[pl.BlockSpec((16, seq), lambda b: (0, b)),
                  pl.BlockSpec((LANES, LANES), lambda b: (0, 0)),
                  pl.BlockSpec((LANES, LANES), lambda b: (0, 0)),
                  pl.BlockSpec(sel.shape, lambda b: (0, 0)),
                  pl.BlockSpec((1, LANES), lambda b: (0, 0))],
        out_specs=pl.BlockSpec((1, seq, LANES), lambda b: (b, 0, 0)),
        compiler_params=_cparams("arbitrary"),
        name="cumsum",
    )(lf_t, tri, eye, sel, ones)


def _attend(n_chains, nq, q_of, k_of, vt_of, diag_mask, finalize, s_a, s_b, m_ref, acc_ref):
    n_steps = nq * (nq + 1) // 2

    def produce(qi, ki, s_ref):
        for c in range(n_chains):
            s_ref[c] = lax.dot_general(k_of(c, ki), q_of(c, qi), _NT, preferred_element_type=F32)

    def consume(qi, ki, s_ref, masked):
        par = lax.bitwise_and(qi, 1)
        tk, tq = diag_mask.shape
        half = tq // 2
        parts = [(slice(0, half), slice(0, tk // 2)), (slice(half, tq), slice(0, tk))] if masked \
            else [(slice(0, tq), slice(0, tk))]
        for c in range(n_chains):
            for cols, keys in parts:
                s = s_ref[c, keys, cols]
                if masked:
                    s = jnp.where(diag_mask[keys, cols], s, NEG_INF)
                m = jnp.where(ki == 0, NEG_INF, m_ref[par, c, :, cols])
                m_new = jnp.maximum(m, jnp.max(s, axis=0, keepdims=True))
                p = jnp.exp2(s - m_new).astype(BF16)
                acc_ref[par, c, :, cols] = (jnp.exp2(m - m_new) * acc_ref[par, c, :, cols]
                                            + jnp.dot(vt_of(c, ki)[:, keys], p, preferred_element_type=F32))
                m_ref[par, c, :, cols] = m_new

    def advance(qi, ki):
        last = (ki == qi).astype(I32)
        return qi + last, (ki + 1) * (1 - last)

    def two_steps(qi, ki, has_next):
        q1, k1 = advance(qi, ki)
        q2, k2 = advance(q1, k1)
        d0 = ki == qi
        d1 = k1 == q1

        def block(mask0, mask1):
            produce(q1, k1, s_b)
            consume(qi, ki, s_a, mask0)
            if has_next:
                produce(q2, k2, s_a)
            consume(q1, k1, s_b, mask1)

        pl.when(d0)(lambda: block(True, False))
        pl.when(d1)(lambda: block(False, True))
        pl.when(jnp.logical_not(jnp.logical_or(d0, d1)))(lambda: block(False, False))
        pl.when(d0)(lambda: finalize(qi))
        pl.when(d1)(lambda: finalize(q1))
        return q2, k2

    m_ref[...] = jnp.full(m_ref.shape, NEG_INF, F32)
    acc_ref[...] = jnp.zeros(acc_ref.shape, F32)
    produce(0, 0, s_a)
    n_pairs = (n_steps - 1) // 2
    qi, ki = lax.fori_loop(0, n_pairs, lambda _, carry: two_steps(*carry, True),
                           (jnp.int32(0), jnp.int32(0)))
    if n_steps - 2 * n_pairs == 2:
        two_steps(qi, ki, False)
    else:
        consume(qi, ki, s_a, True)
        finalize(qi)


def _attn_scratch(tq, rows):
    return [pltpu.VMEM((2, tq, tq), F32), pltpu.VMEM((2, tq, tq), F32),
            pltpu.VMEM((2, 2, 1, tq), F32), pltpu.VMEM((2, 2, rows, tq), F32)]


def _fox_kernel(q_ref, k_ref, vt_ref, dec_ref, o_ref, qaug_ref, kaug_ref, vtaug_ref, s_a, s_b,
                m_ref, acc_ref, *, tq):
    hp = pl.program_id(1)
    seq = q_ref.shape[1]
    nq = seq // tq
    qdec = dec_ref[0, :, :HEAD_DIM]
    kdec = dec_ref[0, :, HEAD_DIM:].astype(F32)
    lane_head = lax.shift_right_logical(lax.broadcasted_iota(I32, kdec.shape, 1), 3)
    ones = jnp.ones((ONES_ROWS, tq), BF16)
    for hh in range(2):
        hs = slice(hh * HEAD_DIM, (hh + 1) * HEAD_DIM)
        kd = jnp.where(lane_head == 2 * hp + hh, kdec, 0.0).astype(BF16)
        kaug_ref[hh] = jnp.concatenate([k_ref[0, :, hs], kd], axis=-1)
        qaug_ref[hh] = jnp.concatenate([q_ref[0, :, hs], qdec], axis=-1)
        for j in range(nq):
            vtaug_ref[hh, j] = jnp.concatenate([vt_ref[hs, j * tq:(j + 1) * tq], ones], axis=0)

    key = lax.broadcasted_iota(I32, (tq, tq), 0)
    qry = lax.broadcasted_iota(I32, (tq, tq), 1)

    def tile(ref, c, i):
        return ref[c, pl.ds(pl.multiple_of(i * tq, tq), tq), :]

    def finalize(qi):
        outs = []
        for c in range(2):
            acc = acc_ref[lax.bitwise_and(qi, 1), c]
            outs.append(acc[:HEAD_DIM] / acc[HEAD_DIM:HEAD_DIM + 1])
        o_t = jnp.concatenate(outs, axis=0)
        o_ref[0, pl.ds(pl.multiple_of(qi * tq, tq), tq), :] = o_t.T.astype(BF16)

    _attend(2, nq, functools.partial(tile, qaug_ref), functools.partial(tile, kaug_ref),
            lambda c, ki: vtaug_ref[c, ki], key <= qry, finalize, s_a, s_b, m_ref, acc_ref)


def _fox_attention(fq, fk, fv_t, dec, tq):
    b, s, _ = fq.shape
    head_pair = pl.BlockSpec((1, s, LANES), lambda bi, hp: (bi, 0, hp))
    return pl.pallas_call(
        functools.partial(_fox_kernel, tq=tq),
        out_shape=jax.ShapeDtypeStruct((b, s, FOX_WIDTH), BF16),
        grid=(b, N_FOX_HEADS // 2),
        in_specs=[head_pair, head_pair,
                  pl.BlockSpec((LANES, s), lambda bi, hp: (hp, bi)),
                  pl.BlockSpec((1, s, LANES), lambda bi, hp: (bi, 0, 0))],
        out_specs=head_pair,
        scratch_shapes=[pltpu.VMEM((2, s, 2 * HEAD_DIM), BF16), pltpu.VMEM((2, s, 2 * HEAD_DIM), BF16),
                        pltpu.VMEM((2, s // tq, HEAD_DIM + ONES_ROWS, tq), BF16)]
        + _attn_scratch(tq, HEAD_DIM + ONES_ROWS),
        compiler_params=_cparams("arbitrary", "arbitrary"),
        name="fox",
    )(fq, fk, fv_t, dec)


def _diff_kernel(q_ref, k_ref, vt_ref, lam_ref, gs_ref, o_ref, vtaug_ref, s_a, s_b, m_ref, acc_ref, *, tq):
    dv = 2 * HEAD_DIM
    seq = q_ref.shape[1]
    nq = seq // tq
    ones = jnp.ones((ONES_ROWS, tq), BF16)
    for j in range(nq):
        vtaug_ref[j] = jnp.concatenate([vt_ref[:, j * tq:(j + 1) * tq], ones], axis=0)

    key = lax.broadcasted_iota(I32, (tq, tq), 0)
    qry = lax.broadcasted_iota(I32, (tq, tq), 1)
    chunk_causal = lax.shift_right_logical(key, CHUNK_SHIFT) <= lax.shift_right_logical(qry, CHUNK_SHIFT)
    lv = lam_ref[...]
    lam = (jnp.exp(jnp.sum(lv[0:1] * lv[1:2], axis=-1, keepdims=True))
           - jnp.exp(jnp.sum(lv[2:3] * lv[3:4], axis=-1, keepdims=True)) + LAM0)

    def rows(i):
        return pl.ds(pl.multiple_of(i * tq, tq), tq)

    def q_of(c, qi):
        return q_ref[0, rows(qi), c * HEAD_DIM:(c + 1) * HEAD_DIM]

    def k_of(c, ki):
        return k_ref[0, rows(ki), c * HEAD_DIM:(c + 1) * HEAD_DIM]

    def finalize(qi):
        par = lax.bitwise_and(qi, 1)
        outs = [acc_ref[par, c][:dv] / acc_ref[par, c][dv:dv + 1] for c in range(2)]
        o_t = outs[0] - lam * outs[1]
        inv = lax.rsqrt(jnp.mean(o_t * o_t, axis=0, keepdims=True) + EPS)
        o_t = o_t * inv * gs_ref[...] * (1.0 - LAM0)
        o_ref[0, rows(qi), :] = o_t.T.astype(BF16)

    _attend(2, nq, q_of, k_of, lambda c, ki: vtaug_ref[ki], chunk_causal, finalize,
            s_a, s_b, m_ref, acc_ref)


def _diff_attention(dq, dk, dv_t, lam_vecs, g_subln_col, tq):
    b, s, _ = dq.shape
    head = pl.BlockSpec((1, s, LANES), lambda bi, h: (bi, 0, h))
    rows = 2 * HEAD_DIM + ONES_ROWS
    return pl.pallas_call(
        functools.partial(_diff_kernel, tq=tq),
        out_shape=jax.ShapeDtypeStruct((b, s, DIFF_WIDTH), BF16),
        grid=(b, N_DIFF_HEADS),
        in_specs=[head, head,
                  pl.BlockSpec((LANES, s), lambda bi, h: (h, bi)),
                  pl.BlockSpec((4, HEAD_DIM), lambda bi, h: (0, 0)),
                  pl.BlockSpec((2 * HEAD_DIM, 1), lambda bi, h: (0, 0))],
        out_specs=head,
        scratch_shapes=[pltpu.VMEM((s // tq, rows, tq), BF16)] + _attn_scratch(tq, rows),
        compiler_params=_cparams("arbitrary", "arbitrary"),
        name="diff",
    )(dq, dk, dv_t, lam_vecs, g_subln_col)


def _merge_kernel(of_ref, od_ref, g0_ref, g1_ref, x_ref, mod_ref, g2_ref, wpf_ref, wpd_ref, wout_ref,
                  wrh_ref, wrl_ref, br_ref,
                  x1_ref, hp_ref, lg_ref):
    gt1 = mod_ref[0, 2:3, :]
    sh2 = mod_ref[0, 3:4, :]
    sc2 = mod_ref[0, 4:5, :]
    sub = x_ref.shape[0] // ROW_CHAINS
    for r in range(ROW_CHAINS):
        rows = pl.ds(r * sub, sub)
        a = jnp.dot(of_ref[rows, :], wpf_ref[...], preferred_element_type=F32)
        b = jnp.dot(od_ref[rows, :], wpd_ref[...], preferred_element_type=F32)
        merged = g0_ref[rows, :].astype(F32) * a + g1_ref[rows, :].astype(F32) * b
        x1 = x_ref[rows, :] + gt1 * jnp.dot(merged.astype(BF16), wout_ref[...], preferred_element_type=F32)
        x1_ref[rows, :] = x1
        h2 = _rms_rows(x1, g2_ref[...]) * (1.0 + sc2) + sh2

        hi = h2.astype(BF16)
        hf = hi.astype(F32)
        half = D_MODEL // 2
        lo_bits = pltpu.bitcast(hf[:, :half], U32) >> 16
        hi_bits = pltpu.bitcast(hf[:, half:], U32) & jnp.uint32(0xFFFF0000)
        hp_ref[rows, :] = lo_bits | hi_bits

        lo = (h2 - hf).astype(BF16)
        wrh = wrh_ref[...]
        lg_ref[rows, :] = (jnp.dot(hi, wrh, preferred_element_type=F32)
                           + jnp.dot(lo, wrh, preferred_element_type=F32)
                           + jnp.dot(hi, wrl_ref[...], preferred_element_type=F32)) + br_ref[...]


def _route_kernel(lg_ref, ids_ref, wts_ref, cnt_ref):
    i = pl.program_id(0)
    lg = lg_ref[:, :LANES]
    le = lg_ref[:, LANES:]
    lane = lax.broadcasted_iota(I32, lg.shape, 1)
    big = jnp.int32(1 << 20)

    def softmax_masked(z, mask):
        zm = jnp.where(mask, z, -jnp.inf)
        e = jnp.exp(zm - jnp.max(zm, axis=-1, keepdims=True))
        return e / jnp.sum(e, axis=-1, keepdims=True)

    def top1(p, mask):
        pm = jnp.where(mask, p, -1.0)
        best = jnp.max(pm, axis=-1, keepdims=True)
        idx = jnp.min(jnp.where(pm == best, lane, big), axis=-1, keepdims=True)
        return best, idx

    gmask = lane < N_GROUPS
    g_w, g_idx = top1(softmax_masked(lg, gmask), gmask)
    emask = (lane >> 3) == g_idx
    p_exp = softmax_masked(le, emask)
    p1, i1 = top1(p_exp, emask)
    p2, i2 = top1(p_exp, emask & (lane != i1))
    denom = p1 + p2
    w1 = g_w * (p1 / denom)
    w2 = g_w * (p2 / denom)
    ids_ref[...] = jnp.where(lane == 0, i1, i2)[:, :TOP_K]
    wts_ref[...] = jnp.where(lane == 0, w1, w2)[:, :TOP_K]

    @pl.when(i == 0)
    def _():
        cnt_ref[...] = jnp.zeros_like(cnt_ref)

    cnt_ref[...] += jnp.sum(((lane == i1) | (lane == i2)).astype(F32), axis=0, keepdims=True)


def _route(logits, tr):
    t = logits.shape[0]
    return pl.pallas_call(
        _route_kernel,
        out_shape=(jax.ShapeDtypeStruct((t, TOP_K), I32),
                   jax.ShapeDtypeStruct((t, TOP_K), F32),
                   jax.ShapeDtypeStruct((1, LANES), F32)),
        grid=(t // tr,),
        in_specs=[pl.BlockSpec((tr, 2 * LANES), lambda i: (i, 0))],
        out_specs=(pl.BlockSpec((tr, TOP_K), lambda i: (i, 0)),
                   pl.BlockSpec((tr, TOP_K), lambda i: (i, 0)),
                   pl.BlockSpec((1, LANES), lambda i: (0, 0))),
        compiler_params=_cparams("arbitrary"),
        name="route",
    )(logits)


def _merge(o_f, o_d, g0, g1, x2, mod3, g2, wpf, wpd, wout, wr_hi, wr_lo, br, tm, seq):
    t, d = x2.shape
    tps = seq // tm
    row = lambda i: (i, 0)
    full = lambda i: (0, 0)
    return pl.pallas_call(
        _merge_kernel,
        out_shape=(jax.ShapeDtypeStruct((t, d), F32),
                   jax.ShapeDtypeStruct((t, d // 2), U32),
                   jax.ShapeDtypeStruct((t, 2 * LANES), F32)),
        grid=(t // tm,),
        in_specs=[pl.BlockSpec((tm, 512), row), pl.BlockSpec((tm, 512), row),
                  pl.BlockSpec((tm, d), row), pl.BlockSpec((tm, d), row),
                  pl.BlockSpec((tm, d), row),
                  pl.BlockSpec((1, 6, d), lambda i: (i // tps, 0, 0)),
                  pl.BlockSpec((1, d), full),
                  pl.BlockSpec(wpf.shape, full), pl.BlockSpec(wpd.shape, full),
                  pl.BlockSpec(wout.shape, full),
                  pl.BlockSpec(wr_hi.shape, full), pl.BlockSpec(wr_lo.shape, full),
                  pl.BlockSpec(br.shape, full)],
        out_specs=(pl.BlockSpec((tm, d), row), pl.BlockSpec((tm, d // 2), row),
                   pl.BlockSpec((tm, 2 * LANES), row)),
        compiler_params=_cparams("arbitrary"),
        name="merge",
    )(o_f, o_d, g0, g1, x2, mod3, g2, wpf, wpd, wout, wr_hi, wr_lo, br)


def _rank_kernel(ids_ref, pstart_ref, ltri_ref, dest_ref, carry_ref):
    i = pl.program_id(0)

    @pl.when(i == 0)
    def _():
        carry_ref[...] = jnp.zeros_like(carry_ref)

    ids = ids_ref[...]
    lane = lax.broadcasted_iota(I32, (ids.shape[0], LANES), 1)
    oh0 = lane == ids[:, 0:1]
    oh1 = lane == ids[:, 1:2]
    both = (oh0 | oh1).astype(BF16)
    before = jnp.dot(ltri_ref[...], both, preferred_element_type=F32)
    base = before + carry_ref[...] + pstart_ref[...]
    d0 = jnp.sum(jnp.where(oh0, base, 0.0), axis=-1, keepdims=True)
    d1 = jnp.sum(jnp.where(oh1, base, 0.0), axis=-1, keepdims=True)
    dest_ref[...] = jnp.where(lane == 0, d0, d1)[:, :TOP_K].astype(I32)
    carry_ref[...] += jnp.sum(both.astype(F32), axis=0, keepdims=True)


def _rank(ids, pad_start_row, tt):
    t = ids.shape[0]
    ltri = (jnp.arange(tt)[:, None] > jnp.arange(tt)[None, :]).astype(BF16)
    return pl.pallas_call(
        _rank_kernel,
        out_shape=jax.ShapeDtypeStruct((t, TOP_K), I32),
        grid=(t // tt,),
        in_specs=[pl.BlockSpec((tt, TOP_K), lambda i: (i, 0)),
                  pl.BlockSpec((1, LANES), lambda i: (0, 0)),
                  pl.BlockSpec((tt, tt), lambda i: (0, 0))],
        out_specs=pl.BlockSpec((tt, TOP_K), lambda i: (i, 0)),
        scratch_shapes=[pltpu.VMEM((1, LANES), F32)],
        compiler_params=_cparams("arbitrary"),
        name="rank",
    )(ids, pad_start_row, ltri)


def _row_copy(src_ref, s, dst_ref, d, sem):
    return pltpu.make_async_copy(src_ref.at[pl.ds(s, 1)], dst_ref.at[pl.ds(d, 1)], sem)


def _dispatch_kernel(pend_ref, padded_ref, nused_ref, dest_ref, h_ref, xs_ref, zero_ref, sem, zsem,
                     *, td, n_blocks):
    i = pl.program_id(0)

    @pl.when(i == 0)
    def _():
        zero_ref[...] = jnp.zeros_like(zero_ref)

        def zero_block(start):
            cp = pltpu.make_async_copy(
                zero_ref, xs_ref.at[pl.ds(pl.multiple_of(start, MOE_BLOCK), MOE_BLOCK)], zsem)
            cp.start()
            cp.wait()

        for e in range(N_EXPERTS):
            @pl.when(padded_ref[e] > 0)
            def _(e=e):
                zero_block(pend_ref[e] - MOE_BLOCK)

            @pl.when(nused_ref[0] + e < n_blocks)
            def _(e=e):
                zero_block((nused_ref[0] + e) * MOE_BLOCK)

    def issue(t, _):
        _row_copy(h_ref, t, xs_ref, dest_ref[2 * t], sem).start(priority=0)
        _row_copy(h_ref, t, xs_ref, dest_ref[2 * t + 1], sem).start(priority=1)
        return 0

    lax.fori_loop(0, td, issue, 0, unroll=DMA_UNROLL)

    def drain(t, _):
        _row_copy(h_ref, 0, xs_ref, 0, sem).wait()
        _row_copy(h_ref, 0, xs_ref, 0, sem).wait()
        return 0

    lax.fori_loop(0, td, drain, 0, unroll=DMA_UNROLL)


def _dispatch(pad_end, padded, n_used, dest_flat, h_packed, p_rows, td):
    t, w = h_packed.shape
    grid_spec = pltpu.PrefetchScalarGridSpec(
        num_scalar_prefetch=3,
        grid=(t // td,),
        in_specs=[pl.BlockSpec((TOP_K * td,), lambda i, *_: (i,), memory_space=pltpu.SMEM),
                  pl.BlockSpec((td, w), lambda i, *_: (i, 0))],
        out_specs=pl.BlockSpec(memory_space=pl.ANY),
        scratch_shapes=[pltpu.VMEM((MOE_BLOCK, w), U32),
                        pltpu.SemaphoreType.DMA(()), pltpu.SemaphoreType.DMA(())],
    )
    return pl.pallas_call(
        functools.partial(_dispatch_kernel, td=td, n_blocks=p_rows // MOE_BLOCK),
        out_shape=jax.ShapeDtypeStruct((p_rows, w), U32),
        grid_spec=grid_spec,
        compiler_params=_cparams("arbitrary"),
        name="dispatch",
    )(pad_end, padded, n_used, dest_flat, h_packed)


def _expert_kernel(be_ref, nused_ref, xs_ref, w1_ref, w3_ref, w2_ref, y_ref, w1b_ref, w3b_ref, w2b_ref):
    p = pl.program_id(0)
    active = p < nused_ref[0]
    new_expert = jnp.logical_or(p == 0, be_ref[p] != be_ref[jnp.maximum(p - 1, 0)])

    @pl.when(jnp.logical_and(active, new_expert))
    def _():
        w1b_ref[...] = w1_ref[0].astype(BF16)
        w3b_ref[...] = w3_ref[0].astype(BF16)
        w2b_ref[...] = w2_ref[0].astype(BF16)

    @pl.when(active)
    def _():
        words = xs_ref[...]
        half = D_MODEL // 2
        xa = pltpu.bitcast(words << 16, F32).astype(BF16)
        xb = pltpu.bitcast(words & jnp.uint32(0xFFFF0000), F32).astype(BF16)

        def up(w_ref):
            return (jnp.dot(xa, w_ref[:half, :], preferred_element_type=F32)
                    + jnp.dot(xb, w_ref[half:, :], preferred_element_type=F32))

        g = up(w1b_ref)
        u = up(w3b_ref)
        act = (g * jax.nn.sigmoid(g) * u).astype(BF16)
        y_ref[...] = jnp.dot(act, w2b_ref[...], preferred_element_type=F32)

    @pl.when(p >= nused_ref[0])
    def _():
        y_ref[...] = jnp.zeros_like(y_ref)


def _experts(block_expert, n_used, xs, w1, w3, w2):
    p_rows, w = xs.shape
    nb = p_rows // MOE_BLOCK
    d = w1.shape[1]
    blk = lambda p, be, nu: (p, 0)
    wsel = lambda p, be, nu: (be[jnp.minimum(p, nu[0] - 1)], 0, 0)
    grid_spec = pltpu.PrefetchScalarGridSpec(
        num_scalar_prefetch=2,
        grid=(nb,),
        in_specs=[pl.BlockSpec((MOE_BLOCK, w), blk),
                  pl.BlockSpec((1, d, D_EXPERT), wsel),
                  pl.BlockSpec((1, d, D_EXPERT), wsel),
                  pl.BlockSpec((1, D_EXPERT, d), wsel)],
        out_specs=pl.BlockSpec((MOE_BLOCK, d), blk),
        scratch_shapes=[pltpu.VMEM((d, D_EXPERT), BF16), pltpu.VMEM((d, D_EXPERT), BF16),
                        pltpu.VMEM((D_EXPERT, d), BF16)],
    )
    return pl.pallas_call(
        _expert_kernel,
        out_shape=jax.ShapeDtypeStruct((p_rows, d), F32),
        grid_spec=grid_spec,
        compiler_params=_cparams("arbitrary"),
        name="experts",
    )(block_expert, n_used, xs, w1, w3, w2)


def _combine_kernel(dcur_ref, dnxt_ref, wts_ref, x1_ref, mod_ref, yb_ref, o_ref, ybuf_ref, sems,
                    *, tc, n_tiles):
    i = pl.program_id(0)
    slot = lax.bitwise_and(i, 1)

    def start_gather(dest_ref, s):
        def issue(t, _):
            _row_copy(yb_ref, dest_ref[2 * t], ybuf_ref.at[s].at[0], t, sems.at[s]).start(priority=0)
            _row_copy(yb_ref, dest_ref[2 * t + 1], ybuf_ref.at[s].at[1], t, sems.at[s]).start(priority=1)
            return 0

        lax.fori_loop(0, tc, issue, 0, unroll=DMA_UNROLL)

    @pl.when(i == 0)
    def _():
        start_gather(dcur_ref, slot)

    @pl.when(i + 1 < n_tiles)
    def _():
        start_gather(dnxt_ref, 1 - slot)

    def drain(t, _):
        _row_copy(yb_ref, 0, ybuf_ref.at[slot].at[0], 0, sems.at[slot]).wait()
        _row_copy(yb_ref, 0, ybuf_ref.at[slot].at[1], 0, sems.at[slot]).wait()
        return 0

    lax.fori_loop(0, tc, drain, 0, unroll=DMA_UNROLL)
    gt2 = mod_ref[0, 5:6, :]
    wts = wts_ref[...]
    y = ybuf_ref[slot, 0] * wts[:, 0:1] + ybuf_ref[slot, 1] * wts[:, 1:2]
    o_ref[...] = x1_ref[...] + gt2 * y


def _combine(dest_flat, wts, x1, mod3, yb, tc, seq):
    t, d = x1.shape
    tps = seq // tc
    n_tiles = t // tc
    dest_block = lambda off: pl.BlockSpec((TOP_K * tc,), lambda i: (jnp.minimum(i + off, n_tiles - 1),),
                                          memory_space=pltpu.SMEM)
    return pl.pallas_call(
        functools.partial(_combine_kernel, tc=tc, n_tiles=n_tiles),
        out_shape=jax.ShapeDtypeStruct((t, d), F32),
        grid=(n_tiles,),
        in_specs=[dest_block(0), dest_block(1),
                  pl.BlockSpec((tc, TOP_K), lambda i: (i, 0)),
                  pl.BlockSpec((tc, d), lambda i: (i, 0)),
                  pl.BlockSpec((1, 6, d), lambda i: (i // tps, 0, 0)),
                  pl.BlockSpec(memory_space=pl.ANY)],
        out_specs=pl.BlockSpec((tc, d), lambda i: (i, 0)),
        scratch_shapes=[pltpu.VMEM((2, 2, tc, d), F32), pltpu.SemaphoreType.DMA((2,))],
        compiler_params=_cparams("arbitrary"),
        name="combine",
    )(dest_flat, dest_flat, wts, x1, mod3, yb)


def _tile(n, pref):
    t = pref
    while n % t:
        t //= 2
    return t


def kernel(x, c, positions, w_ada, b_ada, g_norm1, w_in, b_f, g_q_fox, g_k_fox, g_q_diff, g_k_diff,
           lam_q1, lam_k1, lam_q2, lam_k2, g_subln, w_proj_fox, w_proj_diff, w_out, g_norm2,
           w_router_group, b_router_group, w_router_expert, b_router_expert, w1, w3, w2):
    b, s, d = x.shape
    t = b * s
    assert d == D_MODEL and w_ada.shape[0] == 1 and s % LANES == 0
    tm = _tile(s, 512)
    tq = _tile(s, 512)

    wi = w_in[0]
    c_fk, c_fv, c_ff = FOX_WIDTH, 2 * FOX_WIDTH, 3 * FOX_WIDTH
    c_dq = c_ff + N_FOX_HEADS
    c_gate = c_dq + 3 * 512
    c_dv = c_dq + 2 * 512
    wqk = jnp.concatenate([wi[:, :c_fv], wi[:, c_dq:c_dv]], axis=1).astype(BF16)
    wvt = jnp.concatenate([wi[:, c_fv:c_ff], wi[:, c_dv:c_gate]], axis=1).T.astype(BF16)
    wff_t = jnp.zeros((16, d), BF16).at[:N_FOX_HEADS].set(wi[:, c_ff:c_dq].T.astype(BF16))
    wgate = wi[:, c_gate:].astype(BF16)
    bf_col = jnp.zeros((16, 1), F32).at[:N_FOX_HEADS, 0].set(b_f[0])
    head_id = jnp.arange(MXU_TILE) // HEAD_DIM
    bd = (head_id[:, None] == head_id[None, :]).astype(BF16)
    gvecs = jnp.stack([jnp.tile(g[0], FOX_WIDTH // HEAD_DIM)
                       for g in (g_q_fox, g_k_fox, g_q_diff, g_k_diff)])
    lam_vecs = jnp.stack([lam_q1[0], lam_k1[0], lam_q2[0], lam_k2[0]])
    inv_freq = ROPE_THETA ** (-jnp.arange(0, HEAD_DIM, 2, dtype=F32) / HEAD_DIM)
    invf_col = inv_freq.reshape(HEAD_DIM // 2, 1)
    wr = jnp.zeros((d, 2 * LANES), F32)
    wr = wr.at[:, :N_GROUPS].set(w_router_group[0]).at[:, LANES:LANES + N_EXPERTS].set(w_router_expert[0])
    wr_hi = wr.astype(BF16)
    wr_lo = (wr - wr_hi.astype(F32)).astype(BF16)
    br = jnp.zeros((1, 2 * LANES), F32)
    br = br.at[0, :N_GROUPS].set(b_router_group[0]).at[0, LANES:LANES + N_EXPERTS].set(b_router_expert[0])

    x2 = x.reshape(t, d)
    mod3 = _ada(c, w_ada[0], b_ada[0]).reshape(b, 6, d)
    cos_t, sin_t = _rope_tables(positions.reshape(1, t), invf_col, _tile(t, 2048))

    fq, fk, fv_t, dq, dk, dv_t, g0, g1, lf_t = _inproj(
        x2, mod3, g_norm1, wqk, wvt, wff_t, wgate, bd, gvecs, bf_col, cos_t, sin_t, tm, s)
    dec = _cumsum(lf_t, b, s)
    r3 = lambda a: a.reshape(b, s, a.shape[-1])
    o_f = _fox_attention(r3(fq), r3(fk), fv_t, dec, tq)
    o_d = _diff_attention(r3(dq), r3(dk), dv_t, lam_vecs, g_subln.reshape(2 * HEAD_DIM, 1), tq)

    x1, h_packed, logits = _merge(
        o_f.reshape(t, FOX_WIDTH), o_d.reshape(t, DIFF_WIDTH), g0, g1, x2, mod3, g_norm2,
        w_proj_fox[0].astype(BF16), w_proj_diff[0].astype(BF16), w_out[0].astype(BF16),
        wr_hi, wr_lo, br, tm, s)
    ids, wts, counts = _route(logits, _tile(t, 2048))

    a = t * TOP_K
    n_blocks = -(-a // MOE_BLOCK) + N_EXPERTS
    p_rows = n_blocks * MOE_BLOCK
    cnt = counts[0, :N_EXPERTS].astype(I32)
    padded = ((cnt + MOE_BLOCK - 1) // MOE_BLOCK) * MOE_BLOCK
    pad_end = jnp.cumsum(padded).astype(I32)
    pad_start = pad_end - padded
    block_start = jnp.arange(n_blocks, dtype=I32) * MOE_BLOCK
    block_expert = jnp.minimum(
        jnp.sum((pad_end[None, :] <= block_start[:, None]).astype(I32), axis=1), N_EXPERTS - 1)
    n_used = (pad_end[-1:] // MOE_BLOCK).astype(I32)
    pstart_row = jnp.zeros((1, LANES), F32).at[0, :N_EXPERTS].set(pad_start.astype(F32))

    dest = _rank(ids, pstart_row, _tile(t, 1024)).reshape(a)
    xs = _dispatch(pad_end, padded, n_used, dest, h_packed, p_rows, _tile(t, 512))
    yb = _experts(block_expert, n_used, xs, w1[0], w3[0], w2[0])
    out = _combine(dest, wts, x1, mod3, yb, _tile(s, 256), s)
    return out.reshape(b, s, d)
```
